```python
import jax, jax.numpy as jnp
from jax import lax
import numpy as np

D_MODEL = 1024
BATCH = 32
SEQ = 256
DEPTH = 4
DEC_BATCH = 4
DEC_SEQ = 1024
PAST_LEN = 512

GRID_W = 64
N_HEADS = 8
HEAD_DIM = 64
KV_HEADS = 2
GROUP = N_HEADS // KV_HEADS
WINDOW = 128
Q_BLOCK = 128
RET_HEADS = 4
RET_DK = 128
RET_DV = 128
RET_CHUNK = 128
ATTN_W = N_HEADS * HEAD_DIM
KV_W = KV_HEADS * HEAD_DIM
RET_W = RET_HEADS * RET_DK
RET_VW = RET_HEADS * RET_DV
IN_WIDTH = ATTN_W + 2 * KV_W + 2 * RET_W + 3 * RET_VW + 2 * D_MODEL
D_FF = 4 * D_MODEL
ROPE_BASE = 10000.0
NORM_EPS = 1e-6
NEG_INF = -1e30

kernel_name = "hybrid_dit_swa_retention_step"


def rmsnorm(x, g):
    x32 = x.astype(jnp.float32)
    y = x32 * lax.rsqrt(jnp.mean(x32 * x32, axis=-1, keepdims=True) + NORM_EPS)
    return (y * g.astype(jnp.float32)).astype(x.dtype)


def head_norm(o, g):
    mu = jnp.mean(o, axis=-1, keepdims=True)
    var = jnp.mean(jnp.square(o - mu), axis=-1, keepdims=True)
    y = (o - mu) * lax.rsqrt(var + NORM_EPS)
    B, T, H, dv = o.shape
    return y.reshape(B, T, H * dv) * g.astype(jnp.float32)


def rope_2d(x, rows, cols):
    d = x.shape[-1]
    half = d // 2
    nf = half // 2
    inv = ROPE_BASE ** (-jnp.arange(nf, dtype=jnp.float32) / nf)

    def rot(xh, pos):
        ang = pos[:, None] * inv[None, :]
        ang = jnp.concatenate([ang, ang], axis=-1)[None, :, None, :]
        x1, x2 = xh[..., :nf], xh[..., nf:]
        return xh * jnp.cos(ang) + jnp.concatenate([-x2, x1], axis=-1) * jnp.sin(ang)

    x32 = x.astype(jnp.float32)
    out = jnp.concatenate([rot(x32[..., :half], rows), rot(x32[..., half:], cols)], axis=-1)
    return out.astype(x.dtype)


def context_attention(q, k, v, sink):
    B, L, H, hd = q.shape
    nb = L // Q_BLOCK
    qb = q.reshape(B, nb, Q_BLOCK, KV_HEADS, GROUP, hd).transpose(1, 0, 2, 3, 4, 5)
    sink_col = sink.astype(jnp.float32).reshape(KV_HEADS, GROUP)[None, :, :, None, None]
    scale = HEAD_DIM ** -0.5

    def block(qi):
        s = jnp.einsum('bqkgd,bskd->bkgqs', qi, k).astype(jnp.float32) * scale
        sk = jnp.broadcast_to(sink_col, s.shape[:-1] + (1,))
        p = jax.nn.softmax(jnp.concatenate([s, sk], axis=-1), axis=-1)
        return jnp.einsum('bkgqs,bskd->bqkgd', p[..., :L].astype(v.dtype), v)

    o = lax.map(block, qb)
    return o.transpose(1, 0, 2, 3, 4, 5).reshape(B, L, H * hd)


def latent_attention(q, k, v, ck, cv, sink):
    B, T, H, hd = q.shape
    Lc = ck.shape[2]
    nb = T // Q_BLOCK
    band = Q_BLOCK + 2 * WINDOW
    qg = q.reshape(B, T, KV_HEADS, GROUP, hd)
    pad = ((0, 0), (WINDOW, WINDOW), (0, 0), (0, 0))
    kp = jnp.pad(k, pad)
    vp = jnp.pad(v, pad)
    sink_col = sink.astype(jnp.float32).reshape(KV_HEADS, GROUP)[None, :, :, None, None]
    scale = HEAD_DIM ** -0.5

    def block(b):
        start = b * Q_BLOCK
        qi = lax.dynamic_slice_in_dim(qg, start, Q_BLOCK, axis=1)
        ki = lax.dynamic_slice_in_dim(kp, start, band, axis=1)
        vi = lax.dynamic_slice_in_dim(vp, start, band, axis=1)
        s_loc = jnp.einsum('bqkgd,bskd->bkgqs', qi, ki).astype(jnp.float32) * scale
        qpos = start + jnp.arange(Q_BLOCK)
        kpos = start - WINDOW + jnp.arange(band)
        valid = (kpos[None, :] >= 0) & (kpos[None, :] < T) & (jnp.abs(qpos[:, None] - kpos[None, :]) <= WINDOW)
        s_loc = jnp.where(valid, s_loc, NEG_INF)
        s_ctx = jnp.einsum('bqkgd,bksd->bkgqs', qi, ck).astype(jnp.float32) * scale
        sk = jnp.broadcast_to(sink_col, s_loc.shape[:-1] + (1,))
        p = jax.nn.softmax(jnp.concatenate([s_loc, s_ctx, sk], axis=-1), axis=-1)
        o = jnp.einsum('bkgqs,bskd->bqkgd', p[..., :band].astype(v.dtype), vi)
        o = o + jnp.einsum('bkgqs,bksd->bqkgd', p[..., band:band + Lc].astype(cv.dtype), cv)
        return o

    o = lax.map(block, jnp.arange(nb))
    return o.transpose(1, 0, 2, 3, 4, 5).reshape(B, T, H * hd)


def retention_chunkwise(q, k, v, gamma, s0):
    B, T, H, dk = q.shape
    dv = v.shape[-1]
    C = RET_CHUNK
    nc = T // C

    def to_chunks(x):
        return x.astype(jnp.float32).reshape(B, nc, C, H, x.shape[-1]).transpose(1, 0, 3, 2, 4)

    log_g = jnp.log(gamma.astype(jnp.float32))
    idx = jnp.arange(C, dtype=jnp.float32)
    diff = idx[:, None] - idx[None, :]
    intra = jnp.where(diff >= 0, jnp.exp(jnp.maximum(diff, 0.0)[None] * log_g[:, None, None]), 0.0)
    cross = jnp.exp((idx + 1.0)[None, :] * log_g[:, None])
    kdec = jnp.exp((C - 1.0 - idx)[None, :] * log_g[:, None])
    g_chunk = jnp.exp(C * log_g)

    def step(S, inp):
        qi, ki, vi = inp
        o_inter = jnp.einsum('bhcd,bhde->bhce', qi, S) * cross[None, :, :, None]
        sc = jnp.einsum('bhid,bhjd->bhij', qi, ki) * intra[None]
        o_intra = jnp.einsum('bhij,bhje->bhie', sc, vi)
        S_new = S * g_chunk[None, :, None, None] + jnp.einsum('bhjd,bhje->bhde', ki * kdec[None, :, :, None], vi)
        return S_new, o_inter + o_intra

    S, o = lax.scan(step, s0.astype(jnp.float32), (to_chunks(q), to_chunks(k), to_chunks(v)))
    o = o.transpose(1, 0, 3, 2, 4).reshape(B, T, H, dv)
    return o, S


def token_mixing(h, w_in, sink, dec_f, dec_b, gn_g, w_ba, w_br, w_o, ctx_k, ctx_v, s_f0, s_b0, grid):
    B, T, _ = h.shape
    p = h @ w_in
    sizes = (ATTN_W, KV_W, KV_W, RET_W, RET_W, RET_VW, RET_VW, RET_VW, D_MODEL, D_MODEL)
    offsets = np.cumsum(sizes)[:-1].tolist()
    q, k, v, rq, rk, rv, rgf, rgb, ga, gr = jnp.split(p, offsets, axis=-1)
    q = q.reshape(B, T, N_HEADS, HEAD_DIM)
    k = k.reshape(B, T, KV_HEADS, HEAD_DIM)
    v = v.reshape(B, T, KV_HEADS, HEAD_DIM)
    rq = rq.reshape(B, T, RET_HEADS, RET_DK)
    rk = rk.reshape(B, T, RET_HEADS, RET_DK)
    rv = rv.reshape(B, T, RET_HEADS, RET_DV)
    if grid is not None:
        rows, cols = grid
        q = rope_2d(q, rows, cols)
        k = rope_2d(k, rows, cols)
        rq = rope_2d(rq, rows, cols)
        rk = rope_2d(rk, rows, cols)
    if ctx_k is None:
        attn = context_attention(q, k, v, sink)
    else:
        attn = latent_attention(q, k, v, ctx_k, ctx_v, sink)
    rk = rk * (RET_DK ** -0.5)
    if s_f0 is None:
        s_f0 = jnp.zeros((B, RET_HEADS, RET_DK, RET_DV), jnp.float32)
        s_b0 = jnp.zeros((B, RET_HEADS, RET_DK, RET_DV), jnp.float32)
    o_f, S_f = retention_chunkwise(rq, rk, rv, jax.nn.sigmoid(dec_f.astype(jnp.float32)), s_f0)
    o_b, S_b = retention_chunkwise(jnp.flip(rq, 1), jnp.flip(rk, 1), jnp.flip(rv, 1),
                                   jax.nn.sigmoid(dec_b.astype(jnp.float32)), s_b0)
    o_b = jnp.flip(o_b, 1)
    ret = (jax.nn.silu(rgf.astype(jnp.float32)) * head_norm(o_f, gn_g)
           + jax.nn.silu(rgb.astype(jnp.float32)) * head_norm(o_b, gn_g)).astype(h.dtype)
    merged = jax.nn.sigmoid(ga) * (attn @ w_ba) + jax.nn.sigmoid(gr) * (ret @ w_br)
    y = merged @ w_o
    return y, k.transpose(0, 2, 1, 3), v.transpose(0, 2, 1, 3), S_f, S_b


def adaln(cond, w_mod, b_mod):
    return (jax.nn.silu(cond) @ w_mod + b_mod)[:, None, :]


def trunk_layer(x, mod, n1, n2, w_in, sink, dec_f, dec_b, gn_g, w_ba, w_br, w_o, w_ff1, w_ff2,
                ctx_k, ctx_v, s_f0, s_b0, grid):
    sh1, sc1, g1, sh2, sc2, g2 = jnp.split(mod, 6, axis=-1)
    h = rmsnorm(x, n1) * (1.0 + sc1) + sh1
    y, k_c, v_c, S_f, S_b = token_mixing(h, w_in, sink, dec_f, dec_b, gn_g, w_ba, w_br, w_o,
                                         ctx_k, ctx_v, s_f0, s_b0, grid)
    x = x + g1 * y
    h2 = rmsnorm(x, n2) * (1.0 + sc2) + sh2
    x = x + g2 * (jnp.square(jax.nn.relu(h2 @ w_ff1)) @ w_ff2)
    return x, k_c, v_c, S_f, S_b


def setup_inputs(seed: int = 0) -> dict:
    key = jax.random.key(seed)
    ks = jax.random.split(key, 24)
    nrm = lambda k, shape, s: jax.random.normal(k, shape, jnp.float32) * s
    base_logit = jnp.log(2.0 ** (5.0 + jnp.arange(RET_HEADS, dtype=jnp.float32)) - 1.0)
    return {
        "x_prompt": nrm(ks[0], (BATCH, SEQ, D_MODEL), 1.0),
        "x_sample": nrm(ks[1], (DEC_BATCH, DEC_SEQ, D_MODEL), 1.0),
        "cache_k": nrm(ks[2], (DEC_BATCH, DEPTH, KV_HEADS, PAST_LEN, HEAD_DIM), 1.0),
        "cache_v": nrm(ks[3], (DEC_BATCH, DEPTH, KV_HEADS, PAST_LEN, HEAD_DIM), 1.0),
        "state_ret_fwd": nrm(ks[4], (DEC_BATCH, DEPTH, RET_HEADS, RET_DK, RET_DV), 0.3),
        "state_ret_bwd": nrm(ks[5], (DEC_BATCH, DEPTH, RET_HEADS, RET_DK, RET_DV), 0.3),
        "c": nrm(ks[6], (DEC_BATCH, D_MODEL), 1.0),
        "c_ctx": nrm(ks[7], (D_MODEL,), 1.0),
        "norm1_g": 1.0 + nrm(ks[8], (DEPTH, D_MODEL), 0.01),
        "norm2_g": 1.0 + nrm(ks[9], (DEPTH, D_MODEL), 0.01),
        "final_norm_g": 1.0 + nrm(ks[10], (D_MODEL,), 0.01),
        "w_mod": nrm(ks[11], (DEPTH, D_MODEL, 6 * D_MODEL), 0.5 * D_MODEL ** -0.5),
        "b_mod": nrm(ks[12], (DEPTH, 6 * D_MODEL), 0.01),
        "w_in": nrm(ks[13], (DEPTH, D_MODEL, IN_WIDTH), D_MODEL ** -0.5),
        "attn_sink": nrm(ks[14], (DEPTH, N_HEADS), 0.5),
        "ret_decay_fwd": base_logit[None, :] + nrm(ks[15], (DEPTH, RET_HEADS), 0.05),
        "ret_decay_bwd": base_logit[None, :] + nrm(ks[16], (DEPTH, RET_HEADS), 0.05),
        "ret_gn_g": 1.0 + nrm(ks[17], (DEPTH, RET_VW), 0.01),
        "w_branch_attn": nrm(ks[18], (DEPTH, ATTN_W, D_MODEL), ATTN_W ** -0.5),
        "w_branch_ret": nrm(ks[19], (DEPTH, RET_VW, D_MODEL), RET_VW ** -0.5),
        "w_out": nrm(ks[20], (DEPTH, D_MODEL, D_MODEL), D_MODEL ** -0.5),
        "w_ff1": nrm(ks[21], (DEPTH, D_MODEL, D_FF), D_MODEL ** -0.5),
        "w_ff2": nrm(ks[22], (DEPTH, D_FF, D_MODEL), D_FF ** -0.5),
    }


def reference(x_prompt, x_sample, cache_k, cache_v, state_ret_fwd, state_ret_bwd, c, c_ctx,
              norm1_g, norm2_g, final_norm_g, w_mod, b_mod, w_in, attn_sink, ret_decay_fwd,
              ret_decay_bwd, ret_gn_g, w_branch_attn, w_branch_ret, w_out, w_ff1, w_ff2):
    T = x_sample.shape[1]
    ROWS = T // GRID_W
    rows = jnp.broadcast_to(jnp.arange(ROWS, dtype=jnp.float32)[:, None], (ROWS, GRID_W)).reshape(-1)
    cols = jnp.broadcast_to(jnp.arange(GRID_W, dtype=jnp.float32)[None, :], (ROWS, GRID_W)).reshape(-1)
    grid = (rows, cols)

    xp = x_prompt
    xs = x_sample
    ks_out, vs_out, sf_out, sb_out = [], [], [], []
    for l in range(DEPTH):
        layer_w = (norm1_g[l], norm2_g[l], w_in[l], attn_sink[l], ret_decay_fwd[l], ret_decay_bwd[l],
                   ret_gn_g[l], w_branch_attn[l], w_branch_ret[l], w_out[l], w_ff1[l], w_ff2[l])
        mod_ctx = adaln(c_ctx[None, :], w_mod[l], b_mod[l])
        xp, k_c, v_c, S_f, S_b = trunk_layer(xp, mod_ctx, *layer_w, None, None, None, None, None)
        ks_out.append(k_c)
        vs_out.append(v_c)
        sf_out.append(S_f)
        sb_out.append(S_b)
        mod_lat = adaln(c, w_mod[l], b_mod[l])
        xs, _, _, _, _ = trunk_layer(xs, mod_lat, *layer_w, cache_k[:, l], cache_v[:, l],
                                     state_ret_fwd[:, l], state_ret_bwd[:, l], grid)

    y_prompt = rmsnorm(xp, final_norm_g)
    y_sample = rmsnorm(xs, final_norm_g)
    new_cache_k = jnp.stack(ks_out, axis=1)
    new_cache_v = jnp.stack(vs_out, axis=1)
    new_state_ret_fwd = jnp.stack(sf_out, axis=1)
    new_state_ret_bwd = jnp.stack(sb_out, axis=1)
    return (y_prompt, y_sample, new_cache_k, new_cache_v, new_state_ret_fwd, new_state_ret_bwd)
```

```python
import functools

import jax
import jax.numpy as jnp
from jax import lax
from jax.experimental import pallas as pl
from jax.experimental.pallas import tpu as pltpu

D_MODEL = 1024
DEPTH = 4
GRID_W = 64
N_HEADS = 8
HEAD_DIM = 64
KV_HEADS = 2
GROUP = N_HEADS // KV_HEADS
WINDOW = 128
Q_BLOCK = 128
RET_HEADS = 4
RET_DK = 128
RET_DV = 128
RET_CHUNK = 128
ATTN_W = N_HEADS * HEAD_DIM
KV_W = KV_HEADS * HEAD_DIM
RET_W = RET_HEADS * RET_DK
RET_VW = RET_HEADS * RET_DV
GATE_W = 2 * RET_VW + 2 * D_MODEL
IN_WIDTH = ATTN_W + 2 * KV_W + 2 * RET_W + RET_VW + GATE_W
D_FF = 4 * D_MODEL
ROPE_BASE = 10000.0
NORM_EPS = 1e-6
NEG_INF = -1e30

OFF_Q = 0
OFF_K = OFF_Q + ATTN_W
OFF_V = OFF_K + KV_W
OFF_RQ = OFF_V + KV_W
OFF_RK = OFF_RQ + RET_W
OFF_RV = OFF_RK + RET_W
OFF_GATES = OFF_RV + RET_VW

LANES = 128
ROW_TILE = 256
MOD_TILE_N = 1536
FF_CHUNK = 1024
VMEM_LIMIT = 56 * 1024 * 1024

BF16 = jnp.bfloat16
F32 = jnp.float32


def _params(n_axes):
    return pltpu.CompilerParams(dimension_semantics=("arbitrary",) * n_axes,
                                vmem_limit_bytes=VMEM_LIMIT)


def _const_spec(shape):
    nd = len(shape)
    return pl.BlockSpec(shape, lambda *_: (0,) * nd, pipeline_mode=pl.Buffered(1))


def _mod_spec(mod, tiles_per_batch):
    if mod.shape[0] == 1:
        return _const_spec((1, 1, 6 * D_MODEL))
    return pl.BlockSpec((1, 1, 6 * D_MODEL), lambda i: (i // tiles_per_batch, 0, 0))


def _dot(a, b):
    return jnp.dot(a, b, preferred_element_type=F32)


def _dot_nt(a, b):
    return lax.dot_general(a, b, (((1,), (1,)), ((), ())), preferred_element_type=F32)


def _mod_kernel(cond_ref, w_ref, b_ref, out_ref):
    cond = cond_ref[...]
    s = (cond * jax.nn.sigmoid(cond)).astype(BF16)
    out_ref[0] = _dot(s, w_ref[0].astype(BF16)) + b_ref[0]


def _modulation(cond8, w_mod, b_mod):
    n_rows = cond8.shape[0]
    return pl.pallas_call(
        _mod_kernel,
        grid=(DEPTH, 6 * D_MODEL // MOD_TILE_N),
        in_specs=[
            pl.BlockSpec((n_rows, D_MODEL), lambda l, j: (0, 0)),
            pl.BlockSpec((1, D_MODEL, MOD_TILE_N), lambda l, j: (l, 0, j)),
            pl.BlockSpec((1, 1, MOD_TILE_N), lambda l, j: (l, 0, j)),
        ],
        out_specs=pl.BlockSpec((1, n_rows, MOD_TILE_N), lambda l, j: (l, 0, j)),
        out_shape=jax.ShapeDtypeStruct((DEPTH, n_rows, 6 * D_MODEL), F32),
        compiler_params=_params(2),
        name="adaln_modulation",
    )(cond8, w_mod, b_mod.reshape(DEPTH, 1, 6 * D_MODEL))


def _modulated_norm(x, gain, shift, scale):
    y = x * lax.rsqrt(jnp.mean(x * x, axis=-1, keepdims=True) + NORM_EPS) * gain
    return y * (1.0 + scale) + shift


def _rope_chunk(xc, cos, sin_signed, first_half, nf):
    partner = jnp.where(first_half, pltpu.roll(xc, LANES - nf, 1), pltpu.roll(xc, nf, 1))
    return xc * cos + partner * sin_signed


def _inproj_kernel(*refs, latent):
    if latent:
        (x_ref, mod_ref, n1_ref, w_ref, cos_a, sin_a, cos_r, sin_r,
         q_ref, k_ref, v_ref, rq_ref, rk_ref, rv_ref, g_ref) = refs
    else:
        (x_ref, mod_ref, n1_ref, w_ref,
         q_ref, k_ref, v_ref, rq_ref, rk_ref, rv_ref, g_ref, kc_ref, vc_ref) = refs
    mod = mod_ref[0]
    h = _modulated_norm(x_ref[...], n1_ref[...], mod[:, 0:D_MODEL], mod[:, D_MODEL:2 * D_MODEL])
    hb = h.astype(BF16)

    def proj(off, width):
        return _dot(hb, w_ref[:, off:off + width])

    def rope(p, cos_ref, sin_ref, nf):
        if not latent:
            return p
        lane = lax.broadcasted_iota(jnp.int32, (p.shape[0], LANES), 1)
        first_half = (lane % (2 * nf)) < nf
        cos = cos_ref[...]
        sin = sin_ref[...]
        parts = [_rope_chunk(p[:, c:c + LANES], cos, sin, first_half, nf)
                 for c in range(0, p.shape[1], LANES)]
        return parts[0] if len(parts) == 1 else jnp.concatenate(parts, axis=1)

    nf_a = HEAD_DIM // 4
    nf_r = RET_DK // 4
    q_ref[...] = rope(proj(OFF_Q, ATTN_W), cos_a if latent else None, sin_a if latent else None,
                      nf_a).astype(BF16)
    k = rope(proj(OFF_K, KV_W), cos_a if latent else None, sin_a if latent else None, nf_a)
    k_ref[...] = k.astype(BF16)
    v = proj(OFF_V, KV_W)
    v_ref[...] = v.astype(BF16)
    if not latent:
        for kh in range(KV_HEADS):
            kc_ref[0, 0, kh] = k[:, kh * HEAD_DIM:(kh + 1) * HEAD_DIM]
            vc_ref[0, 0, kh] = v[:, kh * HEAD_DIM:(kh + 1) * HEAD_DIM]
    rq_ref[...] = rope(proj(OFF_RQ, RET_W), cos_r if latent else None, sin_r if latent else None,
                       nf_r).astype(BF16)
    rk = rope(proj(OFF_RK, RET_W), cos_r if latent else None, sin_r if latent else None, nf_r)
    rk_ref[...] = rk * (RET_DK ** -0.5)
    rv_ref[...] = proj(OFF_RV, RET_VW).astype(BF16)
    for c in range(0, GATE_W, D_MODEL):
        g_ref[:, c:c + D_MODEL] = proj(OFF_GATES + c, D_MODEL)


def _in_projection(x, mod, n1, w_in, seq_len, tables):
    n_tok = x.shape[0]
    latent = tables is not None
    tiles_per_batch = seq_len // ROW_TILE
    row = lambda w: pl.BlockSpec((ROW_TILE, w), lambda i: (i, 0))
    in_specs = [
        row(D_MODEL),
        _mod_spec(mod, tiles_per_batch),
        _const_spec((1, D_MODEL)),
        _const_spec((D_MODEL, IN_WIDTH)),
    ]
    args = [x, mod, n1, w_in]
    out_specs = [row(ATTN_W), row(KV_W), row(KV_W), row(RET_W), row(RET_W), row(RET_VW), row(GATE_W)]
    out_shape = [
        jax.ShapeDtypeStruct((n_tok, ATTN_W), BF16),
        jax.ShapeDtypeStruct((n_tok, KV_W), BF16),
        jax.ShapeDtypeStruct((n_tok, KV_W), BF16),
        jax.ShapeDtypeStruct((n_tok, RET_W), BF16),
        jax.ShapeDtypeStruct((n_tok, RET_W), F32),
        jax.ShapeDtypeStruct((n_tok, RET_VW), BF16),
        jax.ShapeDtypeStruct((n_tok, GATE_W), F32),
    ]
    if latent:
        tab = pl.BlockSpec((ROW_TILE, LANES), lambda i: (i % tiles_per_batch, 0))
        in_specs += [tab] * 4
        args += list(tables)
    else:
        assert seq_len == ROW_TILE
        n_batch = n_tok // seq_len
        cache = pl.BlockSpec((1, 1, KV_HEADS, seq_len, HEAD_DIM), lambda i: (i, 0, 0, 0, 0))
        out_specs += [cache, cache]
        out_shape += [jax.ShapeDtypeStruct((n_batch, 1, KV_HEADS, seq_len, HEAD_DIM), F32)] * 2
    return pl.pallas_call(
        functools.partial(_inproj_kernel, latent=latent),
        grid=(n_tok // ROW_TILE,),
        in_specs=in_specs,
        out_specs=out_specs,
        out_shape=out_shape,
        compiler_params=_params(1),
        name="in_projection_lat" if latent else "in_projection_ctx",
    )(*args)


def _softmax_parts(scores, sink_col):
    m = sink_col
    for s in scores:
        m = jnp.maximum(m, jnp.max(s, axis=-1, keepdims=True))
    es = [jnp.exp(s - m) for s in scores]
    denom = jnp.exp(sink_col - m)
    for e in es:
        denom = denom + jnp.sum(e, axis=-1, keepdims=True)
    return [e / denom for e in es]


def _ctx_attn_kernel(sink_ref, q_ref, k_ref, v_ref, o_ref):
    scale = HEAD_DIM ** -0.5
    rows = q_ref.shape[0]
    for kh in range(KV_HEADS):
        ks = k_ref[:, kh * HEAD_DIM:(kh + 1) * HEAD_DIM]
        vs = v_ref[:, kh * HEAD_DIM:(kh + 1) * HEAD_DIM]
        for g in range(GROUP):
            h = kh * GROUP + g
            qh = q_ref[:, h * HEAD_DIM:(h + 1) * HEAD_DIM]
            s = _dot_nt(qh, ks) * scale
            sink_col = jnp.full((rows, 1), sink_ref[h], F32)
            (p,) = _softmax_parts([s], sink_col)
            o_ref[:, h * HEAD_DIM:(h + 1) * HEAD_DIM] = _dot(p.astype(BF16), vs).astype(BF16)


def _context_attention(q, k, v, sink, seq_len):
    n_tok = q.shape[0]
    row = lambda w: pl.BlockSpec((seq_len, w), lambda b: (b, 0))
    return pl.pallas_call(
        _ctx_attn_kernel,
        grid=(n_tok // seq_len,),
        in_specs=[pl.BlockSpec(memory_space=pltpu.SMEM), row(ATTN_W), row(KV_W), row(KV_W)],
        out_specs=row(ATTN_W),
        out_shape=jax.ShapeDtypeStruct((n_tok, ATTN_W), BF16),
        compiler_params=_params(1),
        name="attention_ctx",
    )(sink, q, k, v)


def _lat_attn_kernel(sink_ref, q_ref, k_ref, v_ref, ck_ref, cv_ref, o_ref):
    scale = HEAD_DIM ** -0.5
    seq_len = k_ref.shape[0]
    band = Q_BLOCK + 2 * WINDOW
    j = pl.program_id(1)
    start = pl.multiple_of(jnp.clip(j * Q_BLOCK - WINDOW, 0, seq_len - band), Q_BLOCK)
    qpos = j * Q_BLOCK + lax.broadcasted_iota(jnp.int32, (Q_BLOCK, band), 0)
    kpos = start + lax.broadcasted_iota(jnp.int32, (Q_BLOCK, band), 1)
    valid1 = jnp.abs(qpos - kpos) <= WINDOW
    valid = jnp.concatenate([valid1] * GROUP, axis=0)
    kb = k_ref[pl.ds(start, band), :]
    vb = v_ref[pl.ds(start, band), :]
    for kh in range(KV_HEADS):
        hd = slice(kh * HEAD_DIM, (kh + 1) * HEAD_DIM)
        ck = ck_ref[0, 0, kh].astype(BF16)
        cv = cv_ref[0, 0, kh].astype(BF16)
        heads = [kh * GROUP + g for g in range(GROUP)]
        qg = jnp.concatenate([q_ref[:, h * HEAD_DIM:(h + 1) * HEAD_DIM] for h in heads], axis=0)
        sink_col = jnp.concatenate([jnp.full((Q_BLOCK, 1), sink_ref[h], F32) for h in heads], axis=0)
        s_loc = jnp.where(valid, _dot_nt(qg, kb[:, hd]) * scale, NEG_INF)
        s_ctx = _dot_nt(qg, ck) * scale
        p_loc, p_ctx = _softmax_parts([s_loc, s_ctx], sink_col)
        o = _dot(p_loc.astype(BF16), vb[:, hd]) + _dot(p_ctx.astype(BF16), cv)
        for g, h in enumerate(heads):
            o_ref[:, h * HEAD_DIM:(h + 1) * HEAD_DIM] = o[g * Q_BLOCK:(g + 1) * Q_BLOCK].astype(BF16)


def _latent_attention(q, k, v, cache_k, cache_v, sink, layer, seq_len):
    n_tok = q.shape[0]
    n_batch = n_tok // seq_len
    n_blocks = seq_len // Q_BLOCK
    past_len = cache_k.shape[3]
    seq = lambda w: pl.BlockSpec((seq_len, w), lambda b, j: (b, 0))
    blk = pl.BlockSpec((Q_BLOCK, ATTN_W), lambda b, j: (b * n_blocks + j, 0))
    cache = pl.BlockSpec((1, 1, KV_HEADS, past_len, HEAD_DIM), lambda b, j: (b, layer, 0, 0, 0))
    return pl.pallas_call(
        _lat_attn_kernel,
        grid=(n_batch, n_blocks),
        in_specs=[pl.BlockSpec(memory_space=pltpu.SMEM), blk, seq(KV_W), seq(KV_W), cache, cache],
        out_specs=blk,
        out_shape=jax.ShapeDtypeStruct((n_tok, ATTN_W), BF16),
        compiler_params=_params(2),
        name="attention_lat",
    )(sink, q, k, v, cache_k, cache_v)


def _retention_kernel(*refs, latent):
    if latent:
        (rq_ref, rk_ref, rv_ref, gf_ref, gb_ref, dec_ref, gn_ref, s0f_ref, s0b_ref,
         ret_ref, acc_ref, state_ref) = refs
    else:
        (rq_ref, rk_ref, rv_ref, gf_ref, gb_ref, dec_ref, gn_ref,
         ret_ref, sf_ref, sb_ref, acc_ref, state_ref) = refs
    C = RET_CHUNK
    n_chunks = rq_ref.shape[0] // C
    log_g = jnp.log(jax.nn.sigmoid(dec_ref[...]))
    row = lax.broadcasted_iota(jnp.int32, (C, C), 0).astype(F32)
    col = lax.broadcasted_iota(jnp.int32, (C, C), 1).astype(F32)
    for backward in (False, True):
        g_ref = gb_ref if backward else gf_ref
        for h in range(RET_HEADS):
            hs = slice(h * RET_DK, (h + 1) * RET_DK)
            r = (RET_HEADS if backward else 0) + h
            lg = log_g[r:r + 1, :]
            diff = (col - row) if backward else (row - col)
            intra = jnp.where(diff >= 0, jnp.exp(jnp.maximum(diff, 0.0) * lg), 0.0)
            cross = jnp.exp(((C - row) if backward else (row + 1.0)) * lg)
            kdec = jnp.exp((row if backward else (C - 1.0 - row)) * lg)
            g_chunk = jnp.exp(C * lg)
            if latent:
                state_ref[...] = (s0b_ref if backward else s0f_ref)[0, 0, h]
            else:
                state_ref[...] = jnp.zeros((RET_DK, RET_DV), F32)

            def chunk(step, carry, backward=backward, hs=hs, intra=intra, cross=cross, kdec=kdec,
                      g_chunk=g_chunk, g_ref=g_ref):
                i = (n_chunks - 1 - step) if backward else step
                rows = pl.ds(pl.multiple_of(i * C, C), C)
                qi = rq_ref[rows, hs]
                ki = rk_ref[rows, hs]
                vi = rv_ref[rows, hs]
                state = state_ref[...]
                o = _dot(qi, state.astype(BF16)) * cross
                sc = _dot_nt(qi, ki.astype(BF16)) * intra
                o = o + _dot(sc.astype(BF16), vi)
                kd_t = jnp.transpose(ki * kdec).astype(BF16)
                state_ref[...] = state * g_chunk + _dot(kd_t, vi)
                mu = jnp.mean(o, axis=-1, keepdims=True)
                dev = o - mu
                var = jnp.mean(dev * dev, axis=-1, keepdims=True)
                y = dev * lax.rsqrt(var + NORM_EPS) * gn_ref[:, hs]
                gate = g_ref[rows, hs]
                contrib = gate * jax.nn.sigmoid(gate) * y
                if backward:
                    ret_ref[rows, hs] = (acc_ref[rows, hs] + contrib).astype(BF16)
                else:
                    acc_ref[rows, hs] = contrib
                return carry

            lax.fori_loop(0, n_chunks, chunk, 0)
            if not latent:
                (sb_ref if backward else sf_ref)[0, 0, h] = state_ref[...]


def _retention(rq, rk, rv, gates, dec_rows, gn, seq_len, states):
    n_tok = rq.shape[0]
    n_batch = n_tok // seq_len
    latent = states is not None
    seq = lambda w, c=0: pl.BlockSpec((seq_len, w), lambda b, c=c: (b, c))
    in_specs = [seq(RET_W), seq(RET_W), seq(RET_VW), seq(RET_VW, 0), seq(RET_VW, 1),
                _const_spec((2 * RET_HEADS, LANES)), _const_spec((1, RET_VW))]
    args = [rq, rk, rv, gates, gates, dec_rows, gn]
    out_specs = [seq(RET_VW)]
    out_shape = [jax.ShapeDtypeStruct((n_tok, RET_VW), BF16)]
    state_block = (1, 1, RET_HEADS, RET_DK, RET_DV)
    if latent:
        s_f, s_b, layer = states
        spec = pl.BlockSpec(state_block, lambda b: (b, layer, 0, 0, 0))
        in_specs += [spec, spec]
        args += [s_f, s_b]
    else:
        spec = pl.BlockSpec(state_block, lambda b: (b, 0, 0, 0, 0))
        out_specs += [spec, spec]
        out_shape += [jax.ShapeDtypeStruct((n_batch,) + state_block[1:], F32)] * 2
    return pl.pallas_call(
        functools.partial(_retention_kernel, latent=latent),
        grid=(n_batch,),
        in_specs=in_specs,
        out_specs=out_specs,
        out_shape=out_shape,
        scratch_shapes=[pltpu.VMEM((seq_len, RET_VW), F32), pltpu.VMEM((RET_DK, RET_DV), F32)],
        compiler_params=_params(1),
        name="retention_lat" if latent else "retention_ctx",
    )(*args)


def _post_kernel(*refs, final):
    if final:
        (attn_ref, ret_ref, ga_ref, gr_ref, x_ref, mod_ref, n2_ref,
         wba_ref, wbr_ref, wo_ref, w1_ref, w2_ref, fn_ref, out_ref) = refs
    else:
        (attn_ref, ret_ref, ga_ref, gr_ref, x_ref, mod_ref, n2_ref,
         wba_ref, wbr_ref, wo_ref, w1_ref, w2_ref, out_ref) = refs
    mod = mod_ref[0]
    part = lambda k: mod[:, k * D_MODEL:(k + 1) * D_MODEL]
    merged = (jax.nn.sigmoid(ga_ref[...]) * _dot(attn_ref[...], wba_ref[...])
              + jax.nn.sigmoid(gr_ref[...]) * _dot(ret_ref[...], wbr_ref[...]))
    x = x_ref[...] + part(2) * _dot(merged.astype(BF16), wo_ref[...])
    h2 = _modulated_norm(x, n2_ref[...], part(3), part(4)).astype(BF16)
    ff = None
    for c in range(0, D_FF, FF_CHUNK):
        a = jnp.maximum(_dot(h2, w1_ref[:, c:c + FF_CHUNK]), 0.0)
        t = _dot((a * a).astype(BF16), w2_ref[c:c + FF_CHUNK, :])
        ff = t if ff is None else ff + t
    x = x + part(5) * ff
    if final:
        x = x * lax.rsqrt(jnp.mean(x * x, axis=-1, keepdims=True) + NORM_EPS) * fn_ref[...]
    out_ref[...] = x


def _post(attn, ret, gates, x, mod, n2, w_ba, w_br, w_o, w_ff1, w_ff2, seq_len, final_gain):
    n_tok = x.shape[0]
    tiles_per_batch = seq_len // ROW_TILE
    final = final_gain is not None
    row = lambda w, c=0: pl.BlockSpec((ROW_TILE, w), lambda i, c=c: (i, c))
    in_specs = [
        row(ATTN_W), row(RET_VW), row(D_MODEL, 1), row(D_MODEL, 2), row(D_MODEL),
        _mod_spec(mod, tiles_per_batch),
        _const_spec((1, D_MODEL)),
        _const_spec((ATTN_W, D_MODEL)), _const_spec((RET_VW, D_MODEL)),
        _const_spec((D_MODEL, D_MODEL)), _const_spec((D_MODEL, D_FF)), _const_spec((D_FF, D_MODEL)),
    ]
    args = [attn, ret, gates, gates, x, mod, n2, w_ba, w_br, w_o, w_ff1, w_ff2]
    if final:
        in_specs.append(_const_spec((1, D_MODEL)))
        args.append(final_gain)
    return pl.pallas_call(
        functools.partial(_post_kernel, final=final),
        grid=(n_tok // ROW_TILE,),
        in_specs=in_specs,
        out_specs=row(D_MODEL),
        out_shape=jax.ShapeDtypeStruct((n_tok, D_MODEL), F32),
        compiler_params=_params(1),
        name="post_final" if final else "post",
    )(*args)


def _rope_tables(seq_len):
    t = jnp.arange(seq_len, dtype=jnp.int32)
    rows = (t // GRID_W).astype(F32)
    cols = (t % GRID_W).astype(F32)

    def table(head_dim):
        half = head_dim // 2
        nf = half // 2
        inv = ROPE_BASE ** (-jnp.arange(nf, dtype=F32) / nf)
        lane = jnp.arange(LANES) % head_dim
        pos = jnp.where((lane < half)[None, :], rows[:, None], cols[:, None])
        ang = pos * inv[lane % nf][None, :]
        sign = jnp.where((lane % half) < nf, -1.0, 1.0)[None, :]
        return jnp.cos(ang), jnp.sin(ang) * sign

    cos_a, sin_a = table(HEAD_DIM)
    cos_r, sin_r = table(RET_DK)
    return cos_a, sin_a, cos_r, sin_r


def kernel(x_prompt, x_sample, cache_k, cache_v, state_ret_fwd, state_ret_bwd, c, c_ctx, norm1_g, norm2_g, final_norm_g, w_mod, b_mod, w_in, attn_sink, ret_decay_fwd, ret_decay_bwd, ret_gn_g, w_branch_attn, w_branch_ret, w_out, w_ff1, w_ff2):
    n_ctx, ctx_len, _ = x_prompt.shape
    n_lat, lat_len, _ = x_sample.shape
    xp = x_prompt.reshape(n_ctx * ctx_len, D_MODEL)
    xs = x_sample.reshape(n_lat * lat_len, D_MODEL)

    cond = jnp.concatenate([c_ctx[None, :], c], axis=0)
    cond8 = jnp.pad(cond, ((0, -cond.shape[0] % 8), (0, 0)))
    mod = _modulation(cond8, w_mod, b_mod)
    tables = _rope_tables(lat_len)
    final_gain = final_norm_g.reshape(1, D_MODEL)

    w_in_b = w_in.astype(BF16)
    w_ba_b = w_branch_attn.astype(BF16)
    w_br_b = w_branch_ret.astype(BF16)
    w_o_b = w_out.astype(BF16)
    w_ff1_b = w_ff1.astype(BF16)
    w_ff2_b = w_ff2.astype(BF16)

    ks, vs, sfs, sbs = [], [], [], []
    for l in range(DEPTH):
        n1 = norm1_g[l].reshape(1, D_MODEL)
        n2 = norm2_g[l].reshape(1, D_MODEL)
        gn = ret_gn_g[l].reshape(1, RET_VW)
        dec_rows = jnp.broadcast_to(
            jnp.concatenate([ret_decay_fwd[l], ret_decay_bwd[l]])[:, None], (2 * RET_HEADS, LANES))
        last = l == DEPTH - 1
        post_w = (n2, w_ba_b[l], w_br_b[l], w_o_b[l], w_ff1_b[l], w_ff2_b[l])

        mod_ctx = mod[l, 0:1].reshape(1, 1, 6 * D_MODEL)
        q, k, v, rq, rk, rv, gates, k_c, v_c = _in_projection(xp, mod_ctx, n1, w_in_b[l], ctx_len, None)
        attn = _context_attention(q, k, v, attn_sink[l], ctx_len)
        ret, s_f, s_b = _retention(rq, rk, rv, gates, dec_rows, gn, ctx_len, None)
        xp = _post(attn, ret, gates, xp, mod_ctx, *post_w, ctx_len, final_gain if last else None)
        ks.append(k_c)
        vs.append(v_c)
        sfs.append(s_f)
        sbs.append(s_b)

        mod_lat = mod[l, 1:1 + n_lat].reshape(n_lat, 1, 6 * D_MODEL)
        q, k, v, rq, rk, rv, gates = _in_projection(xs, mod_lat, n1, w_in_b[l], lat_len, tables)
        attn = _latent_attention(q, k, v, cache_k, cache_v, attn_sink[l], l, lat_len)
        (ret,) = _retention(rq, rk, rv, gates, dec_rows, gn, lat_len,
                            (state_ret_fwd, state_ret_bwd, l))
        xs = _post(attn, ret, gates, xs, mod_lat, *post_w, lat_len, final_gain if last else None)

    return (xp.reshape(n_ctx, ctx_len, D_MODEL),
            xs.reshape(n_lat, lat_len, D_MODEL),
            jnp.concatenate(ks, axis=1),
            jnp.concatenate(vs, axis=1),
            jnp.concatenate(sfs, axis=1),
            jnp.concatenate(sbs, axis=1))
```

```python
import functools

import jax
import jax.numpy as jnp
from jax import lax
from jax.experimental import pallas as pl
from jax.experimental.pallas import tpu as pltpu

D_MODEL = 1024
DEPTH = 4
GRID_W = 64
N_HEADS = 8
HEAD_DIM = 64
KV_HEADS = 2
GROUP = N_HEADS // KV_HEADS
WINDOW = 128
Q_BLOCK = 128
RET_HEADS = 4
RET_DK = 128
RET_DV = 128
ATTN_W = N_HEADS * HEAD_DIM
KV_W = KV_HEADS * HEAD_DIM
RET_W = RET_HEADS * RET_DK
RET_VW = RET_HEADS * RET_DV
GATE_W = 2 * RET_VW + 2 * D_MODEL
IN_WIDTH = ATTN_W + 2 * KV_W + 2 * RET_W + RET_VW + GATE_W
D_FF = 4 * D_MODEL
MOD_W = 6 * D_MODEL
ROPE_BASE = 10000.0
NORM_EPS = 1e-6
NEG_INF = -1e30

OFF_Q = 0
OFF_K = OFF_Q + ATTN_W
OFF_V = OFF_K + KV_W
OFF_RQ = OFF_V + KV_W
OFF_RK = OFF_RQ + RET_W
OFF_RV = OFF_RK + RET_W
OFF_GATES = OFF_RV + RET_VW

LANES = 128
ROW_TILE = 256
RET_BLOCK = 256
MOD_TILE_N = 1536
FF_CHUNK = 1024
VMEM_LIMIT = 56 * 1024 * 1024

BF16 = jnp.bfloat16
F32 = jnp.float32


def _params(n_axes):
    return pltpu.CompilerParams(dimension_semantics=("arbitrary",) * n_axes,
                                vmem_limit_bytes=VMEM_LIMIT)


def _const_spec(shape):
    nd = len(shape)
    return pl.BlockSpec(shape, lambda *_: (0,) * nd, pipeline_mode=pl.Buffered(1))


def _layer_spec(shape, layer):
    nd = len(shape)
    return pl.BlockSpec((None,) + tuple(shape), lambda *_: (layer,) + (0,) * nd,
                        pipeline_mode=pl.Buffered(1))


def _mod_spec(layer, tiles_per_batch):
    if tiles_per_batch is None:
        return pl.BlockSpec((None, None, 1, MOD_W), lambda *_: (layer, 0, 0, 0))
    return pl.BlockSpec((None, None, 1, MOD_W), lambda i: (layer, 1 + i // tiles_per_batch, 0, 0))


def _dot(a, b):
    return jnp.dot(a, b, preferred_element_type=F32)


def _dot_nt(a, b):
    return lax.dot_general(a, b, (((1,), (1,)), ((), ())), preferred_element_type=F32)


def _dot_tn(a, b):
    return lax.dot_general(a, b, (((0,), (0,)), ((), ())), preferred_element_type=F32)


def _mod_kernel(cond_ref, w_ref, b_ref, out_ref):
    cond = cond_ref[...]
    s = (cond * jax.nn.sigmoid(cond)).astype(BF16)
    out_ref[0] = _dot(s, w_ref[0].astype(BF16)) + b_ref[0]


def _modulation(cond8, w_mod, b_mod):
    n_rows = cond8.shape[0]
    return pl.pallas_call(
        _mod_kernel,
        grid=(DEPTH, MOD_W // MOD_TILE_N),
        in_specs=[
            pl.BlockSpec((n_rows, D_MODEL), lambda l, j: (0, 0)),
            pl.BlockSpec((1, D_MODEL, MOD_TILE_N), lambda l, j: (l, 0, j)),
            pl.BlockSpec((1, 1, MOD_TILE_N), lambda l, j: (l, 0, j)),
        ],
        out_specs=pl.BlockSpec((1, n_rows, MOD_TILE_N), lambda l, j: (l, 0, j)),
        out_shape=jax.ShapeDtypeStruct((DEPTH, n_rows, MOD_W), F32),
        compiler_params=_params(2),
        name="adaln_modulation",
    )(cond8, w_mod, b_mod.reshape(DEPTH, 1, MOD_W))


def _decay_kernel(dec_ref, intra_ref, cross_ref, kdec_ref, gblk_ref):
    backward = pl.program_id(1) == 1
    log_g = jnp.log(jax.nn.sigmoid(dec_ref[...]))
    n = RET_BLOCK
    row = lax.broadcasted_iota(jnp.int32, (n, n), 0).astype(F32)
    col = lax.broadcasted_iota(jnp.int32, (n, n), 1).astype(F32)
    diff = jnp.where(backward, col - row, row - col)
    row1 = row[:, :LANES]
    cross_pow = jnp.where(backward, n - row1, row1 + 1.0)
    kdec_pow = jnp.where(backward, row1, n - 1.0 - row1)
    for h in range(RET_HEADS):
        lg = log_g[h:h + 1, :]
        lg_wide = jnp.concatenate([lg] * (n // LANES), axis=1)
        intra_ref[h] = jnp.where(diff >= 0, jnp.exp(jnp.maximum(diff, 0.0) * lg_wide), 0.0)
        cross_ref[h] = jnp.exp(cross_pow * lg)
        kdec_ref[h] = jnp.exp(kdec_pow * lg)
        gblk_ref[h] = jnp.exp(n * lg)


def _decay_tables(dec_rows):
    lead = lambda *tail: pl.BlockSpec((None, None, RET_HEADS) + tail,
                                      lambda l, d: (l, d, 0) + (0,) * len(tail))
    shape = lambda *tail: jax.ShapeDtypeStruct((DEPTH, 2, RET_HEADS) + tail, F32)
    return pl.pallas_call(
        _decay_kernel,
        grid=(DEPTH, 2),
        in_specs=[lead(LANES)],
        out_specs=[lead(RET_BLOCK, RET_BLOCK), lead(RET_BLOCK, LANES), lead(RET_BLOCK, LANES),
                   lead(1, LANES)],
        out_shape=[shape(RET_BLOCK, RET_BLOCK), shape(RET_BLOCK, LANES), shape(RET_BLOCK, LANES),
                   shape(1, LANES)],
        compiler_params=_params(2),
        name="retention_decay_tables",
    )(dec_rows)


def _modulated_norm(x, gain, shift, scale):
    y = x * lax.rsqrt(jnp.mean(x * x, axis=-1, keepdims=True) + NORM_EPS) * gain
    return y * (1.0 + scale) + shift


def _rope_chunk(xc, cos, sin_signed, first_half, nf):
    partner = jnp.where(first_half, pltpu.roll(xc, LANES - nf, 1), pltpu.roll(xc, nf, 1))
    return xc * cos + partner * sin_signed


def _inproj_kernel(*refs, latent, layer):
    if latent:
        (x_ref, mod_ref, n1_ref, w_ref, cos_a, sin_a, cos_r, sin_r,
         q_ref, k_ref, v_ref, rq_ref, rk_ref, rv_ref, g_ref) = refs
    else:
        (x_ref, mod_ref, n1_ref, w_ref, *_unused_prev,
         q_ref, k_ref, v_ref, rq_ref, rk_ref, rv_ref, g_ref, kc_ref, vc_ref) = refs
        cos_a = sin_a = cos_r = sin_r = None
    mod = mod_ref[...]
    h = _modulated_norm(x_ref[...], n1_ref[...], mod[:, 0:D_MODEL], mod[:, D_MODEL:2 * D_MODEL])
    hb = h.astype(BF16)

    def proj(off, width):
        return _dot(hb, w_ref[:, off:off + width])

    def rope(p, cos_ref, sin_ref, nf):
        if not latent:
            return p
        lane = lax.broadcasted_iota(jnp.int32, (p.shape[0], LANES), 1)
        first_half = (lane % (2 * nf)) < nf
        cos = cos_ref[...]
        sin = sin_ref[...]
        parts = [_rope_chunk(p[:, c:c + LANES], cos, sin, first_half, nf)
                 for c in range(0, p.shape[1], LANES)]
        return parts[0] if len(parts) == 1 else jnp.concatenate(parts, axis=1)

    nf_a = HEAD_DIM // 4
    nf_r = RET_DK // 4
    q_ref[...] = rope(proj(OFF_Q, ATTN_W), cos_a, sin_a, nf_a).astype(BF16)
    k = rope(proj(OFF_K, KV_W), cos_a, sin_a, nf_a)
    k_ref[...] = k.astype(BF16)
    v = proj(OFF_V, KV_W)
    v_ref[...] = v.astype(BF16)
    if not latent:
        for kh in range(KV_HEADS):
            kc_ref[0, 0, kh] = k[:, kh * HEAD_DIM:(kh + 1) * HEAD_DIM]
            vc_ref[0, 0, kh] = v[:, kh * HEAD_DIM:(kh + 1) * HEAD_DIM]
    rq_ref[...] = rope(proj(OFF_RQ, RET_W), cos_r, sin_r, nf_r).astype(BF16)
    rk = rope(proj(OFF_RK, RET_W), cos_r, sin_r, nf_r)
    rk_ref[...] = rk * (RET_DK ** -0.5)
    rv_ref[...] = proj(OFF_RV, RET_VW).astype(BF16)
    for c in range(0, GATE_W, D_MODEL):
        g_ref[:, c:c + D_MODEL] = proj(OFF_GATES + c, D_MODEL)


def _in_projection(x, mod, n1, w_in, layer, seq_len, tables, prev_caches):
    n_tok = x.shape[0]
    latent = tables is not None
    tiles_per_batch = seq_len // ROW_TILE
    row = lambda w: pl.BlockSpec((ROW_TILE, w), lambda i: (i, 0))
    in_specs = [
        row(D_MODEL),
        _mod_spec(layer, tiles_per_batch if latent else None),
        _layer_spec((1, D_MODEL), layer),
        _layer_spec((D_MODEL, IN_WIDTH), layer),
    ]
    args = [x, mod, n1, w_in]
    out_specs = [row(ATTN_W), row(KV_W), row(KV_W), row(RET_W), row(RET_W), row(RET_VW), row(GATE_W)]
    out_shape = [
        jax.ShapeDtypeStruct((n_tok, ATTN_W), BF16),
        jax.ShapeDtypeStruct((n_tok, KV_W), BF16),
        jax.ShapeDtypeStruct((n_tok, KV_W), BF16),
        jax.ShapeDtypeStruct((n_tok, RET_W), BF16),
        jax.ShapeDtypeStruct((n_tok, RET_W), F32),
        jax.ShapeDtypeStruct((n_tok, RET_VW), BF16),
        jax.ShapeDtypeStruct((n_tok, GATE_W), F32),
    ]
    aliases = {}
    if latent:
        tab = pl.BlockSpec((ROW_TILE, LANES), lambda i: (i % tiles_per_batch, 0))
        in_specs += [tab] * 4
        args += list(tables)
    else:
        assert seq_len == ROW_TILE
        n_batch = n_tok // seq_len
        cache = pl.BlockSpec((1, 1, KV_HEADS, seq_len, HEAD_DIM), lambda i: (i, layer, 0, 0, 0))
        if prev_caches is not None:
            for cache_arr in prev_caches:
                aliases[len(args)] = len(out_shape)
                in_specs.append(pl.BlockSpec(memory_space=pl.ANY))
                args.append(cache_arr)
                out_specs.append(cache)
                out_shape.append(jax.ShapeDtypeStruct(cache_arr.shape, F32))
        else:
            out_specs += [cache, cache]
            out_shape += [jax.ShapeDtypeStruct((n_batch, DEPTH, KV_HEADS, seq_len, HEAD_DIM), F32)] * 2
    return pl.pallas_call(
        functools.partial(_inproj_kernel, latent=latent, layer=layer),
        grid=(n_tok // ROW_TILE,),
        in_specs=in_specs,
        out_specs=out_specs,
        out_shape=out_shape,
        input_output_aliases=aliases,
        compiler_params=_params(1),
        name="in_projection_lat" if latent else "in_projection_ctx",
    )(*args)


def _softmax_parts(scores, sink_col):
    m = sink_col
    for s in scores:
        m = jnp.maximum(m, jnp.max(s, axis=-1, keepdims=True))
    es = [jnp.exp(s - m) for s in scores]
    denom = jnp.exp(sink_col - m)
    for e in es:
        denom = denom + jnp.sum(e, axis=-1, keepdims=True)
    return [e / denom for e in es]


def _ctx_attn_kernel(sink_ref, q_ref, k_ref, v_ref, o_ref, *, layer):
    scale = HEAD_DIM ** -0.5
    rows = q_ref.shape[0]
    for kh in range(KV_HEADS):
        ks = k_ref[:, kh * HEAD_DIM:(kh + 1) * HEAD_DIM]
        vs = v_ref[:, kh * HEAD_DIM:(kh + 1) * HEAD_DIM]
        for g in range(GROUP):
            h = kh * GROUP + g
            qh = q_ref[:, h * HEAD_DIM:(h + 1) * HEAD_DIM]
            s = _dot_nt(qh, ks) * scale
            sink_col = jnp.full((rows, 1), sink_ref[layer, h], F32)
            (p,) = _softmax_parts([s], sink_col)
            o_ref[:, h * HEAD_DIM:(h + 1) * HEAD_DIM] = _dot(p.astype(BF16), vs).astype(BF16)


def _context_attention(q, k, v, sink, layer, seq_len):
    n_tok = q.shape[0]
    row = lambda w: pl.BlockSpec((seq_len, w), lambda b: (b, 0))
    return pl.pallas_call(
        functools.partial(_ctx_attn_kernel, layer=layer),
        grid=(n_tok // seq_len,),
        in_specs=[pl.BlockSpec(memory_space=pltpu.SMEM), row(ATTN_W), row(KV_W), row(KV_W)],
        out_specs=row(ATTN_W),
        out_shape=jax.ShapeDtypeStruct((n_tok, ATTN_W), BF16),
        compiler_params=_params(1),
        name="attention_ctx",
    )(sink, q, k, v)


def _lat_attn_kernel(sink_ref, q_ref, k_ref, v_ref, ck_ref, cv_ref, o_ref, *, layer):
    scale = HEAD_DIM ** -0.5
    seq_len = k_ref.shape[0]
    band = Q_BLOCK + 2 * WINDOW
    j = pl.program_id(1)
    start = pl.multiple_of(jnp.clip(j * Q_BLOCK - WINDOW, 0, seq_len - band), Q_BLOCK)
    qpos = j * Q_BLOCK + lax.broadcasted_iota(jnp.int32, (Q_BLOCK, band), 0)
    kpos = start + lax.broadcasted_iota(jnp.int32, (Q_BLOCK, band), 1)
    valid1 = jnp.abs(qpos - kpos) <= WINDOW
    valid = jnp.concatenate([valid1] * GROUP, axis=0)
    kb = k_ref[pl.ds(start, band), :]
    vb = v_ref[pl.ds(start, band), :]
    for kh in range(KV_HEADS):
        hd = slice(kh * HEAD_DIM, (kh + 1) * HEAD_DIM)
        ck = ck_ref[0, 0, kh].astype(BF16)
        cv = cv_ref[0, 0, kh].astype(BF16)
        heads = [kh * GROUP + g for g in range(GROUP)]
        qg = jnp.concatenate([q_ref[:, h * HEAD_DIM:(h + 1) * HEAD_DIM] for h in heads], axis=0)
        sink_col = jnp.concatenate(
            [jnp.full((Q_BLOCK, 1), sink_ref[layer, h], F32) for h in heads], axis=0)
        s_loc = jnp.where(valid, _dot_nt(qg, kb[:, hd]) * scale, NEG_INF)
        s_ctx = _dot_nt(qg, ck) * scale
        p_loc, p_ctx = _softmax_parts([s_loc, s_ctx], sink_col)
        o = _dot(p_loc.astype(BF16), vb[:, hd]) + _dot(p_ctx.astype(BF16), cv)
        for g, h in enumerate(heads):
            o_ref[:, h * HEAD_DIM:(h + 1) * HEAD_DIM] = o[g * Q_BLOCK:(g + 1) * Q_BLOCK].astype(BF16)


def _latent_attention(q, k, v, cache_k, cache_v, sink, layer, seq_len):
    n_tok = q.shape[0]
    n_batch = n_tok // seq_len
    n_blocks = seq_len // Q_BLOCK
    past_len = cache_k.shape[3]
    seq = lambda w: pl.BlockSpec((seq_len, w), lambda b, j: (b, 0))
    blk = pl.BlockSpec((Q_BLOCK, ATTN_W), lambda b, j: (b * n_blocks + j, 0))
    cache = pl.BlockSpec((1, 1, KV_HEADS, past_len, HEAD_DIM), lambda b, j: (b, layer, 0, 0, 0))
    return pl.pallas_call(
        functools.partial(_lat_attn_kernel, layer=layer),
        grid=(n_batch, n_blocks),
        in_specs=[pl.BlockSpec(memory_space=pltpu.SMEM), blk, seq(KV_W), seq(KV_W), cache, cache],
        out_specs=blk,
        out_shape=jax.ShapeDtypeStruct((n_tok, ATTN_W), BF16),
        compiler_params=_params(2),
        name="attention_lat",
    )(sink, q, k, v, cache_k, cache_v)


def _retention_kernel(*refs, n_heads, latent):
    if latent:
        (rq_ref, rk_ref, rv_ref, gf_ref, gb_ref, intra_ref, cross_ref, kdec_ref, gblk_ref, gn_ref,
         s0f_ref, s0b_ref, ret_ref, acc_ref) = refs
    else:
        (rq_ref, rk_ref, rv_ref, gf_ref, gb_ref, intra_ref, cross_ref, kdec_ref, gblk_ref, gn_ref,
         *_unused_prev, ret_ref, sf_ref, sb_ref, acc_ref) = refs
    n_blocks = rq_ref.shape[0] // RET_BLOCK
    for h in range(n_heads):
        hs = slice(h * RET_DK, (h + 1) * RET_DK)
        blocks = [slice(i * RET_BLOCK, (i + 1) * RET_BLOCK) for i in range(n_blocks)]
        q = [rq_ref[rows, hs] for rows in blocks]
        k = [rk_ref[rows, hs] for rows in blocks]
        v = [rv_ref[rows, hs] for rows in blocks]
        qk = [_dot_nt(q[i], k[i].astype(BF16)) for i in range(n_blocks)]
        for d, backward in enumerate((False, True)):
            g_ref = gb_ref if backward else gf_ref
            intra = intra_ref[d, h]
            cross = cross_ref[d, h]
            kdec = kdec_ref[d, h]
            g_block = gblk_ref[d, h]
            state = (s0b_ref if backward else s0f_ref)[0, 0, h] if latent else None
            order = list(range(n_blocks))
            if backward:
                order.reverse()
            for pos, i in enumerate(order):
                o = _dot((qk[i] * intra).astype(BF16), v[i])
                if state is not None:
                    o = o + _dot(q[i], state.astype(BF16)) * cross
                if pos + 1 < n_blocks or not latent:
                    kv = _dot_tn((k[i] * kdec).astype(BF16), v[i])
                    state = kv if state is None else state * g_block + kv
                mu = jnp.mean(o, axis=-1, keepdims=True)
                dev = o - mu
                var = jnp.mean(dev * dev, axis=-1, keepdims=True)
                y = dev * lax.rsqrt(var + NORM_EPS) * gn_ref[:, hs]
                gate = g_ref[blocks[i], hs]
                contrib = gate * jax.nn.sigmoid(gate) * y
                if backward:
                    ret_ref[blocks[i], hs] = (acc_ref[blocks[i], hs] + contrib).astype(BF16)
                else:
                    acc_ref[blocks[i], hs] = contrib
            if not latent:
                (sb_ref if backward else sf_ref)[0, 0, h] = state


def _retention(rq, rk, rv, gates, decay, gn, layer, seq_len, states0, prev_states):
    n_tok = rq.shape[0]
    n_batch = n_tok // seq_len
    latent = states0 is not None
    n_heads = 1 if latent else RET_HEADS
    head_blocks = RET_HEADS // n_heads
    width = n_heads * RET_DK
    gate_blocks = RET_VW // width
    seq = lambda off=0: pl.BlockSpec((seq_len, width), lambda b, hb: (b, off + hb))
    tab = lambda *tail: pl.BlockSpec((None, 2, n_heads) + tail,
                                     lambda b, hb: (layer, 0, hb) + (0,) * len(tail))
    in_specs = [seq(), seq(), seq(), seq(), seq(gate_blocks),
                tab(RET_BLOCK, RET_BLOCK), tab(RET_BLOCK, LANES), tab(RET_BLOCK, LANES), tab(1, LANES),
                pl.BlockSpec((None, 1, width), lambda b, hb: (layer, 0, hb))]
    args = [rq, rk, rv, gates, gates, *decay, gn]
    out_specs = [seq()]
    out_shape = [jax.ShapeDtypeStruct((n_tok, RET_VW), BF16)]
    aliases = {}
    if latent:
        spec = pl.BlockSpec((1, 1, n_heads, RET_DK, RET_DV), lambda b, hb: (b, layer, hb, 0, 0))
        in_specs += [spec, spec]
        args += list(states0)
    else:
        spec = pl.BlockSpec((1, 1, n_heads, RET_DK, RET_DV), lambda b, hb: (b, layer, hb, 0, 0))
        if prev_states is not None:
            for arr in prev_states:
                aliases[len(args)] = len(out_shape)
                in_specs.append(pl.BlockSpec(memory_space=pl.ANY))
                args.append(arr)
                out_specs.append(spec)
                out_shape.append(jax.ShapeDtypeStruct(arr.shape, F32))
        else:
            out_specs += [spec, spec]
            out_shape += [jax.ShapeDtypeStruct((n_batch, DEPTH, RET_HEADS, RET_DK, RET_DV), F32)] * 2
    return pl.pallas_call(
        functools.partial(_retention_kernel, n_heads=n_heads, latent=latent),
        grid=(n_batch, head_blocks),
        in_specs=in_specs,
        out_specs=out_specs,
        out_shape=out_shape,
        scratch_shapes=[pltpu.VMEM((seq_len, width), F32)],
        input_output_aliases=aliases,
        compiler_params=_params(2),
        name="retention_lat" if latent else "retention_ctx",
    )(*args)


def _post_kernel(*refs, final):
    if final:
        (attn_ref, ret_ref, ga_ref, gr_ref, x_ref, mod_ref, n2_ref,
         wba_ref, wbr_ref, wo_ref, w1_ref, w2_ref, fn_ref, out_ref) = refs
    else:
        (attn_ref, ret_ref, ga_ref, gr_ref, x_ref, mod_ref, n2_ref,
         wba_ref, wbr_ref, wo_ref, w1_ref, w2_ref, out_ref) = refs
    mod = mod_ref[...]
    part = lambda k: mod[:, k * D_MODEL:(k + 1) * D_MODEL]
    merged = (jax.nn.sigmoid(ga_ref[...]) * _dot(attn_ref[...], wba_ref[...])
              + jax.nn.sigmoid(gr_ref[...]) * _dot(ret_ref[...], wbr_ref[...]))
    x = x_ref[...] + part(2) * _dot(merged.astype(BF16), wo_ref[...])
    h2 = _modulated_norm(x, n2_ref[...], part(3), part(4)).astype(BF16)
    ff = None
    for c in range(0, D_FF, FF_CHUNK):
        a = jnp.maximum(_dot(h2, w1_ref[:, c:c + FF_CHUNK]), 0.0)
        t = _dot((a * a).astype(BF16), w2_ref[c:c + FF_CHUNK, :])
        ff = t if ff is None else ff + t
    x = x + part(5) * ff
    if final:
        x = x * lax.rsqrt(jnp.mean(x * x, axis=-1, keepdims=True) + NORM_EPS) * fn_ref[...]
    out_ref[...] = x


def _post(attn, ret, gates, x, mod, n2, w_ba, w_br, w_o, w_ff1, w_ff2, layer, seq_len, latent,
          final_gain):
    n_tok = x.shape[0]
    tiles_per_batch = seq_len // ROW_TILE
    final = final_gain is not None
    row = lambda w, c=0: pl.BlockSpec((ROW_TILE, w), lambda i, c=c: (i, c))
    in_specs = [
        row(ATTN_W), row(RET_VW), row(D_MODEL, 1), row(D_MODEL, 2), row(D_MODEL),
        _mod_spec(layer, tiles_per_batch if latent else None),
        _layer_spec((1, D_MODEL), layer),
        _layer_spec((ATTN_W, D_MODEL), layer), _layer_spec((RET_VW, D_MODEL), layer),
        _layer_spec((D_MODEL, D_MODEL), layer), _layer_spec((D_MODEL, D_FF), layer),
        _layer_spec((D_FF, D_MODEL), layer),
    ]
    args = [attn, ret, gates, gates, x, mod, n2, w_ba, w_br, w_o, w_ff1, w_ff2]
    if final:
        in_specs.append(_const_spec((1, D_MODEL)))
        args.append(final_gain)
    return pl.pallas_call(
        functools.partial(_post_kernel, final=final),
        grid=(n_tok // ROW_TILE,),
        in_specs=in_specs,
        out_specs=row(D_MODEL),
        out_shape=jax.ShapeDtypeStruct((n_tok, D_MODEL), F32),
        compiler_params=_params(1),
        name="post_final" if final else "post",
    )(*args)


def _rope_tables(seq_len):
    t = jnp.arange(seq_len, dtype=jnp.int32)
    rows = (t // GRID_W).astype(F32)
    cols = (t % GRID_W).astype(F32)

    def table(head_dim):
        half = head_dim // 2
        nf = half // 2
        inv = ROPE_BASE ** (-jnp.arange(nf, dtype=F32) / nf)
        lane = jnp.arange(LANES) % head_dim
        pos = jnp.where((lane < half)[None, :], rows[:, None], cols[:, None])
        ang = pos * inv[lane % nf][None, :]
        sign = jnp.where((lane % half) < nf, -1.0, 1.0)[None, :]
        return jnp.cos(ang), jnp.sin(ang) * sign

    cos_a, sin_a = table(HEAD_DIM)
    cos_r, sin_r = table(RET_DK)
    return cos_a, sin_a, cos_r, sin_r


def kernel(x_prompt, x_sample, cache_k, cache_v, state_ret_fwd, state_ret_bwd, c, c_ctx, norm1_g, norm2_g, final_norm_g, w_mod, b_mod, w_in, attn_sink, ret_decay_fwd, ret_decay_bwd, ret_gn_g, w_branch_attn, w_branch_ret, w_out, w_ff1, w_ff2):
    n_ctx, ctx_len, _ = x_prompt.shape
    n_lat, lat_len, _ = x_sample.shape
    xp = x_prompt.reshape(n_ctx * ctx_len, D_MODEL)
    xs = x_sample.reshape(n_lat * lat_len, D_MODEL)

    cond = jnp.concatenate([c_ctx[None, :], c], axis=0)
    cond8 = jnp.pad(cond, ((0, -cond.shape[0] % 8), (0, 0)))
    mod = _modulation(cond8, w_mod, b_mod).reshape(DEPTH, cond8.shape[0], 1, MOD_W)
    dec_rows = jnp.broadcast_to(jnp.stack([ret_decay_fwd, ret_decay_bwd], axis=1)[..., None],
                                (DEPTH, 2, RET_HEADS, LANES))
    decay = _decay_tables(dec_rows)
    tables = _rope_tables(lat_len)
    final_gain = final_norm_g.reshape(1, D_MODEL)

    n1 = norm1_g.reshape(DEPTH, 1, D_MODEL)
    n2 = norm2_g.reshape(DEPTH, 1, D_MODEL)
    gn = ret_gn_g.reshape(DEPTH, 1, RET_VW)
    w_in_b = w_in.astype(BF16)
    post_w = (n2, w_branch_attn.astype(BF16), w_branch_ret.astype(BF16), w_out.astype(BF16),
              w_ff1.astype(BF16), w_ff2.astype(BF16))

    caches = None
    states = None
    for l in range(DEPTH):
        last = l == DEPTH - 1
        q, k, v, rq, rk, rv, gates, *caches = _in_projection(xp, mod, n1, w_in_b, l, ctx_len, None,
                                                             caches)
        attn = _context_attention(q, k, v, attn_sink, l, ctx_len)
        ret, *states = _retention(rq, rk, rv, gates, decay, gn, l, ctx_len, None, states)
        xp = _post(attn, ret, gates, xp, mod, *post_w, l, ctx_len, False,
                   final_gain if last else None)
        q, k, v, rq, rk, rv, gates = _in_projection(xs, mod, n1, w_in_b, l, lat_len, tables, None)
        attn = _latent_attention(q, k, v, cache_k, cache_v, attn_sink, l, lat_len)
        (ret,) = _retention(rq, rk, rv, gates, decay, gn, l, lat_len,
                            (state_ret_fwd, state_ret_bwd), None)
        xs = _post(attn, ret, gates, xs, mod, *post_w, l, lat_len, True,
                   final_gain if last else None)

    return (xp.reshape(n_ctx, ctx_len, D_MODEL), xs.reshape(n_lat, lat_len, D_MODEL),
            caches[0], caches[1], states[0], states[1])
```

```python
import functools

import jax
import jax.numpy as jnp
from jax import lax
from jax.experimental import pallas as pl
from jax.experimental.pallas import tpu as pltpu

D_MODEL = 1024
DEPTH = 4
GRID_W = 64
N_HEADS = 8
HEAD_DIM = 64
KV_HEADS = 2
GROUP = N_HEADS // KV_HEADS
WINDOW = 128
Q_BLOCK = 128
RET_HEADS = 4
RET_DK = 128
RET_DV = 128
ATTN_W = N_HEADS * HEAD_DIM
KV_W = KV_HEADS * HEAD_DIM
RET_W = RET_HEADS * RET_DK
RET_VW = RET_HEADS * RET_DV
GATE_W = 2 * RET_VW + 2 * D_MODEL
IN_WIDTH = ATTN_W + 2 * KV_W + 2 * RET_W + RET_VW + GATE_W
D_FF = 4 * D_MODEL
MOD_W = 6 * D_MODEL
ROPE_BASE = 10000.0
NORM_EPS = 1e-6
NEG_INF = -1e30

OFF_Q = 0
OFF_K = OFF_Q + ATTN_W
OFF_V = OFF_K + KV_W
OFF_RQ = OFF_V + KV_W
OFF_RK = OFF_RQ + RET_W
OFF_RV = OFF_RK + RET_W
OFF_GATES = OFF_RV + RET_VW

LANES = 128
ROW_TILE = 512
CTX_ATTN_ROWS = 512
RET_BLOCK = 256
MOD_TILE_N = 1536
FF_CHUNK = 1024
VMEM_LIMIT = 56 * 1024 * 1024

BF16 = jnp.bfloat16
F32 = jnp.float32


def _params(n_axes):
    return pltpu.CompilerParams(dimension_semantics=("arbitrary",) * n_axes,
                                vmem_limit_bytes=VMEM_LIMIT)


def _const_spec(shape):
    nd = len(shape)
    return pl.BlockSpec(shape, lambda *_: (0,) * nd, pipeline_mode=pl.Buffered(1))


def _layer_spec(shape, layer):
    nd = len(shape)
    return pl.BlockSpec((None,) + tuple(shape), lambda *_: (layer,) + (0,) * nd,
                        pipeline_mode=pl.Buffered(1))


def _mod_spec(layer, tiles_per_batch):
    if tiles_per_batch is None:
        return pl.BlockSpec((None, None, 1, MOD_W), lambda *_: (layer, 0, 0, 0))
    return pl.BlockSpec((None, None, 1, MOD_W), lambda i: (layer, 1 + i // tiles_per_batch, 0, 0))


def _zero_other_layers(out_refs, layer):
    n_slots = out_refs[0].shape[1]
    if n_slots == 1:
        return 0
    for ref in out_refs:
        for other in range(n_slots):
            if other != layer:
                ref[:, other] = jnp.zeros((ref.shape[0],) + ref.shape[2:], ref.dtype)
    return layer


def _dot(a, b):
    return jnp.dot(a, b, preferred_element_type=F32)


def _dot_nt(a, b):
    return lax.dot_general(a, b, (((1,), (1,)), ((), ())), preferred_element_type=F32)


def _dot_tn(a, b):
    return lax.dot_general(a, b, (((0,), (0,)), ((), ())), preferred_element_type=F32)


def _mod_kernel(cond_ref, w_ref, b_ref, out_ref):
    cond = cond_ref[...]
    s = (cond * jax.nn.sigmoid(cond)).astype(BF16)
    out_ref[0] = _dot(s, w_ref[0].astype(BF16)) + b_ref[0]


def _modulation(cond8, w_mod, b_mod):
    n_rows = cond8.shape[0]
    return pl.pallas_call(
        _mod_kernel,
        grid=(DEPTH, MOD_W // MOD_TILE_N),
        in_specs=[
            pl.BlockSpec((n_rows, D_MODEL), lambda l, j: (0, 0)),
            pl.BlockSpec((1, D_MODEL, MOD_TILE_N), lambda l, j: (l, 0, j)),
            pl.BlockSpec((1, 1, MOD_TILE_N), lambda l, j: (l, 0, j)),
        ],
        out_specs=pl.BlockSpec((1, n_rows, MOD_TILE_N), lambda l, j: (l, 0, j)),
        out_shape=jax.ShapeDtypeStruct((DEPTH, n_rows, MOD_W), F32),
        compiler_params=_params(2),
        name="adaln_modulation",
    )(cond8, w_mod, b_mod.reshape(DEPTH, 1, MOD_W))


def _decay_kernel(dec_ref, intra_ref, cross_ref, kdec_ref, gblk_ref):
    backward = pl.program_id(1) == 1
    log_g = jnp.log(jax.nn.sigmoid(dec_ref[...]))
    n = RET_BLOCK
    row = lax.broadcasted_iota(jnp.int32, (n, n), 0).astype(F32)
    col = lax.broadcasted_iota(jnp.int32, (n, n), 1).astype(F32)
    diff = jnp.where(backward, col - row, row - col)
    row1 = row[:, :LANES]
    cross_pow = jnp.where(backward, n - row1, row1 + 1.0)
    kdec_pow = jnp.where(backward, row1, n - 1.0 - row1)
    for h in range(RET_HEADS):
        lg = log_g[h:h + 1, :]
        lg_wide = jnp.concatenate([lg] * (n // LANES), axis=1)
        intra_ref[h] = jnp.where(diff >= 0, jnp.exp(jnp.maximum(diff, 0.0) * lg_wide), 0.0)
        cross_ref[h] = jnp.exp(cross_pow * lg)
        kdec_ref[h] = jnp.exp(kdec_pow * lg)
        gblk_ref[h] = jnp.exp(n * lg)


def _decay_tables(dec_rows):
    lead = lambda *tail: pl.BlockSpec((None, None, RET_HEADS) + tail,
                                      lambda l, d: (l, d, 0) + (0,) * len(tail))
    shape = lambda *tail: jax.ShapeDtypeStruct((DEPTH, 2, RET_HEADS) + tail, F32)
    return pl.pallas_call(
        _decay_kernel,
        grid=(DEPTH, 2),
        in_specs=[lead(LANES)],
        out_specs=[lead(RET_BLOCK, RET_BLOCK), lead(RET_BLOCK, LANES), lead(RET_BLOCK, LANES),
                   lead(1, LANES)],
        out_shape=[shape(RET_BLOCK, RET_BLOCK), shape(RET_BLOCK, LANES), shape(RET_BLOCK, LANES),
                   shape(1, LANES)],
        compiler_params=_params(2),
        name="retention_decay_tables",
    )(dec_rows)


def _modulated_norm(x, gain, shift, scale):
    y = x * lax.rsqrt(jnp.mean(x * x, axis=-1, keepdims=True) + NORM_EPS) * gain
    return y * (1.0 + scale) + shift


def _rope_chunk(xc, cos, sin_signed, first_half, nf):
    partner = jnp.where(first_half, pltpu.roll(xc, LANES - nf, 1), pltpu.roll(xc, nf, 1))
    return xc * cos + partner * sin_signed


def _inproj_kernel(*refs, latent, layer):
    if latent:
        (x_ref, mod_ref, n1_ref, w_ref, cos_a, sin_a, cos_r, sin_r,
         q_ref, k_ref, v_ref, rq_ref, rk_ref, rv_ref, g_ref) = refs
    else:
        (x_ref, mod_ref, n1_ref, w_ref, *_unused_prev,
         q_ref, k_ref, v_ref, rq_ref, rk_ref, rv_ref, g_ref, kc_ref, vc_ref) = refs
        cos_a = sin_a = cos_r = sin_r = None
    mod = mod_ref[...]
    h = _modulated_norm(x_ref[...], n1_ref[...], mod[:, 0:D_MODEL], mod[:, D_MODEL:2 * D_MODEL])
    hb = h.astype(BF16)

    def proj(off, width):
        return _dot(hb, w_ref[:, off:off + width])

    def rope(p, cos_ref, sin_ref, nf):
        if not latent:
            return p
        lane = lax.broadcasted_iota(jnp.int32, (p.shape[0], LANES), 1)
        first_half = (lane % (2 * nf)) < nf
        cos = cos_ref[...]
        sin = sin_ref[...]
        parts = [_rope_chunk(p[:, c:c + LANES], cos, sin, first_half, nf)
                 for c in range(0, p.shape[1], LANES)]
        return parts[0] if len(parts) == 1 else jnp.concatenate(parts, axis=1)

    nf_a = HEAD_DIM // 4
    nf_r = RET_DK // 4
    q_ref[...] = rope(proj(OFF_Q, ATTN_W), cos_a, sin_a, nf_a).astype(BF16)
    k = rope(proj(OFF_K, KV_W), cos_a, sin_a, nf_a)
    k_ref[...] = k.astype(BF16)
    v = proj(OFF_V, KV_W)
    v_ref[...] = v.astype(BF16)
    if not latent:
        seq_len = kc_ref.shape[3]
        slot = _zero_other_layers((kc_ref, vc_ref), layer)
        for b in range(kc_ref.shape[0]):
            rows = slice(b * seq_len, (b + 1) * seq_len)
            for kh in range(KV_HEADS):
                kc_ref[b, slot, kh] = k[rows, kh * HEAD_DIM:(kh + 1) * HEAD_DIM]
                vc_ref[b, slot, kh] = v[rows, kh * HEAD_DIM:(kh + 1) * HEAD_DIM]
    rq_ref[...] = rope(proj(OFF_RQ, RET_W), cos_r, sin_r, nf_r).astype(BF16)
    rk = rope(proj(OFF_RK, RET_W), cos_r, sin_r, nf_r)
    rk_ref[...] = rk * (RET_DK ** -0.5)
    rv_ref[...] = proj(OFF_RV, RET_VW).astype(BF16)
    for c in range(0, GATE_W, D_MODEL):
        g_ref[:, c:c + D_MODEL] = proj(OFF_GATES + c, D_MODEL)


def _in_projection(x, mod, n1, w_in, layer, seq_len, tables, prev_caches):
    n_tok = x.shape[0]
    latent = tables is not None
    tiles_per_batch = seq_len // ROW_TILE
    row = lambda w: pl.BlockSpec((ROW_TILE, w), lambda i: (i, 0))
    in_specs = [
        row(D_MODEL),
        _mod_spec(layer, tiles_per_batch if latent else None),
        _layer_spec((1, D_MODEL), layer),
        _layer_spec((D_MODEL, IN_WIDTH), layer),
    ]
    args = [x, mod, n1, w_in]
    out_specs = [row(ATTN_W), row(KV_W), row(KV_W), row(RET_W), row(RET_W), row(RET_VW), row(GATE_W)]
    out_shape = [
        jax.ShapeDtypeStruct((n_tok, ATTN_W), BF16),
        jax.ShapeDtypeStruct((n_tok, KV_W), BF16),
        jax.ShapeDtypeStruct((n_tok, KV_W), BF16),
        jax.ShapeDtypeStruct((n_tok, RET_W), BF16),
        jax.ShapeDtypeStruct((n_tok, RET_W), F32),
        jax.ShapeDtypeStruct((n_tok, RET_VW), BF16),
        jax.ShapeDtypeStruct((n_tok, GATE_W), F32),
    ]
    aliases = {}
    if latent:
        tab = pl.BlockSpec((ROW_TILE, LANES), lambda i: (i % tiles_per_batch, 0))
        in_specs += [tab] * 4
        args += list(tables)
    else:
        n_batch = n_tok // seq_len
        cache = pl.BlockSpec((ROW_TILE // seq_len, 1, KV_HEADS, seq_len, HEAD_DIM),
                             lambda i: (i, layer, 0, 0, 0))
        if prev_caches is not None:
            for cache_arr in prev_caches:
                aliases[len(args)] = len(out_shape)
                in_specs.append(pl.BlockSpec(memory_space=pl.ANY))
                args.append(cache_arr)
                out_specs.append(cache)
                out_shape.append(jax.ShapeDtypeStruct(cache_arr.shape, F32))
        else:
            assert layer == 0
            all_layers = pl.BlockSpec((ROW_TILE // seq_len, DEPTH, KV_HEADS, seq_len, HEAD_DIM),
                                      lambda i: (i, 0, 0, 0, 0))
            out_specs += [all_layers, all_layers]
            out_shape += [jax.ShapeDtypeStruct((n_batch, DEPTH, KV_HEADS, seq_len, HEAD_DIM), F32)] * 2
    return pl.pallas_call(
        functools.partial(_inproj_kernel, latent=latent, layer=layer),
        grid=(n_tok // ROW_TILE,),
        in_specs=in_specs,
        out_specs=out_specs,
        out_shape=out_shape,
        input_output_aliases=aliases,
        compiler_params=_params(1),
        name="in_projection_lat" if latent else "in_projection_ctx",
    )(*args)


def _softmax_t(score_blocks, sink):
    m = sink
    for s in score_blocks:
        m = jnp.maximum(jnp.max(s, axis=0, keepdims=True), m)
    es = [jnp.exp(s - m) for s in score_blocks]
    denom = jnp.exp(sink - m)
    for e in es:
        denom = denom + jnp.sum(e, axis=0, keepdims=True)
    return [e.astype(BF16) for e in es], 1.0 / denom


def _store_heads(o_ref, rows, heads, o_t, inv):
    for pair in range(0, len(heads), 2):
        halves = [o_t[:, g * Q_BLOCK:(g + 1) * Q_BLOCK] * inv[g] for g in (pair, pair + 1)]
        both = jnp.transpose(jnp.concatenate(halves, axis=0))
        h0 = heads[pair]
        o_ref[rows, h0 * HEAD_DIM:(h0 + 2) * HEAD_DIM] = both.astype(BF16)


def _ctx_attn_kernel(sink_ref, q_ref, k_ref, v_ref, o_ref, *, layer, seq_len):
    scale = HEAD_DIM ** -0.5
    chunks = [(b, kh, c) for b in range(0, q_ref.shape[0], seq_len) for kh in range(KV_HEADS)
              for c in range(b, b + seq_len, Q_BLOCK)]
    scores = []
    for b, kh, c in chunks:
        heads = [kh * GROUP + g for g in range(GROUP)]
        q4 = jnp.concatenate([q_ref[c:c + Q_BLOCK, h * HEAD_DIM:(h + 1) * HEAD_DIM] for h in heads],
                             axis=0)
        scores.append(_dot_nt(k_ref[b:b + seq_len, kh * HEAD_DIM:(kh + 1) * HEAD_DIM], q4) * scale)
    v_t = {b: jnp.transpose(v_ref[b:b + seq_len, :].astype(F32)).astype(BF16)
           for b in range(0, q_ref.shape[0], seq_len)}
    for (b, kh, c), s4 in zip(chunks, scores):
        heads = [kh * GROUP + g for g in range(GROUP)]
        es, inv = [], []
        for g, h in enumerate(heads):
            (e,), r = _softmax_t([s4[:, g * Q_BLOCK:(g + 1) * Q_BLOCK]], sink_ref[layer, h])
            es.append(e)
            inv.append(r)
        o_t = _dot(v_t[b][kh * HEAD_DIM:(kh + 1) * HEAD_DIM, :], jnp.concatenate(es, axis=1))
        _store_heads(o_ref, slice(c, c + Q_BLOCK), heads, o_t, inv)


def _context_attention(q, k, v, sink, layer, seq_len):
    n_tok = q.shape[0]
    row = lambda w: pl.BlockSpec((CTX_ATTN_ROWS, w), lambda b: (b, 0))
    return pl.pallas_call(
        functools.partial(_ctx_attn_kernel, layer=layer, seq_len=seq_len),
        grid=(n_tok // CTX_ATTN_ROWS,),
        in_specs=[pl.BlockSpec(memory_space=pltpu.SMEM), row(ATTN_W), row(KV_W), row(KV_W)],
        out_specs=row(ATTN_W),
        out_shape=jax.ShapeDtypeStruct((n_tok, ATTN_W), BF16),
        compiler_params=_params(1),
        name="attention_ctx",
    )(sink, q, k, v)


def _lat_attn_kernel(sink_ref, q_ref, k_ref, v_ref, ck_ref, cv_ref, o_ref,
                     vt_ref, ckb_ref, cvt_ref, *, layer):
    scale = HEAD_DIM ** -0.5
    n_blocks = k_ref.shape[0] // Q_BLOCK
    band_blocks = 1 + 2 * WINDOW // Q_BLOCK
    band = band_blocks * Q_BLOCK
    j = pl.program_id(1)

    @pl.when(j == 0)
    def _per_batch_element():
        for t in range(n_blocks):
            vt_ref[t] = jnp.transpose(
                v_ref[t * Q_BLOCK:(t + 1) * Q_BLOCK, :].astype(F32)).astype(BF16)
        cv = jnp.concatenate([cv_ref[0, 0, kh] for kh in range(KV_HEADS)], axis=1)
        cvt_ref[...] = jnp.transpose(cv).astype(BF16)
        for kh in range(KV_HEADS):
            ckb_ref[kh] = ck_ref[0, 0, kh].astype(BF16)

    first = jnp.clip(j - WINDOW // Q_BLOCK, 0, n_blocks - band_blocks)
    start = pl.multiple_of(first * Q_BLOCK, Q_BLOCK)
    kpos = start + lax.broadcasted_iota(jnp.int32, (band, Q_BLOCK), 0)
    qpos = j * Q_BLOCK + lax.broadcasted_iota(jnp.int32, (band, Q_BLOCK), 1)
    valid = jnp.abs(qpos - kpos) <= WINDOW
    kb = k_ref[pl.ds(start, band), :]
    scores = []
    for kh in range(KV_HEADS):
        hd = slice(kh * HEAD_DIM, (kh + 1) * HEAD_DIM)
        heads = [kh * GROUP + g for g in range(GROUP)]
        q4 = jnp.concatenate([q_ref[:, h * HEAD_DIM:(h + 1) * HEAD_DIM] for h in heads], axis=0)
        scores.append((_dot_nt(kb[:, hd], q4) * scale,
                       _dot_nt(ckb_ref[kh], q4) * scale))
    for kh, (s_loc4, s_ctx4) in enumerate(scores):
        hd = slice(kh * HEAD_DIM, (kh + 1) * HEAD_DIM)
        heads = [kh * GROUP + g for g in range(GROUP)]
        e_loc, e_ctx, inv = [], [], []
        for g, h in enumerate(heads):
            cols = slice(g * Q_BLOCK, (g + 1) * Q_BLOCK)
            (el, ec), r = _softmax_t([jnp.where(valid, s_loc4[:, cols], NEG_INF), s_ctx4[:, cols]],
                                     sink_ref[layer, h])
            e_loc.append(el)
            e_ctx.append(ec)
            inv.append(r)
        e_loc4 = jnp.concatenate(e_loc, axis=1)
        o_t = _dot(cvt_ref[hd, :], jnp.concatenate(e_ctx, axis=1))
        for t in range(band_blocks):
            o_t = o_t + _dot(vt_ref[first + t, hd, :], e_loc4[t * Q_BLOCK:(t + 1) * Q_BLOCK, :])
        _store_heads(o_ref, slice(None), heads, o_t, inv)


def _latent_attention(q, k, v, cache_k, cache_v, sink, layer, seq_len):
    n_tok = q.shape[0]
    n_batch = n_tok // seq_len
    n_blocks = seq_len // Q_BLOCK
    past_len = cache_k.shape[3]
    seq = lambda w: pl.BlockSpec((seq_len, w), lambda b, j: (b, 0))
    blk = pl.BlockSpec((Q_BLOCK, ATTN_W), lambda b, j: (b * n_blocks + j, 0))
    cache = pl.BlockSpec((1, 1, KV_HEADS, past_len, HEAD_DIM), lambda b, j: (b, layer, 0, 0, 0))
    return pl.pallas_call(
        functools.partial(_lat_attn_kernel, layer=layer),
        grid=(n_batch, n_blocks),
        in_specs=[pl.BlockSpec(memory_space=pltpu.SMEM), blk, seq(KV_W), seq(KV_W), cache, cache],
        out_specs=blk,
        out_shape=jax.ShapeDtypeStruct((n_tok, ATTN_W), BF16),
        scratch_shapes=[pltpu.VMEM((n_blocks, KV_W, Q_BLOCK), BF16),
                        pltpu.VMEM((KV_HEADS, past_len, HEAD_DIM), BF16),
                        pltpu.VMEM((KV_W, past_len), BF16)],
        compiler_params=_params(2),
        name="attention_lat",
    )(sink, q, k, v, cache_k, cache_v)


def _retention_kernel(*refs, n_heads, latent, layer):
    if latent:
        (rq_ref, rk_ref, rv_ref, gf_ref, gb_ref, intra_ref, cross_ref, kdec_ref, gblk_ref, gn_ref,
         s0f_ref, s0b_ref, ret_ref, acc_ref) = refs
    else:
        (rq_ref, rk_ref, rv_ref, gf_ref, gb_ref, intra_ref, cross_ref, kdec_ref, gblk_ref, gn_ref,
         *_unused_prev, ret_ref, sf_ref, sb_ref, acc_ref) = refs
    n_blocks = rq_ref.shape[0] // RET_BLOCK
    slot = 0 if latent else _zero_other_layers((sf_ref, sb_ref), layer)
    blocks = [slice(i * RET_BLOCK, (i + 1) * RET_BLOCK) for i in range(n_blocks)]
    head_cols = [slice(h * RET_DK, (h + 1) * RET_DK) for h in range(n_heads)]
    qk = {(h, i): _dot_nt(rq_ref[blocks[i], head_cols[h]], rk_ref[blocks[i], head_cols[h]].astype(BF16))
          for h in range(n_heads) for i in range(n_blocks)}
    o_intra, kv = {}, {}
    for h in range(n_heads):
        for d in range(2):
            for i in range(n_blocks):
                v_i = rv_ref[blocks[i], head_cols[h]]
                o_intra[h, d, i] = _dot((qk[h, i] * intra_ref[d, h]).astype(BF16), v_i)
                last = i == (0 if d else n_blocks - 1)
                if not (latent and last):
                    k_i = rk_ref[blocks[i], head_cols[h]]
                    kv[h, d, i] = _dot_tn((k_i * kdec_ref[d, h]).astype(BF16), v_i)
    for h in range(n_heads):
        hs = head_cols[h]
        for d, backward in enumerate((False, True)):
            g_ref = gb_ref if backward else gf_ref
            cross = cross_ref[d, h]
            g_block = gblk_ref[d, h]
            state = (s0b_ref if backward else s0f_ref)[0, 0, h] if latent else None
            order = list(range(n_blocks))
            if backward:
                order.reverse()
            for i in order:
                o = o_intra[h, d, i]
                if state is not None:
                    o = o + _dot(rq_ref[blocks[i], hs], state.astype(BF16)) * cross
                if (h, d, i) in kv:
                    state = kv[h, d, i] if state is None else state * g_block + kv[h, d, i]
                mu = jnp.mean(o, axis=-1, keepdims=True)
                dev = o - mu
                var = jnp.mean(dev * dev, axis=-1, keepdims=True)
                y = dev * lax.rsqrt(var + NORM_EPS) * gn_ref[:, hs]
                gate = g_ref[blocks[i], hs]
                contrib = gate * jax.nn.sigmoid(gate) * y
                if backward:
                    ret_ref[blocks[i], hs] = (acc_ref[blocks[i], hs] + contrib).astype(BF16)
                else:
                    acc_ref[blocks[i], hs] = contrib
            if not latent:
                (sb_ref if backward else sf_ref)[0, slot, h] = state


def _retention(rq, rk, rv, gates, decay, gn, layer, seq_len, states0, prev_states):
    n_tok = rq.shape[0]
    n_batch = n_tok // seq_len
    latent = states0 is not None
    n_heads = 1 if latent else RET_HEADS
    head_blocks = RET_HEADS // n_heads
    width = n_heads * RET_DK
    gate_blocks = RET_VW // width
    seq = lambda off=0: pl.BlockSpec((seq_len, width), lambda b, hb: (b, off + hb))
    tab = lambda *tail: pl.BlockSpec((None, 2, n_heads) + tail,
                                     lambda b, hb: (layer, 0, hb) + (0,) * len(tail))
    in_specs = [seq(), seq(), seq(), seq(), seq(gate_blocks),
                tab(RET_BLOCK, RET_BLOCK), tab(RET_BLOCK, LANES), tab(RET_BLOCK, LANES), tab(1, LANES),
                pl.BlockSpec((None, 1, width), lambda b, hb: (layer, 0, hb))]
    args = [rq, rk, rv, gates, gates, *decay, gn]
    out_specs = [seq()]
    out_shape = [jax.ShapeDtypeStruct((n_tok, RET_VW), BF16)]
    aliases = {}
    if latent:
        spec = pl.BlockSpec((1, 1, n_heads, RET_DK, RET_DV), lambda b, hb: (b, layer, hb, 0, 0))
        in_specs += [spec, spec]
        args += list(states0)
    else:
        spec = pl.BlockSpec((1, 1, n_heads, RET_DK, RET_DV), lambda b, hb: (b, layer, hb, 0, 0))
        if prev_states is not None:
            for arr in prev_states:
                aliases[len(args)] = len(out_shape)
                in_specs.append(pl.BlockSpec(memory_space=pl.ANY))
                args.append(arr)
                out_specs.append(spec)
                out_shape.append(jax.ShapeDtypeStruct(arr.shape, F32))
        else:
            assert layer == 0 and head_blocks == 1
            all_layers = pl.BlockSpec((1, DEPTH, n_heads, RET_DK, RET_DV), lambda b, hb: (b, 0, 0, 0, 0))
            out_specs += [all_layers, all_layers]
            out_shape += [jax.ShapeDtypeStruct((n_batch, DEPTH, RET_HEADS, RET_DK, RET_DV), F32)] * 2
    return pl.pallas_call(
        functools.partial(_retention_kernel, n_heads=n_heads, latent=latent, layer=layer),
        grid=(n_batch, head_blocks),
        in_specs=in_specs,
        out_specs=out_specs,
        out_shape=out_shape,
        scratch_shapes=[pltpu.VMEM((seq_len, width), F32)],
        input_output_aliases=aliases,
        compiler_params=_params(2),
        name="retention_lat" if latent else "retention_ctx",
    )(*args)


def _post_kernel(*refs, final):
    if final:
        (attn_ref, ret_ref, ga_ref, gr_ref, x_ref, mod_ref, n2_ref,
         wba_ref, wbr_ref, wo_ref, w1_ref, w2_ref, fn_ref, out_ref) = refs
    else:
        (attn_ref, ret_ref, ga_ref, gr_ref, x_ref, mod_ref, n2_ref,
         wba_ref, wbr_ref, wo_ref, w1_ref, w2_ref, out_ref) = refs
    mod = mod_ref[...]
    part = lambda k: mod[:, k * D_MODEL:(k + 1) * D_MODEL]
    merged = (jax.nn.sigmoid(ga_ref[...]) * _dot(attn_ref[...], wba_ref[...])
              + jax.nn.sigmoid(gr_ref[...]) * _dot(ret_ref[...], wbr_ref[...]))
    x = x_ref[...] + part(2) * _dot(merged.astype(BF16), wo_ref[...])
    h2 = _modulated_norm(x, n2_ref[...], part(3), part(4)).astype(BF16)
    ff = None
    for c in range(0, D_FF, FF_CHUNK):
        a = jnp.maximum(_dot(h2, w1_ref[:, c:c + FF_CHUNK]), 0.0)
        t = _dot((a * a).astype(BF16), w2_ref[c:c + FF_CHUNK, :])
        ff = t if ff is None else ff + t
    x = x + part(5) * ff
    if final:
        x = x * lax.rsqrt(jnp.mean(x * x, axis=-1, keepdims=True) + NORM_EPS) * fn_ref[...]
    out_ref[...] = x


def _post(attn, ret, gates, x, mod, n2, w_ba, w_br, w_o, w_ff1, w_ff2, layer, seq_len, latent,
          final_gain):
    n_tok = x.shape[0]
    tiles_per_batch = seq_len // ROW_TILE
    final = final_gain is not None
    row = lambda w, c=0: pl.BlockSpec((ROW_TILE, w), lambda i, c=c: (i, c))
    in_specs = [
        row(ATTN_W), row(RET_VW), row(D_MODEL, 1), row(D_MODEL, 2), row(D_MODEL),
        _mod_spec(layer, tiles_per_batch if latent else None),
        _layer_spec((1, D_MODEL), layer),
        _layer_spec((ATTN_W, D_MODEL), layer), _layer_spec((RET_VW, D_MODEL), layer),
        _layer_spec((D_MODEL, D_MODEL), layer), _layer_spec((D_MODEL, D_FF), layer),
        _layer_spec((D_FF, D_MODEL), layer),
    ]
    args = [attn, ret, gates, gates, x, mod, n2, w_ba, w_br, w_o, w_ff1, w_ff2]
    if final:
        in_specs.append(_const_spec((1, D_MODEL)))
        args.append(final_gain)
    return pl.pallas_call(
        functools.partial(_post_kernel, final=final),
        grid=(n_tok // ROW_TILE,),
        in_specs=in_specs,
        out_specs=row(D_MODEL),
        out_shape=jax.ShapeDtypeStruct((n_tok, D_MODEL), F32),
        compiler_params=_params(1),
        name="post_final" if final else "post",
    )(*args)


def _rope_tables(seq_len):
    t = jnp.arange(seq_len, dtype=jnp.int32)
    rows = (t // GRID_W).astype(F32)
    cols = (t % GRID_W).astype(F32)

    def table(head_dim):
        half = head_dim // 2
        nf = half // 2
        inv = ROPE_BASE ** (-jnp.arange(nf, dtype=F32) / nf)
        lane = jnp.arange(LANES) % head_dim
        pos = jnp.where((lane < half)[None, :], rows[:, None], cols[:, None])
        ang = pos * inv[lane % nf][None, :]
        sign = jnp.where((lane % half) < nf, -1.0, 1.0)[None, :]
        return jnp.cos(ang), jnp.sin(ang) * sign

    cos_a, sin_a = table(HEAD_DIM)
    cos_r, sin_r = table(RET_DK)
    return cos_a, sin_a, cos_r, sin_r


def kernel(x_prompt, x_sample, cache_k, cache_v, state_ret_fwd, state_ret_bwd, c, c_ctx, norm1_g, norm2_g, final_norm_g, w_mod, b_mod, w_in, attn_sink, ret_decay_fwd, ret_decay_bwd, ret_gn_g, w_branch_attn, w_branch_ret, w_out, w_ff1, w_ff2):
    n_ctx, ctx_len, _ = x_prompt.shape
    n_lat, lat_len, _ = x_sample.shape
    xp = x_prompt.reshape(n_ctx * ctx_len, D_MODEL)
    xs = x_sample.reshape(n_lat * lat_len, D_MODEL)

    cond = jnp.concatenate([c_ctx[None, :], c], axis=0)
    cond8 = jnp.pad(cond, ((0, -cond.shape[0] % 8), (0, 0)))
    mod = _modulation(cond8, w_mod, b_mod).reshape(DEPTH, cond8.shape[0], 1, MOD_W)
    dec_rows = jnp.broadcast_to(jnp.stack([ret_decay_fwd, ret_decay_bwd], axis=1)[..., None],
                                (DEPTH, 2, RET_HEADS, LANES))
    decay = _decay_tables(dec_rows)
    tables = _rope_tables(lat_len)
    final_gain = final_norm_g.reshape(1, D_MODEL)

    n1 = norm1_g.reshape(DEPTH, 1, D_MODEL)
    n2 = norm2_g.reshape(DEPTH, 1, D_MODEL)
    gn = ret_gn_g.reshape(DEPTH, 1, RET_VW)
    w_in_b = w_in.astype(BF16)
    post_w = (n2, w_branch_attn.astype(BF16), w_branch_ret.astype(BF16), w_out.astype(BF16),
              w_ff1.astype(BF16), w_ff2.astype(BF16))

    caches = None
    states = None
    for l in range(DEPTH):
        last = l == DEPTH - 1
        q, k, v, rq, rk, rv, gates, *caches = _in_projection(xp, mod, n1, w_in_b, l, ctx_len, None,
                                                             caches)
        attn = _context_attention(q, k, v, attn_sink, l, ctx_len)
        ret, *states = _retention(rq, rk, rv, gates, decay, gn, l, ctx_len, None, states)
        xp = _post(attn, ret, gates, xp, mod, *post_w, l, ctx_len, False,
                   final_gain if last else None)
        q, k, v, rq, rk, rv, gates = _in_projection(xs, mod, n1, w_in_b, l, lat_len, tables, None)
        attn = _latent_attention(q, k, v, cache_k, cache_v, attn_sink, l, lat_len)
        (ret,) = _retention(rq, rk, rv, gates, decay, gn, l, lat_len,
                            (state_ret_fwd, state_ret_bwd), None)
        xs = _post(attn, ret, gates, xs, mod, *post_w, l, lat_len, True,
                   final_gain if last else None)

    return (xp.reshape(n_ctx, ctx_len, D_MODEL), xs.reshape(n_lat, lat_len, D_MODEL),
            caches[0], caches[1], states[0], states[1])
```

```python
import functools

import jax
import jax.numpy as jnp
from jax import lax
from jax.experimental import pallas as pl
from jax.experimental.pallas import tpu as pltpu

D_MODEL = 1024
DEPTH = 4
GRID_W = 64
N_HEADS = 8
HEAD_DIM = 64
KV_HEADS = 2
GROUP = N_HEADS // KV_HEADS
WINDOW = 128
Q_BLOCK = 128
RET_HEADS = 4
RET_DK = 128
RET_DV = 128
ATTN_W = N_HEADS * HEAD_DIM
KV_W = KV_HEADS * HEAD_DIM
RET_W = RET_HEADS * RET_DK
RET_VW = RET_HEADS * RET_DV
GATE_W = 2 * RET_VW + 2 * D_MODEL
IN_WIDTH = ATTN_W + 2 * KV_W + 2 * RET_W + RET_VW + GATE_W
D_FF = 4 * D_MODEL
MOD_W = 6 * D_MODEL
ROPE_BASE = 10000.0
NORM_EPS = 1e-6
NEG_INF = -1e30

OFF_Q = 0
OFF_K = OFF_Q + ATTN_W
OFF_V = OFF_K + KV_W
OFF_RQ = OFF_V + KV_W
OFF_RK = OFF_RQ + RET_W
OFF_RV = OFF_RK + RET_W
OFF_RGF = OFF_RV + RET_VW
OFF_RGB = OFF_RGF + RET_VW
OFF_GA = OFF_RGB + RET_VW

LANES = 128
ROW_TILE = 512
RET_BLOCK = 256
MOD_TILE_N = 1536
FF_CHUNK = 1024
VMEM_LIMIT = 58 * 1024 * 1024

BF16 = jnp.bfloat16
F32 = jnp.float32


def _params(n_axes):
    return pltpu.CompilerParams(dimension_semantics=("arbitrary",) * n_axes,
                                vmem_limit_bytes=VMEM_LIMIT)


def _const_spec(shape):
    nd = len(shape)
    return pl.BlockSpec(shape, lambda *_: (0,) * nd, pipeline_mode=pl.Buffered(1))


def _layer_spec(shape, layer):
    nd = len(shape)
    return pl.BlockSpec((None,) + tuple(shape), lambda *_: (layer,) + (0,) * nd,
                        pipeline_mode=pl.Buffered(1))


def _mod_spec(layer, tiles_per_batch):
    if tiles_per_batch is None:
        return pl.BlockSpec((None, None, 1, MOD_W), lambda *_: (layer, 0, 0, 0))
    return pl.BlockSpec((None, None, 1, MOD_W), lambda i: (layer, 1 + i // tiles_per_batch, 0, 0))


def _zero_other_layers(out_refs, layer):
    n_slots = out_refs[0].shape[1]
    if n_slots == 1:
        return 0
    for ref in out_refs:
        for other in range(n_slots):
            if other != layer:
                ref[:, other] = jnp.zeros((ref.shape[0],) + ref.shape[2:], ref.dtype)
    return layer


def _dot(a, b):
    return jnp.dot(a, b, preferred_element_type=F32)


def _dot_nt(a, b):
    return lax.dot_general(a, b, (((1,), (1,)), ((), ())), preferred_element_type=F32)


def _dot_tn(a, b):
    return lax.dot_general(a, b, (((0,), (0,)), ((), ())), preferred_element_type=F32)


def _mod_kernel(cond_ref, w_ref, b_ref, out_ref):
    cond = cond_ref[...]
    s = (cond * jax.nn.sigmoid(cond)).astype(BF16)
    out_ref[0] = _dot(s, w_ref[0].astype(BF16)) + b_ref[0]


def _modulation(cond8, w_mod, b_mod):
    n_rows = cond8.shape[0]
    return pl.pallas_call(
        _mod_kernel,
        grid=(DEPTH, MOD_W // MOD_TILE_N),
        in_specs=[
            pl.BlockSpec((n_rows, D_MODEL), lambda l, j: (0, 0)),
            pl.BlockSpec((1, D_MODEL, MOD_TILE_N), lambda l, j: (l, 0, j)),
            pl.BlockSpec((1, 1, MOD_TILE_N), lambda l, j: (l, 0, j)),
        ],
        out_specs=pl.BlockSpec((1, n_rows, MOD_TILE_N), lambda l, j: (l, 0, j)),
        out_shape=jax.ShapeDtypeStruct((DEPTH, n_rows, MOD_W), F32),
        compiler_params=_params(2),
        name="adaln_modulation",
    )(cond8, w_mod, b_mod.reshape(DEPTH, 1, MOD_W))


def _decay_kernel(dec_ref, intra_ref, cross_ref, kdec_ref, gblk_ref):
    backward = pl.program_id(1) == 1
    log_g = jnp.log(jax.nn.sigmoid(dec_ref[...]))
    n = RET_BLOCK
    row = lax.broadcasted_iota(jnp.int32, (n, n), 0).astype(F32)
    col = lax.broadcasted_iota(jnp.int32, (n, n), 1).astype(F32)
    diff = jnp.where(backward, col - row, row - col)
    row1 = row[:, :LANES]
    cross_pow = jnp.where(backward, n - row1, row1 + 1.0)
    kdec_pow = jnp.where(backward, row1, n - 1.0 - row1)
    for h in range(RET_HEADS):
        lg = log_g[h:h + 1, :]
        lg_wide = jnp.concatenate([lg] * (n // LANES), axis=1)
        intra_ref[h] = jnp.where(diff >= 0, jnp.exp(jnp.maximum(diff, 0.0) * lg_wide), 0.0)
        cross_ref[h] = jnp.exp(cross_pow * lg)
        kdec_ref[h] = jnp.exp(kdec_pow * lg)
        gblk_ref[h] = jnp.exp(n * lg)


def _decay_tables(dec_rows):
    lead = lambda *tail: pl.BlockSpec((None, None, RET_HEADS) + tail,
                                      lambda l, d: (l, d, 0) + (0,) * len(tail))
    shape = lambda *tail: jax.ShapeDtypeStruct((DEPTH, 2, RET_HEADS) + tail, F32)
    return pl.pallas_call(
        _decay_kernel,
        grid=(DEPTH, 2),
        in_specs=[lead(LANES)],
        out_specs=[lead(RET_BLOCK, RET_BLOCK), lead(RET_BLOCK, LANES), lead(RET_BLOCK, LANES),
                   lead(1, LANES)],
        out_shape=[shape(RET_BLOCK, RET_BLOCK), shape(RET_BLOCK, LANES), shape(RET_BLOCK, LANES),
                   shape(1, LANES)],
        compiler_params=_params(2),
        name="retention_decay_tables",
    )(dec_rows)


def _decay_specs(layer, n_heads, index_map):
    mode = {"pipeline_mode": pl.Buffered(1)} if n_heads == RET_HEADS else {}
    tab = lambda *tail: pl.BlockSpec((None, 2, n_heads) + tail,
                                     lambda *idx: (layer, 0, index_map(*idx)) + (0,) * len(tail), **mode)
    return [tab(RET_BLOCK, RET_BLOCK), tab(RET_BLOCK, LANES), tab(RET_BLOCK, LANES), tab(1, LANES)]


def _modulated_norm(x, gain, shift, scale):
    y = x * lax.rsqrt(jnp.mean(x * x, axis=-1, keepdims=True) + NORM_EPS) * gain
    return y * (1.0 + scale) + shift


def _normed_input(x_ref, mod_ref, n1_ref):
    mod = mod_ref[...]
    h = _modulated_norm(x_ref[...], n1_ref[...], mod[:, 0:D_MODEL], mod[:, D_MODEL:2 * D_MODEL])
    return h.astype(BF16)


def _rope_chunk(xc, cos, sin_signed, first_half, nf):
    partner = jnp.where(first_half, pltpu.roll(xc, LANES - nf, 1), pltpu.roll(xc, nf, 1))
    return xc * cos + partner * sin_signed


def _lat_inproj_kernel(x_ref, mod_ref, n1_ref, w_ref, cos_a, sin_a, cos_r, sin_r,
                       q_ref, k_ref, v_ref, rq_ref, rk_ref, rv_ref, g_ref):
    hb = _normed_input(x_ref, mod_ref, n1_ref)

    def proj(off, width):
        return _dot(hb, w_ref[:, off:off + width])

    def rope(p, cos_ref, sin_ref, nf):
        lane = lax.broadcasted_iota(jnp.int32, (p.shape[0], LANES), 1)
        first_half = (lane % (2 * nf)) < nf
        cos = cos_ref[...]
        sin = sin_ref[...]
        parts = [_rope_chunk(p[:, c:c + LANES], cos, sin, first_half, nf)
                 for c in range(0, p.shape[1], LANES)]
        return parts[0] if len(parts) == 1 else jnp.concatenate(parts, axis=1)

    nf_a = HEAD_DIM // 4
    nf_r = RET_DK // 4
    q_ref[...] = rope(proj(OFF_Q, ATTN_W), cos_a, sin_a, nf_a).astype(BF16)
    k_ref[...] = rope(proj(OFF_K, KV_W), cos_a, sin_a, nf_a).astype(BF16)
    v_ref[...] = proj(OFF_V, KV_W).astype(BF16)
    rq_ref[...] = rope(proj(OFF_RQ, RET_W), cos_r, sin_r, nf_r).astype(BF16)
    rk_ref[...] = rope(proj(OFF_RK, RET_W), cos_r, sin_r, nf_r) * (RET_DK ** -0.5)
    rv_ref[...] = proj(OFF_RV, RET_VW).astype(BF16)
    for c in range(0, GATE_W, D_MODEL):
        g_ref[:, c:c + D_MODEL] = proj(OFF_RGF + c, D_MODEL)


def _latent_in_projection(x, mod, n1, w_in, layer, seq_len, tables):
    n_tok = x.shape[0]
    tiles_per_batch = seq_len // ROW_TILE
    row = lambda w: pl.BlockSpec((ROW_TILE, w), lambda i: (i, 0))
    tab = pl.BlockSpec((ROW_TILE, LANES), lambda i: (i % tiles_per_batch, 0))
    widths = (ATTN_W, KV_W, KV_W, RET_W, RET_W, RET_VW, GATE_W)
    dtypes = (BF16, BF16, BF16, BF16, F32, BF16, F32)
    return pl.pallas_call(
        _lat_inproj_kernel,
        grid=(n_tok // ROW_TILE,),
        in_specs=[row(D_MODEL), _mod_spec(layer, tiles_per_batch), _layer_spec((1, D_MODEL), layer),
                  _layer_spec((D_MODEL, IN_WIDTH), layer), tab, tab, tab, tab],
        out_specs=[row(w) for w in widths],
        out_shape=[jax.ShapeDtypeStruct((n_tok, w), dt) for w, dt in zip(widths, dtypes)],
        compiler_params=_params(1),
        name="in_projection_lat",
    )(x, mod, n1, w_in, *tables)


def _softmax_t(score_blocks, sink):
    m = sink
    for s in score_blocks:
        m = jnp.maximum(jnp.max(s, axis=0, keepdims=True), m)
    es = [jnp.exp(s - m) for s in score_blocks]
    denom = jnp.exp(sink - m)
    for e in es:
        denom = denom + jnp.sum(e, axis=0, keepdims=True)
    return [e.astype(BF16) for e in es], 1.0 / denom


def _store_heads(o_ref, rows, heads, o_t, inv):
    for pair in range(0, len(heads), 2):
        halves = [o_t[:, g * Q_BLOCK:(g + 1) * Q_BLOCK] * inv[g] for g in (pair, pair + 1)]
        both = jnp.transpose(jnp.concatenate(halves, axis=0))
        h0 = heads[pair]
        o_ref[rows, h0 * HEAD_DIM:(h0 + 2) * HEAD_DIM] = both.astype(BF16)


def _ctx_attention_scores(q_ref, k_ref, seq_len):
    scale = HEAD_DIM ** -0.5
    out = []
    for b in range(0, q_ref.shape[0], seq_len):
        for kh in range(KV_HEADS):
            for c in range(b, b + seq_len, Q_BLOCK):
                heads = [kh * GROUP + g for g in range(GROUP)]
                q4 = jnp.concatenate(
                    [q_ref[c:c + Q_BLOCK, h * HEAD_DIM:(h + 1) * HEAD_DIM] for h in heads], axis=0)
                keys = k_ref[b:b + seq_len, kh * HEAD_DIM:(kh + 1) * HEAD_DIM]
                out.append(((b, kh, c), _dot_nt(keys, q4) * scale))
    return out


def _ctx_attention_outputs(scores, b, v_ref, sink_ref, o_ref, layer, seq_len):
    v_t = jnp.transpose(v_ref[b:b + seq_len, :].astype(F32)).astype(BF16)
    for (sb, kh, c), s4 in scores:
        if sb != b:
            continue
        heads = [kh * GROUP + g for g in range(GROUP)]
        es, inv = [], []
        for g, h in enumerate(heads):
            (e,), r = _softmax_t([s4[:, g * Q_BLOCK:(g + 1) * Q_BLOCK]], sink_ref[layer, h])
            es.append(e)
            inv.append(r)
        o_t = _dot(v_t[kh * HEAD_DIM:(kh + 1) * HEAD_DIM, :], jnp.concatenate(es, axis=1))
        _store_heads(o_ref, slice(c, c + Q_BLOCK), heads, o_t, inv)


def _lat_attn_kernel(sink_ref, q_ref, k_ref, v_ref, ck_ref, cv_ref, o_ref,
                     vt_ref, ckb_ref, cvt_ref, *, layer):
    scale = HEAD_DIM ** -0.5
    n_blocks = k_ref.shape[0] // Q_BLOCK
    band_blocks = 1 + 2 * WINDOW // Q_BLOCK
    band = band_blocks * Q_BLOCK
    j = pl.program_id(1)

    @pl.when(j == 0)
    def _per_batch_element():
        for t in range(n_blocks):
            vt_ref[t] = jnp.transpose(
                v_ref[t * Q_BLOCK:(t + 1) * Q_BLOCK, :].astype(F32)).astype(BF16)
        cv = jnp.concatenate([cv_ref[0, 0, kh] for kh in range(KV_HEADS)], axis=1)
        cvt_ref[...] = jnp.transpose(cv).astype(BF16)
        for kh in range(KV_HEADS):
            ckb_ref[kh] = ck_ref[0, 0, kh].astype(BF16)

    first = jnp.clip(j - WINDOW // Q_BLOCK, 0, n_blocks - band_blocks)
    start = pl.multiple_of(first * Q_BLOCK, Q_BLOCK)
    kpos = start + lax.broadcasted_iota(jnp.int32, (band, Q_BLOCK), 0)
    qpos = j * Q_BLOCK + lax.broadcasted_iota(jnp.int32, (band, Q_BLOCK), 1)
    valid = jnp.abs(qpos - kpos) <= WINDOW
    kb = k_ref[pl.ds(start, band), :]
    scores = []
    for kh in range(KV_HEADS):
        hd = slice(kh * HEAD_DIM, (kh + 1) * HEAD_DIM)
        heads = [kh * GROUP + g for g in range(GROUP)]
        q4 = jnp.concatenate([q_ref[:, h * HEAD_DIM:(h + 1) * HEAD_DIM] for h in heads], axis=0)
        scores.append((_dot_nt(kb[:, hd], q4) * scale,
                       _dot_nt(ckb_ref[kh], q4) * scale))
    for kh, (s_loc4, s_ctx4) in enumerate(scores):
        hd = slice(kh * HEAD_DIM, (kh + 1) * HEAD_DIM)
        heads = [kh * GROUP + g for g in range(GROUP)]
        e_loc, e_ctx, inv = [], [], []
        for g, h in enumerate(heads):
            cols = slice(g * Q_BLOCK, (g + 1) * Q_BLOCK)
            (el, ec), r = _softmax_t([jnp.where(valid, s_loc4[:, cols], NEG_INF), s_ctx4[:, cols]],
                                     sink_ref[layer, h])
            e_loc.append(el)
            e_ctx.append(ec)
            inv.append(r)
        e_loc4 = jnp.concatenate(e_loc, axis=1)
        o_t = _dot(cvt_ref[hd, :], jnp.concatenate(e_ctx, axis=1))
        for t in range(band_blocks):
            o_t = o_t + _dot(vt_ref[first + t, hd, :], e_loc4[t * Q_BLOCK:(t + 1) * Q_BLOCK, :])
        _store_heads(o_ref, slice(None), heads, o_t, inv)


def _latent_attention(q, k, v, cache_k, cache_v, sink, layer, seq_len):
    n_tok = q.shape[0]
    n_batch = n_tok // seq_len
    n_blocks = seq_len // Q_BLOCK
    past_len = cache_k.shape[3]
    seq = lambda w: pl.BlockSpec((seq_len, w), lambda b, j: (b, 0))
    blk = pl.BlockSpec((Q_BLOCK, ATTN_W), lambda b, j: (b * n_blocks + j, 0))
    cache = pl.BlockSpec((1, 1, KV_HEADS, past_len, HEAD_DIM), lambda b, j: (b, layer, 0, 0, 0))
    return pl.pallas_call(
        functools.partial(_lat_attn_kernel, layer=layer),
        grid=(n_batch, n_blocks),
        in_specs=[pl.BlockSpec(memory_space=pltpu.SMEM), blk, seq(KV_W), seq(KV_W), cache, cache],
        out_specs=blk,
        out_shape=jax.ShapeDtypeStruct((n_tok, ATTN_W), BF16),
        scratch_shapes=[pltpu.VMEM((n_blocks, KV_W, Q_BLOCK), BF16),
                        pltpu.VMEM((KV_HEADS, past_len, HEAD_DIM), BF16),
                        pltpu.VMEM((KV_W, past_len), BF16)],
        compiler_params=_params(2),
        name="attention_lat",
    )(sink, q, k, v, cache_k, cache_v)


def _retention_products(rq_ref, rk_ref, rv_ref, intra_ref, kdec_ref, n_heads, final_state):
    n_blocks = rq_ref.shape[0] // RET_BLOCK
    blocks = [slice(i * RET_BLOCK, (i + 1) * RET_BLOCK) for i in range(n_blocks)]
    head_cols = [slice(h * RET_DK, (h + 1) * RET_DK) for h in range(n_heads)]
    qk = {(h, i): _dot_nt(rq_ref[blocks[i], head_cols[h]], rk_ref[blocks[i], head_cols[h]].astype(BF16))
          for h in range(n_heads) for i in range(n_blocks)}
    o_intra, kv = {}, {}
    for h in range(n_heads):
        for d in range(2):
            for i in range(n_blocks):
                v_i = rv_ref[blocks[i], head_cols[h]]
                o_intra[h, d, i] = _dot((qk[h, i] * intra_ref[d, h]).astype(BF16), v_i)
                last = i == (0 if d else n_blocks - 1)
                if final_state or not last:
                    k_i = rk_ref[blocks[i], head_cols[h]]
                    kv[h, d, i] = _dot_tn((k_i * kdec_ref[d, h]).astype(BF16), v_i)
    return o_intra, kv


def _retention_outputs(products, rq_ref, gf_ref, gb_ref, cross_ref, gblk_ref, gn_ref, ret_ref, acc_ref,
                       n_heads, initial_state, store_state):
    o_intra, kv = products
    n_blocks = rq_ref.shape[0] // RET_BLOCK
    blocks = [slice(i * RET_BLOCK, (i + 1) * RET_BLOCK) for i in range(n_blocks)]
    for h in range(n_heads):
        hs = slice(h * RET_DK, (h + 1) * RET_DK)
        for d, backward in enumerate((False, True)):
            g_ref = gb_ref if backward else gf_ref
            cross = cross_ref[d, h]
            g_block = gblk_ref[d, h]
            state = initial_state(backward, h)
            order = list(range(n_blocks))
            if backward:
                order.reverse()
            for i in order:
                o = o_intra[h, d, i]
                if state is not None:
                    o = o + _dot(rq_ref[blocks[i], hs], state.astype(BF16)) * cross
                if (h, d, i) in kv:
                    state = kv[h, d, i] if state is None else state * g_block + kv[h, d, i]
                mu = jnp.mean(o, axis=-1, keepdims=True)
                dev = o - mu
                var = jnp.mean(dev * dev, axis=-1, keepdims=True)
                y = dev * lax.rsqrt(var + NORM_EPS) * gn_ref[:, hs]
                gate = g_ref[blocks[i], hs]
                contrib = gate * jax.nn.sigmoid(gate) * y
                if backward:
                    ret_ref[blocks[i], hs] = (acc_ref[blocks[i], hs] + contrib).astype(BF16)
                else:
                    acc_ref[blocks[i], hs] = contrib
            if store_state is not None:
                store_state(backward, h, state)


def _lat_retention_kernel(rq_ref, rk_ref, rv_ref, gf_ref, gb_ref, intra_ref, cross_ref, kdec_ref,
                          gblk_ref, gn_ref, s0f_ref, s0b_ref, ret_ref, acc_ref):
    products = _retention_products(rq_ref, rk_ref, rv_ref, intra_ref, kdec_ref, 1, False)
    _retention_outputs(products, rq_ref, gf_ref, gb_ref, cross_ref, gblk_ref, gn_ref, ret_ref, acc_ref,
                       1, lambda backward, h: (s0b_ref if backward else s0f_ref)[0, 0, h], None)


def _latent_retention(rq, rk, rv, gates, decay, gn, layer, seq_len, states0):
    n_tok = rq.shape[0]
    gate_blocks = RET_VW // RET_DK
    seq = lambda off=0: pl.BlockSpec((seq_len, RET_DK), lambda b, h: (b, off + h))
    state = pl.BlockSpec((1, 1, 1, RET_DK, RET_DV), lambda b, h: (b, layer, h, 0, 0))
    return pl.pallas_call(
        _lat_retention_kernel,
        grid=(n_tok // seq_len, RET_HEADS),
        in_specs=[seq(), seq(), seq(), seq(), seq(gate_blocks),
                  *_decay_specs(layer, 1, lambda b, h: h),
                  pl.BlockSpec((None, 1, RET_DK), lambda b, h: (layer, 0, h)), state, state],
        out_specs=seq(),
        out_shape=jax.ShapeDtypeStruct((n_tok, RET_VW), BF16),
        scratch_shapes=[pltpu.VMEM((seq_len, RET_DK), F32)],
        compiler_params=_params(2),
        name="retention_lat",
    )(rq, rk, rv, gates, gates, *decay, gn, *states0)


N_CTX_MIX_OUT = 7
N_CTX_MIX_SCRATCH = 9


def _ctx_mix_kernel(*refs, layer, seq_len):
    n_in = len(refs) - N_CTX_MIX_OUT - N_CTX_MIX_SCRATCH
    (sink_ref, x_ref, mod_ref, n1_ref, w_ref, intra_ref, cross_ref, kdec_ref, gblk_ref,
     gn_ref) = refs[:10]
    attn_ref, ret_ref, g_ref, kc_ref, vc_ref, sf_ref, sb_ref = refs[n_in:n_in + N_CTX_MIX_OUT]
    q_s, k_s, v_s, rq_s, rk_s, rv_s, gf_s, gb_s, acc_s = refs[n_in + N_CTX_MIX_OUT:]
    hb = _normed_input(x_ref, mod_ref, n1_ref)

    def proj(off, width):
        return _dot(hb, w_ref[:, off:off + width])

    batch_rows = [slice(b, b + seq_len) for b in range(0, x_ref.shape[0], seq_len)]
    q_s[...] = proj(OFF_Q, ATTN_W).astype(BF16)
    k = proj(OFF_K, KV_W)
    v = proj(OFF_V, KV_W)
    k_s[...] = k.astype(BF16)
    v_s[...] = v.astype(BF16)
    slot = _zero_other_layers((kc_ref, vc_ref), layer)
    for b, rows in enumerate(batch_rows):
        for kh in range(KV_HEADS):
            kc_ref[b, slot, kh] = k[rows, kh * HEAD_DIM:(kh + 1) * HEAD_DIM]
            vc_ref[b, slot, kh] = v[rows, kh * HEAD_DIM:(kh + 1) * HEAD_DIM]
    scores = _ctx_attention_scores(q_s, k_s, seq_len)
    rq_s[...] = proj(OFF_RQ, RET_W).astype(BF16)
    rk_s[...] = proj(OFF_RK, RET_W) * (RET_DK ** -0.5)
    rv_s[...] = proj(OFF_RV, RET_VW).astype(BF16)
    products = [_retention_products(rq_s.at[rows], rk_s.at[rows], rv_s.at[rows], intra_ref, kdec_ref,
                                    RET_HEADS, True) for rows in batch_rows]
    gf_s[...] = proj(OFF_RGF, RET_VW)
    gb_s[...] = proj(OFF_RGB, RET_VW)
    state_slot = _zero_other_layers((sf_ref, sb_ref), layer)

    def attention_outputs(b):
        _ctx_attention_outputs(scores, batch_rows[b].start, v_s, sink_ref, attn_ref, layer, seq_len)

    def retention_outputs(b):
        rows = batch_rows[b]

        def store_state(backward, h, state):
            (sb_ref if backward else sf_ref)[b, state_slot, h] = state

        _retention_outputs(products[b], rq_s.at[rows], gf_s.at[rows], gb_s.at[rows], cross_ref, gblk_ref,
                           gn_ref, ret_ref.at[rows], acc_s.at[rows], RET_HEADS,
                           lambda backward, h: None, store_state)

    stages = ([functools.partial(attention_outputs, b) for b in range(len(batch_rows))]
              + [functools.partial(retention_outputs, b) for b in range(len(batch_rows))])
    chunk = 2 * D_MODEL // len(stages)
    for n, stage in enumerate(stages):
        g_ref[:, n * chunk:(n + 1) * chunk] = proj(OFF_GA + n * chunk, chunk)
        stage()


def _context_mixer(x, mod, n1, w_in, sink, decay, gn, layer, seq_len, prev_outputs):
    n_tok = x.shape[0]
    n_batch = n_tok // seq_len
    per_tile = ROW_TILE // seq_len
    row = lambda w: pl.BlockSpec((ROW_TILE, w), lambda i: (i, 0))
    in_specs = [pl.BlockSpec(memory_space=pltpu.SMEM), row(D_MODEL), _mod_spec(layer, None),
                _layer_spec((1, D_MODEL), layer), _layer_spec((D_MODEL, IN_WIDTH), layer),
                *_decay_specs(layer, RET_HEADS, lambda i: 0), _layer_spec((1, RET_VW), layer)]
    args = [sink, x, mod, n1, w_in, *decay, gn]
    out_specs = [row(ATTN_W), row(RET_VW), row(2 * D_MODEL)]
    out_shape = [jax.ShapeDtypeStruct((n_tok, ATTN_W), BF16), jax.ShapeDtypeStruct((n_tok, RET_VW), BF16),
                 jax.ShapeDtypeStruct((n_tok, 2 * D_MODEL), F32)]
    cache_dims = (KV_HEADS, seq_len, HEAD_DIM)
    state_dims = (RET_HEADS, RET_DK, RET_DV)
    aliases = {}
    for idx, dims in enumerate((cache_dims, cache_dims, state_dims, state_dims)):
        if prev_outputs is None:
            assert layer == 0
            out_specs.append(pl.BlockSpec((per_tile, DEPTH) + dims, lambda i: (i, 0, 0, 0, 0)))
        else:
            aliases[len(args)] = len(out_shape)
            in_specs.append(pl.BlockSpec(memory_space=pl.ANY))
            args.append(prev_outputs[idx])
            out_specs.append(pl.BlockSpec((per_tile, 1) + dims, lambda i: (i, layer, 0, 0, 0)))
        out_shape.append(jax.ShapeDtypeStruct((n_batch, DEPTH) + dims, F32))
    assert len(out_shape) == N_CTX_MIX_OUT
    scratch = [pltpu.VMEM((ROW_TILE, ATTN_W), BF16), pltpu.VMEM((ROW_TILE, KV_W), BF16),
               pltpu.VMEM((ROW_TILE, KV_W), BF16), pltpu.VMEM((ROW_TILE, RET_W), BF16),
               pltpu.VMEM((ROW_TILE, RET_W), F32), pltpu.VMEM((ROW_TILE, RET_VW), BF16),
               pltpu.VMEM((ROW_TILE, RET_VW), F32), pltpu.VMEM((ROW_TILE, RET_VW), F32),
               pltpu.VMEM((ROW_TILE, RET_VW), F32)]
    assert len(scratch) == N_CTX_MIX_SCRATCH
    return pl.pallas_call(
        functools.partial(_ctx_mix_kernel, layer=layer, seq_len=seq_len),
        grid=(n_tok // ROW_TILE,),
        in_specs=in_specs,
        out_specs=out_specs,
        out_shape=out_shape,
        scratch_shapes=scratch,
        input_output_aliases=aliases,
        compiler_params=_params(1),
        name="context_mixer",
    )(*args)


def _post_kernel(*refs, final):
    if final:
        (attn_ref, ret_ref, ga_ref, gr_ref, x_ref, mod_ref, n2_ref,
         wba_ref, wbr_ref, wo_ref, w1_ref, w2_ref, fn_ref, out_ref) = refs
    else:
        (attn_ref, ret_ref, ga_ref, gr_ref, x_ref, mod_ref, n2_ref,
         wba_ref, wbr_ref, wo_ref, w1_ref, w2_ref, out_ref) = refs
    mod = mod_ref[...]
    part = lambda k: mod[:, k * D_MODEL:(k + 1) * D_MODEL]
    merged = (jax.nn.sigmoid(ga_ref[...]) * _dot(attn_ref[...], wba_ref[...])
              + jax.nn.sigmoid(gr_ref[...]) * _dot(ret_ref[...], wbr_ref[...]))
    x = x_ref[...] + part(2) * _dot(merged.astype(BF16), wo_ref[...])
    h2 = _modulated_norm(x, n2_ref[...], part(3), part(4)).astype(BF16)
    ff = None
    for c in range(0, D_FF, FF_CHUNK):
        a = jnp.maximum(_dot(h2, w1_ref[:, c:c + FF_CHUNK]), 0.0)
        t = _dot((a * a).astype(BF16), w2_ref[c:c + FF_CHUNK, :])
        ff = t if ff is None else ff + t
    x = x + part(5) * ff
    if final:
        x = x * lax.rsqrt(jnp.mean(x * x, axis=-1, keepdims=True) + NORM_EPS) * fn_ref[...]
    out_ref[...] = x


def _post(attn, ret, gates, merge_gate_block, x, mod, n2, w_ba, w_br, w_o, w_ff1, w_ff2, layer,
          seq_len, latent, final_gain):
    n_tok = x.shape[0]
    tiles_per_batch = seq_len // ROW_TILE
    final = final_gain is not None
    row = lambda w, c=0: pl.BlockSpec((ROW_TILE, w), lambda i, c=c: (i, c))
    in_specs = [
        row(ATTN_W), row(RET_VW), row(D_MODEL, merge_gate_block), row(D_MODEL, merge_gate_block + 1),
        row(D_MODEL),
        _mod_spec(layer, tiles_per_batch if latent else None),
        _layer_spec((1, D_MODEL), layer),
        _layer_spec((ATTN_W, D_MODEL), layer), _layer_spec((RET_VW, D_MODEL), layer),
        _layer_spec((D_MODEL, D_MODEL), layer), _layer_spec((D_MODEL, D_FF), layer),
        _layer_spec((D_FF, D_MODEL), layer),
    ]
    args = [attn, ret, gates, gates, x, mod, n2, w_ba, w_br, w_o, w_ff1, w_ff2]
    if final:
        in_specs.append(_const_spec((1, D_MODEL)))
        args.append(final_gain)
    return pl.pallas_call(
        functools.partial(_post_kernel, final=final),
        grid=(n_tok // ROW_TILE,),
        in_specs=in_specs,
        out_specs=row(D_MODEL),
        out_shape=jax.ShapeDtypeStruct((n_tok, D_MODEL), F32),
        compiler_params=_params(1),
        name="post_final" if final else "post",
    )(*args)


def _rope_tables(seq_len):
    t = jnp.arange(seq_len, dtype=jnp.int32)
    rows = (t // GRID_W).astype(F32)
    cols = (t % GRID_W).astype(F32)

    def table(head_dim):
        half = head_dim // 2
        nf = half // 2
        inv = ROPE_BASE ** (-jnp.arange(nf, dtype=F32) / nf)
        lane = jnp.arange(LANES) % head_dim
        pos = jnp.where((lane < half)[None, :], rows[:, None], cols[:, None])
        ang = pos * inv[lane % nf][None, :]
        sign = jnp.where((lane % half) < nf, -1.0, 1.0)[None, :]
        return jnp.cos(ang), jnp.sin(ang) * sign

    cos_a, sin_a = table(HEAD_DIM)
    cos_r, sin_r = table(RET_DK)
    return cos_a, sin_a, cos_r, sin_r


def kernel(x_prompt, x_sample, cache_k, cache_v, state_ret_fwd, state_ret_bwd, c, c_ctx, norm1_g, norm2_g, final_norm_g, w_mod, b_mod, w_in, attn_sink, ret_decay_fwd, ret_decay_bwd, ret_gn_g, w_branch_attn, w_branch_ret, w_out, w_ff1, w_ff2):
    n_ctx, ctx_len, _ = x_prompt.shape
    n_lat, lat_len, _ = x_sample.shape
    xp = x_prompt.reshape(n_ctx * ctx_len, D_MODEL)
    xs = x_sample.reshape(n_lat * lat_len, D_MODEL)

    cond = jnp.concatenate([c_ctx[None, :], c], axis=0)
    cond8 = jnp.pad(cond, ((0, -cond.shape[0] % 8), (0, 0)))
    mod = _modulation(cond8, w_mod, b_mod).reshape(DEPTH, cond8.shape[0], 1, MOD_W)
    dec_rows = jnp.broadcast_to(jnp.stack([ret_decay_fwd, ret_decay_bwd], axis=1)[..., None],
                                (DEPTH, 2, RET_HEADS, LANES))
    decay = _decay_tables(dec_rows)
    tables = _rope_tables(lat_len)
    final_gain = final_norm_g.reshape(1, D_MODEL)

    n1 = norm1_g.reshape(DEPTH, 1, D_MODEL)
    n2 = norm2_g.reshape(DEPTH, 1, D_MODEL)
    gn = ret_gn_g.reshape(DEPTH, 1, RET_VW)
    w_in_b = w_in.astype(BF16)
    post_w = (n2, w_branch_attn.astype(BF16), w_branch_ret.astype(BF16), w_out.astype(BF16),
              w_ff1.astype(BF16), w_ff2.astype(BF16))

    ctx_outputs = None
    for l in range(DEPTH):
        last = l == DEPTH - 1
        attn, ret, gates, *ctx_outputs = _context_mixer(xp, mod, n1, w_in_b, attn_sink, decay, gn, l,
                                                        ctx_len, ctx_outputs)
        xp = _post(attn, ret, gates, 0, xp, mod, *post_w, l, ctx_len, False,
                   final_gain if last else None)
        q, k, v, rq, rk, rv, gates = _latent_in_projection(xs, mod, n1, w_in_b, l, lat_len, tables)
        attn = _latent_attention(q, k, v, cache_k, cache_v, attn_sink, l, lat_len)
        ret = _latent_retention(rq, rk, rv, gates, decay, gn, l, lat_len,
                                (state_ret_fwd, state_ret_bwd))
        xs = _post(attn, ret, gates, (OFF_GA - OFF_RGF) // D_MODEL, xs, mod, *post_w, l, lat_len, True,
                   final_gain if last else None)

    return (xp.reshape(n_ctx, ctx_len, D_MODEL), xs.reshape(n_lat, lat_len, D_MODEL), *ctx_outputs)
```

```python
import functools

import jax
import jax.numpy as jnp
import numpy as np
from jax import lax
from jax.experimental import pallas as pl
from jax.experimental.pallas import tpu as pltpu

D_MODEL = 1024
DEPTH = 4
GRID_W = 64
N_HEADS = 8
HEAD_DIM = 64
KV_HEADS = 2
GROUP = N_HEADS // KV_HEADS
WINDOW = 128
Q_BLOCK = 128
RET_HEADS = 4
RET_DK = 128
RET_DV = 128
ATTN_W = N_HEADS * HEAD_DIM
KV_W = KV_HEADS * HEAD_DIM
RET_W = RET_HEADS * RET_DK
RET_VW = RET_HEADS * RET_DV
GATE_W = 2 * RET_VW + 2 * D_MODEL
IN_WIDTH = ATTN_W + 2 * KV_W + 2 * RET_W + RET_VW + GATE_W
D_FF = 4 * D_MODEL
MOD_W = 6 * D_MODEL
ROPE_BASE = 10000.0
NORM_EPS = 1e-6
NEG_INF = -1e30

OFF_Q = 0
OFF_K = OFF_Q + ATTN_W
OFF_V = OFF_K + KV_W
OFF_RQ = OFF_V + KV_W
OFF_RK = OFF_RQ + RET_W
OFF_RV = OFF_RK + RET_W
OFF_RGF = OFF_RV + RET_VW
OFF_RGB = OFF_RGF + RET_VW
OFF_GA = OFF_RGB + RET_VW

LANES = 128
ROW_TILE = 512
RET_BLOCK = 256
MOD_TILE_N = 1536
FF_CHUNK = 1024
VMEM_LIMIT = 58 * 1024 * 1024

BF16 = jnp.bfloat16
F32 = jnp.float32


def _params(n_axes):
    return pltpu.CompilerParams(dimension_semantics=("arbitrary",) * n_axes,
                                vmem_limit_bytes=VMEM_LIMIT)


def _const_spec(shape):
    nd = len(shape)
    return pl.BlockSpec(shape, lambda *_: (0,) * nd, pipeline_mode=pl.Buffered(1))


def _layer_spec(shape, layer):
    nd = len(shape)
    return pl.BlockSpec((None,) + tuple(shape), lambda *_: (layer,) + (0,) * nd,
                        pipeline_mode=pl.Buffered(1))


def _mod_spec(layer, tiles_per_batch):
    if tiles_per_batch is None:
        return pl.BlockSpec((None, None, 1, MOD_W), lambda *_: (layer, 0, 0, 0))
    return pl.BlockSpec((None, None, 1, MOD_W), lambda i: (layer, 1 + i // tiles_per_batch, 0, 0))


def _zero_other_layers(out_refs, layer):
    n_slots = out_refs[0].shape[1]
    if n_slots == 1:
        return 0
    for ref in out_refs:
        for other in range(n_slots):
            if other != layer:
                ref[:, other] = jnp.zeros((ref.shape[0],) + ref.shape[2:], ref.dtype)
    return layer


def _dot(a, b):
    return jnp.dot(a, b, preferred_element_type=F32)


def _dot_nt(a, b):
    return lax.dot_general(a, b, (((1,), (1,)), ((), ())), preferred_element_type=F32)


def _dot_tn(a, b):
    return lax.dot_general(a, b, (((0,), (0,)), ((), ())), preferred_element_type=F32)


def _mod_kernel(cond_ref, w_ref, b_ref, out_ref):
    cond = cond_ref[...]
    s = (cond * jax.nn.sigmoid(cond)).astype(BF16)
    out_ref[0] = _dot(s, w_ref[0].astype(BF16)) + b_ref[0]


def _modulation(cond8, w_mod, b_mod):
    n_rows = cond8.shape[0]
    return pl.pallas_call(
        _mod_kernel,
        grid=(DEPTH, MOD_W // MOD_TILE_N),
        in_specs=[
            pl.BlockSpec((n_rows, D_MODEL), lambda l, j: (0, 0)),
            pl.BlockSpec((1, D_MODEL, MOD_TILE_N), lambda l, j: (l, 0, j)),
            pl.BlockSpec((1, 1, MOD_TILE_N), lambda l, j: (l, 0, j)),
        ],
        out_specs=pl.BlockSpec((1, n_rows, MOD_TILE_N), lambda l, j: (l, 0, j)),
        out_shape=jax.ShapeDtypeStruct((DEPTH, n_rows, MOD_W), F32),
        compiler_params=_params(2),
        name="adaln_modulation",
    )(cond8, w_mod, b_mod.reshape(DEPTH, 1, MOD_W))


def _decay_kernel(dec_ref, intra_ref, cross_ref, kdec_ref, gblk_ref):
    backward = pl.program_id(1) == 1
    log_g = jnp.log(jax.nn.sigmoid(dec_ref[...]))
    n = RET_BLOCK
    row = lax.broadcasted_iota(jnp.int32, (n, n), 0).astype(F32)
    col = lax.broadcasted_iota(jnp.int32, (n, n), 1).astype(F32)
    diff = jnp.where(backward, col - row, row - col)
    row1 = row[:, :LANES]
    cross_pow = jnp.where(backward, n - row1, row1 + 1.0)
    kdec_pow = jnp.where(backward, row1, n - 1.0 - row1)
    for h in range(RET_HEADS):
        lg = log_g[h:h + 1, :]
        lg_wide = jnp.concatenate([lg] * (n // LANES), axis=1)
        intra_ref[h] = jnp.where(diff >= 0, jnp.exp(jnp.maximum(diff, 0.0) * lg_wide), 0.0)
        cross_ref[h] = jnp.exp(cross_pow * lg)
        kdec_ref[h] = jnp.exp(kdec_pow * lg)
        gblk_ref[h] = jnp.exp(n * lg)


def _decay_tables(dec_rows):
    lead = lambda *tail: pl.BlockSpec((None, None, RET_HEADS) + tail,
                                      lambda l, d: (l, d, 0) + (0,) * len(tail))
    shape = lambda *tail: jax.ShapeDtypeStruct((DEPTH, 2, RET_HEADS) + tail, F32)
    return pl.pallas_call(
        _decay_kernel,
        grid=(DEPTH, 2),
        in_specs=[lead(LANES)],
        out_specs=[lead(RET_BLOCK, RET_BLOCK), lead(RET_BLOCK, LANES), lead(RET_BLOCK, LANES),
                   lead(1, LANES)],
        out_shape=[shape(RET_BLOCK, RET_BLOCK), shape(RET_BLOCK, LANES), shape(RET_BLOCK, LANES),
                   shape(1, LANES)],
        compiler_params=_params(2),
        name="retention_decay_tables",
    )(dec_rows)


def _decay_specs(layer, n_heads, index_map):
    mode = {"pipeline_mode": pl.Buffered(1)} if n_heads == RET_HEADS else {}
    tab = lambda *tail: pl.BlockSpec((None, 2, n_heads) + tail,
                                     lambda *idx: (layer, 0, index_map(*idx)) + (0,) * len(tail), **mode)
    return [tab(RET_BLOCK, RET_BLOCK), tab(RET_BLOCK, LANES), tab(RET_BLOCK, LANES), tab(1, LANES)]


def _modulated_norm(x, gain, shift, scale):
    y = x * lax.rsqrt(jnp.mean(x * x, axis=-1, keepdims=True) + NORM_EPS) * gain
    return y * (1.0 + scale) + shift


def _normed_input(x_ref, mod_ref, n1_ref):
    mod = mod_ref[...]
    h = _modulated_norm(x_ref[...], n1_ref[...], mod[:, 0:D_MODEL], mod[:, D_MODEL:2 * D_MODEL])
    return h.astype(BF16)


def _rope_chunk(xc, cos, sin_signed, first_half, nf):
    partner = jnp.where(first_half, pltpu.roll(xc, LANES - nf, 1), pltpu.roll(xc, nf, 1))
    return xc * cos + partner * sin_signed


def _lat_inproj_kernel(x_ref, mod_ref, n1_ref, w_ref, cos_a, sin_a, cos_r, sin_r,
                       q_ref, k_ref, v_ref, rq_ref, rk_ref, rv_ref, g_ref):
    hb = _normed_input(x_ref, mod_ref, n1_ref)

    def proj(off, width):
        return _dot(hb, w_ref[:, off:off + width])

    def rope(p, cos_ref, sin_ref, nf):
        lane = lax.broadcasted_iota(jnp.int32, (p.shape[0], LANES), 1)
        first_half = (lane % (2 * nf)) < nf
        cos = cos_ref[...]
        sin = sin_ref[...]
        parts = [_rope_chunk(p[:, c:c + LANES], cos, sin, first_half, nf)
                 for c in range(0, p.shape[1], LANES)]
        return parts[0] if len(parts) == 1 else jnp.concatenate(parts, axis=1)

    nf_a = HEAD_DIM // 4
    nf_r = RET_DK // 4
    q_ref[...] = rope(proj(OFF_Q, ATTN_W), cos_a, sin_a, nf_a).astype(BF16)
    k_ref[...] = rope(proj(OFF_K, KV_W), cos_a, sin_a, nf_a).astype(BF16)
    v_ref[...] = proj(OFF_V, KV_W).astype(BF16)
    rq_ref[...] = rope(proj(OFF_RQ, RET_W), cos_r, sin_r, nf_r).astype(BF16)
    rk_ref[...] = rope(proj(OFF_RK, RET_W), cos_r, sin_r, nf_r) * (RET_DK ** -0.5)
    rv_ref[...] = proj(OFF_RV, RET_VW).astype(BF16)
    for c in range(0, GATE_W, D_MODEL):
        g_ref[:, c:c + D_MODEL] = proj(OFF_RGF + c, D_MODEL)


def _latent_in_projection(x, mod, n1, w_in, layer, seq_len, tables):
    n_tok = x.shape[0]
    tiles_per_batch = seq_len // ROW_TILE
    row = lambda w: pl.BlockSpec((ROW_TILE, w), lambda i: (i, 0))
    tab = pl.BlockSpec((ROW_TILE, LANES), lambda i: (i % tiles_per_batch, 0))
    widths = (ATTN_W, KV_W, KV_W, RET_W, RET_W, RET_VW, GATE_W)
    dtypes = (BF16, BF16, BF16, BF16, F32, BF16, F32)
    return pl.pallas_call(
        _lat_inproj_kernel,
        grid=(n_tok // ROW_TILE,),
        in_specs=[row(D_MODEL), _mod_spec(layer, tiles_per_batch), _layer_spec((1, D_MODEL), layer),
                  _const_spec((D_MODEL, IN_WIDTH)), tab, tab, tab, tab],
        out_specs=[row(w) for w in widths],
        out_shape=[jax.ShapeDtypeStruct((n_tok, w), dt) for w, dt in zip(widths, dtypes)],
        compiler_params=_params(1),
        name="in_projection_lat",
    )(x, mod, n1, w_in, *tables)


def _softmax_t(score_blocks, sink):
    m = sink
    for s in score_blocks:
        m = jnp.maximum(jnp.max(s, axis=0, keepdims=True), m)
    es = [jnp.exp(s - m) for s in score_blocks]
    denom = jnp.exp(sink - m)
    for e in es:
        denom = denom + jnp.sum(e, axis=0, keepdims=True)
    return [e.astype(BF16) for e in es], 1.0 / denom


def _store_heads(o_ref, rows, heads, o_t, inv):
    for pair in range(0, len(heads), 2):
        halves = [o_t[:, g * Q_BLOCK:(g + 1) * Q_BLOCK] * inv[g] for g in (pair, pair + 1)]
        both = jnp.transpose(jnp.concatenate(halves, axis=0))
        h0 = heads[pair]
        o_ref[rows, h0 * HEAD_DIM:(h0 + 2) * HEAD_DIM] = both.astype(BF16)


def _ctx_attention_scores(q_ref, k_ref, seq_len):
    scale = HEAD_DIM ** -0.5
    out = []
    for b in range(0, q_ref.shape[0], seq_len):
        for kh in range(KV_HEADS):
            for c in range(b, b + seq_len, Q_BLOCK):
                heads = [kh * GROUP + g for g in range(GROUP)]
                q4 = jnp.concatenate(
                    [q_ref[c:c + Q_BLOCK, h * HEAD_DIM:(h + 1) * HEAD_DIM] for h in heads], axis=0)
                keys = k_ref[b:b + seq_len, kh * HEAD_DIM:(kh + 1) * HEAD_DIM]
                out.append(((b, kh, c), _dot_nt(keys, q4) * scale))
    return out


def _ctx_attention_outputs(scores, b, v_ref, sink_ref, o_ref, layer, seq_len):
    v_t = jnp.transpose(v_ref[b:b + seq_len, :].astype(F32)).astype(BF16)
    for (sb, kh, c), s4 in scores:
        if sb != b:
            continue
        heads = [kh * GROUP + g for g in range(GROUP)]
        es, inv = [], []
        for g, h in enumerate(heads):
            (e,), r = _softmax_t([s4[:, g * Q_BLOCK:(g + 1) * Q_BLOCK]], sink_ref[layer, h])
            es.append(e)
            inv.append(r)
        o_t = _dot(v_t[kh * HEAD_DIM:(kh + 1) * HEAD_DIM, :], jnp.concatenate(es, axis=1))
        _store_heads(o_ref, slice(c, c + Q_BLOCK), heads, o_t, inv)


def _lat_attn_kernel(sink_ref, q_ref, k_ref, v_ref, ck_ref, cv_ref, o_ref,
                     vt_ref, ckb_ref, cvt_ref, *, layer):
    scale = HEAD_DIM ** -0.5
    n_blocks = k_ref.shape[0] // Q_BLOCK
    band_blocks = 1 + 2 * WINDOW // Q_BLOCK
    band = band_blocks * Q_BLOCK
    j = pl.program_id(1)

    @pl.when(j == 0)
    def _per_batch_element():
        for t in range(n_blocks):
            vt_ref[t] = jnp.transpose(
                v_ref[t * Q_BLOCK:(t + 1) * Q_BLOCK, :].astype(F32)).astype(BF16)
        cv = jnp.concatenate([cv_ref[0, 0, kh] for kh in range(KV_HEADS)], axis=1)
        cvt_ref[...] = jnp.transpose(cv).astype(BF16)
        for kh in range(KV_HEADS):
            ckb_ref[kh] = ck_ref[0, 0, kh].astype(BF16)

    first = jnp.clip(j - WINDOW // Q_BLOCK, 0, n_blocks - band_blocks)
    start = pl.multiple_of(first * Q_BLOCK, Q_BLOCK)
    kpos = start + lax.broadcasted_iota(jnp.int32, (band, Q_BLOCK), 0)
    qpos = j * Q_BLOCK + lax.broadcasted_iota(jnp.int32, (band, Q_BLOCK), 1)
    valid = jnp.abs(qpos - kpos) <= WINDOW
    kb = k_ref[pl.ds(start, band), :]
    scores = []
    for kh in range(KV_HEADS):
        hd = slice(kh * HEAD_DIM, (kh + 1) * HEAD_DIM)
        heads = [kh * GROUP + g for g in range(GROUP)]
        q4 = jnp.concatenate([q_ref[:, h * HEAD_DIM:(h + 1) * HEAD_DIM] for h in heads], axis=0)
        scores.append((_dot_nt(kb[:, hd], q4) * scale,
                       _dot_nt(ckb_ref[kh], q4) * scale))
    for kh, (s_loc4, s_ctx4) in enumerate(scores):
        hd = slice(kh * HEAD_DIM, (kh + 1) * HEAD_DIM)
        heads = [kh * GROUP + g for g in range(GROUP)]
        e_loc, e_ctx, inv = [], [], []
        for g, h in enumerate(heads):
            cols = slice(g * Q_BLOCK, (g + 1) * Q_BLOCK)
            (el, ec), r = _softmax_t([jnp.where(valid, s_loc4[:, cols], NEG_INF), s_ctx4[:, cols]],
                                     sink_ref[layer, h])
            e_loc.append(el)
            e_ctx.append(ec)
            inv.append(r)
        e_loc4 = jnp.concatenate(e_loc, axis=1)
        o_t = _dot(cvt_ref[hd, :], jnp.concatenate(e_ctx, axis=1))
        for t in range(band_blocks):
            o_t = o_t + _dot(vt_ref[first + t, hd, :], e_loc4[t * Q_BLOCK:(t + 1) * Q_BLOCK, :])
        _store_heads(o_ref, slice(None), heads, o_t, inv)


def _latent_attention(q, k, v, cache_k, cache_v, sink, layer, seq_len):
    n_tok = q.shape[0]
    n_batch = n_tok // seq_len
    n_blocks = seq_len // Q_BLOCK
    past_len = cache_k.shape[3]
    seq = lambda w: pl.BlockSpec((seq_len, w), lambda b, j: (b, 0))
    blk = pl.BlockSpec((Q_BLOCK, ATTN_W), lambda b, j: (b * n_blocks + j, 0))
    cache = pl.BlockSpec((1, 1, KV_HEADS, past_len, HEAD_DIM), lambda b, j: (b, layer, 0, 0, 0))
    return pl.pallas_call(
        functools.partial(_lat_attn_kernel, layer=layer),
        grid=(n_batch, n_blocks),
        in_specs=[pl.BlockSpec(memory_space=pltpu.SMEM), blk, seq(KV_W), seq(KV_W), cache, cache],
        out_specs=blk,
        out_shape=jax.ShapeDtypeStruct((n_tok, ATTN_W), BF16),
        scratch_shapes=[pltpu.VMEM((n_blocks, KV_W, Q_BLOCK), BF16),
                        pltpu.VMEM((KV_HEADS, past_len, HEAD_DIM), BF16),
                        pltpu.VMEM((KV_W, past_len), BF16)],
        compiler_params=_params(2),
        name="attention_lat",
    )(sink, q, k, v, cache_k, cache_v)


def _retention_products(rq_ref, rk_ref, rv_ref, intra_ref, kdec_ref, n_heads, final_state):
    n_blocks = rq_ref.shape[0] // RET_BLOCK
    blocks = [slice(i * RET_BLOCK, (i + 1) * RET_BLOCK) for i in range(n_blocks)]
    head_cols = [slice(h * RET_DK, (h + 1) * RET_DK) for h in range(n_heads)]
    qk = {(h, i): _dot_nt(rq_ref[blocks[i], head_cols[h]], rk_ref[blocks[i], head_cols[h]].astype(BF16))
          for h in range(n_heads) for i in range(n_blocks)}
    o_intra, kv = {}, {}
    for h in range(n_heads):
        for d in range(2):
            for i in range(n_blocks):
                v_i = rv_ref[blocks[i], head_cols[h]]
                o_intra[h, d, i] = _dot((qk[h, i] * intra_ref[d, h]).astype(BF16), v_i)
                last = i == (0 if d else n_blocks - 1)
                if final_state or not last:
                    k_i = rk_ref[blocks[i], head_cols[h]]
                    kv[h, d, i] = _dot_tn((k_i * kdec_ref[d, h]).astype(BF16), v_i)
    return o_intra, kv


def _retention_outputs(products, rq_ref, gf_ref, gb_ref, cross_ref, gblk_ref, gn_ref, ret_ref, acc_ref,
                       n_heads, initial_state, store_state):
    o_intra, kv = products
    n_blocks = rq_ref.shape[0] // RET_BLOCK
    blocks = [slice(i * RET_BLOCK, (i + 1) * RET_BLOCK) for i in range(n_blocks)]
    for h in range(n_heads):
        hs = slice(h * RET_DK, (h + 1) * RET_DK)
        for d, backward in enumerate((False, True)):
            g_ref = gb_ref if backward else gf_ref
            cross = cross_ref[d, h]
            g_block = gblk_ref[d, h]
            state = initial_state(backward, h)
            order = list(range(n_blocks))
            if backward:
                order.reverse()
            for i in order:
                o = o_intra[h, d, i]
                if state is not None:
                    o = o + _dot(rq_ref[blocks[i], hs], state.astype(BF16)) * cross
                if (h, d, i) in kv:
                    state = kv[h, d, i] if state is None else state * g_block + kv[h, d, i]
                mu = jnp.mean(o, axis=-1, keepdims=True)
                dev = o - mu
                var = jnp.mean(dev * dev, axis=-1, keepdims=True)
                y = dev * lax.rsqrt(var + NORM_EPS) * gn_ref[:, hs]
                gate = g_ref[blocks[i], hs]
                contrib = gate * jax.nn.sigmoid(gate) * y
                if backward:
                    ret_ref[blocks[i], hs] = (acc_ref[blocks[i], hs] + contrib).astype(BF16)
                else:
                    acc_ref[blocks[i], hs] = contrib
            if store_state is not None:
                store_state(backward, h, state)


def _lat_retention_kernel(rq_ref, rk_ref, rv_ref, gf_ref, gb_ref, intra_ref, cross_ref, kdec_ref,
                          gblk_ref, gn_ref, s0f_ref, s0b_ref, ret_ref, acc_ref):
    products = _retention_products(rq_ref, rk_ref, rv_ref, intra_ref, kdec_ref, 1, False)
    _retention_outputs(products, rq_ref, gf_ref, gb_ref, cross_ref, gblk_ref, gn_ref, ret_ref, acc_ref,
                       1, lambda backward, h: (s0b_ref if backward else s0f_ref)[0, 0, h], None)


def _latent_retention(rq, rk, rv, gates, decay, gn, layer, seq_len, states0):
    n_tok = rq.shape[0]
    gate_blocks = RET_VW // RET_DK
    seq = lambda off=0: pl.BlockSpec((seq_len, RET_DK), lambda b, h: (b, off + h))
    state = pl.BlockSpec((1, 1, 1, RET_DK, RET_DV), lambda b, h: (b, layer, h, 0, 0))
    return pl.pallas_call(
        _lat_retention_kernel,
        grid=(n_tok // seq_len, RET_HEADS),
        in_specs=[seq(), seq(), seq(), seq(), seq(gate_blocks),
                  *_decay_specs(layer, 1, lambda b, h: h),
                  pl.BlockSpec((None, 1, RET_DK), lambda b, h: (layer, 0, h)), state, state],
        out_specs=seq(),
        out_shape=jax.ShapeDtypeStruct((n_tok, RET_VW), BF16),
        scratch_shapes=[pltpu.VMEM((seq_len, RET_DK), F32)],
        compiler_params=_params(2),
        name="retention_lat",
    )(rq, rk, rv, gates, gates, *decay, gn, *states0)


N_CTX_MIX_OUT = 7
N_CTX_MIX_SCRATCH = 9


def _ctx_mix_kernel(*refs, layer, seq_len):
    n_in = len(refs) - N_CTX_MIX_OUT - N_CTX_MIX_SCRATCH
    (sink_ref, x_ref, mod_ref, n1_ref, w_ref, intra_ref, cross_ref, kdec_ref, gblk_ref,
     gn_ref) = refs[:10]
    attn_ref, ret_ref, g_ref, kc_ref, vc_ref, sf_ref, sb_ref = refs[n_in:n_in + N_CTX_MIX_OUT]
    q_s, k_s, v_s, rq_s, rk_s, rv_s, gf_s, gb_s, acc_s = refs[n_in + N_CTX_MIX_OUT:]
    hb = _normed_input(x_ref, mod_ref, n1_ref)

    def proj(off, width):
        return _dot(hb, w_ref[:, off:off + width])

    batch_rows = [slice(b, b + seq_len) for b in range(0, x_ref.shape[0], seq_len)]
    q_s[...] = proj(OFF_Q, ATTN_W).astype(BF16)
    k = proj(OFF_K, KV_W)
    v = proj(OFF_V, KV_W)
    k_s[...] = k.astype(BF16)
    v_s[...] = v.astype(BF16)
    slot = _zero_other_layers((kc_ref, vc_ref), layer)
    for b, rows in enumerate(batch_rows):
        for kh in range(KV_HEADS):
            kc_ref[b, slot, kh] = k[rows, kh * HEAD_DIM:(kh + 1) * HEAD_DIM]
            vc_ref[b, slot, kh] = v[rows, kh * HEAD_DIM:(kh + 1) * HEAD_DIM]
    scores = _ctx_attention_scores(q_s, k_s, seq_len)
    rq_s[...] = proj(OFF_RQ, RET_W).astype(BF16)
    rk_s[...] = proj(OFF_RK, RET_W) * (RET_DK ** -0.5)
    rv_s[...] = proj(OFF_RV, RET_VW).astype(BF16)
    products = [_retention_products(rq_s.at[rows], rk_s.at[rows], rv_s.at[rows], intra_ref, kdec_ref,
                                    RET_HEADS, True) for rows in batch_rows]
    gf_s[...] = proj(OFF_RGF, RET_VW)
    gb_s[...] = proj(OFF_RGB, RET_VW)
    state_slot = _zero_other_layers((sf_ref, sb_ref), layer)

    def attention_outputs(b):
        _ctx_attention_outputs(scores, batch_rows[b].start, v_s, sink_ref, attn_ref, layer, seq_len)

    def retention_outputs(b):
        rows = batch_rows[b]

        def store_state(backward, h, state):
            (sb_ref if backward else sf_ref)[b, state_slot, h] = state

        _retention_outputs(products[b], rq_s.at[rows], gf_s.at[rows], gb_s.at[rows], cross_ref, gblk_ref,
                           gn_ref, ret_ref.at[rows], acc_s.at[rows], RET_HEADS,
                           lambda backward, h: None, store_state)

    stages = ([functools.partial(attention_outputs, b) for b in range(len(batch_rows))]
              + [functools.partial(retention_outputs, b) for b in range(len(batch_rows))])
    chunk = 2 * D_MODEL // len(stages)
    for n, stage in enumerate(stages):
        g_ref[:, n * chunk:(n + 1) * chunk] = proj(OFF_GA + n * chunk, chunk)
        stage()


def _context_mixer(x, mod, n1, w_in, sink, decay, gn, layer, seq_len, prev_outputs):
    n_tok = x.shape[0]
    n_batch = n_tok // seq_len
    per_tile = ROW_TILE // seq_len
    row = lambda w: pl.BlockSpec((ROW_TILE, w), lambda i: (i, 0))
    in_specs = [pl.BlockSpec(memory_space=pltpu.SMEM), row(D_MODEL), _mod_spec(layer, None),
                _layer_spec((1, D_MODEL), layer), _const_spec((D_MODEL, IN_WIDTH)),
                *_decay_specs(layer, RET_HEADS, lambda i: 0), _layer_spec((1, RET_VW), layer)]
    args = [sink, x, mod, n1, w_in, *decay, gn]
    out_specs = [row(ATTN_W), row(RET_VW), row(2 * D_MODEL)]
    out_shape = [jax.ShapeDtypeStruct((n_tok, ATTN_W), BF16), jax.ShapeDtypeStruct((n_tok, RET_VW), BF16),
                 jax.ShapeDtypeStruct((n_tok, 2 * D_MODEL), F32)]
    cache_dims = (KV_HEADS, seq_len, HEAD_DIM)
    state_dims = (RET_HEADS, RET_DK, RET_DV)
    aliases = {}
    for idx, dims in enumerate((cache_dims, cache_dims, state_dims, state_dims)):
        if prev_outputs is None:
            assert layer == 0
            out_specs.append(pl.BlockSpec((per_tile, DEPTH) + dims, lambda i: (i, 0, 0, 0, 0)))
        else:
            aliases[len(args)] = len(out_shape)
            in_specs.append(pl.BlockSpec(memory_space=pl.ANY))
            args.append(prev_outputs[idx])
            out_specs.append(pl.BlockSpec((per_tile, 1) + dims, lambda i: (i, layer, 0, 0, 0)))
        out_shape.append(jax.ShapeDtypeStruct((n_batch, DEPTH) + dims, F32))
    assert len(out_shape) == N_CTX_MIX_OUT
    scratch = [pltpu.VMEM((ROW_TILE, ATTN_W), BF16), pltpu.VMEM((ROW_TILE, KV_W), BF16),
               pltpu.VMEM((ROW_TILE, KV_W), BF16), pltpu.VMEM((ROW_TILE, RET_W), BF16),
               pltpu.VMEM((ROW_TILE, RET_W), F32), pltpu.VMEM((ROW_TILE, RET_VW), BF16),
               pltpu.VMEM((ROW_TILE, RET_VW), F32), pltpu.VMEM((ROW_TILE, RET_VW), F32),
               pltpu.VMEM((ROW_TILE, RET_VW), F32)]
    assert len(scratch) == N_CTX_MIX_SCRATCH
    return pl.pallas_call(
        functools.partial(_ctx_mix_kernel, layer=layer, seq_len=seq_len),
        grid=(n_tok // ROW_TILE,),
        in_specs=in_specs,
        out_specs=out_specs,
        out_shape=out_shape,
        scratch_shapes=scratch,
        input_output_aliases=aliases,
        compiler_params=_params(1),
        name="context_mixer",
    )(*args)


def _post_kernel(*refs, final, n_cast):
    (attn_ref, ret_ref, ga_ref, gr_ref, x_ref, mod_ref, n2_ref,
     wba_ref, wbr_ref, wo_ref, w1_ref, w2_ref) = refs[:12]
    n_in = 12 + (1 if final else 0) + n_cast
    fn_ref = refs[12] if final else None
    out_ref = refs[n_in]
    for src, dst in zip(refs[n_in - n_cast:n_in], refs[n_in + 1:]):
        dst[...] = src[...].astype(BF16)
    mod = mod_ref[...]
    part = lambda k: mod[:, k * D_MODEL:(k + 1) * D_MODEL]
    merged = (jax.nn.sigmoid(ga_ref[...]) * _dot(attn_ref[...], wba_ref[...])
              + jax.nn.sigmoid(gr_ref[...]) * _dot(ret_ref[...], wbr_ref[...]))
    x = x_ref[...] + part(2) * _dot(merged.astype(BF16), wo_ref[...])
    h2 = _modulated_norm(x, n2_ref[...], part(3), part(4)).astype(BF16)
    ff = None
    for c in range(0, D_FF, FF_CHUNK):
        a = jnp.maximum(_dot(h2, w1_ref[:, c:c + FF_CHUNK]), 0.0)
        t = _dot((a * a).astype(BF16), w2_ref[c:c + FF_CHUNK, :])
        ff = t if ff is None else ff + t
    x = x + part(5) * ff
    if final:
        x = x * lax.rsqrt(jnp.mean(x * x, axis=-1, keepdims=True) + NORM_EPS) * fn_ref[...]
    out_ref[...] = x


def _post(attn, ret, gates, merge_gate_block, x, mod, n2, weights, layer, seq_len, latent, final_gain,
          cast_next=()):
    n_tok = x.shape[0]
    n_steps = n_tok // ROW_TILE
    tiles_per_batch = seq_len // ROW_TILE
    final = final_gain is not None
    row = lambda w, c=0: pl.BlockSpec((ROW_TILE, w), lambda i, c=c: (i, c))
    in_specs = [
        row(ATTN_W), row(RET_VW), row(D_MODEL, merge_gate_block), row(D_MODEL, merge_gate_block + 1),
        row(D_MODEL),
        _mod_spec(layer, tiles_per_batch if latent else None),
        _layer_spec((1, D_MODEL), layer),
        *[_const_spec(w.shape) for w in weights],
    ]
    args = [attn, ret, gates, gates, x, mod, n2, *weights]
    if final:
        in_specs.append(_const_spec((1, D_MODEL)))
        args.append(final_gain)
    out_specs = [row(D_MODEL)]
    out_shape = [jax.ShapeDtypeStruct((n_tok, D_MODEL), F32)]
    for w in cast_next:
        _, k, n = w.shape
        rows = k // n_steps
        assert rows * n_steps == k and rows % 16 == 0
        in_specs.append(pl.BlockSpec((None, rows, n), lambda i: (layer + 1, i, 0)))
        args.append(w)
        out_specs.append(pl.BlockSpec((rows, n), lambda i: (i, 0)))
        out_shape.append(jax.ShapeDtypeStruct((k, n), BF16))
    return pl.pallas_call(
        functools.partial(_post_kernel, final=final, n_cast=len(cast_next)),
        grid=(n_steps,),
        in_specs=in_specs,
        out_specs=out_specs,
        out_shape=out_shape,
        compiler_params=_params(1),
        name="post_final" if final else "post",
    )(*args)


def _rope_tables(seq_len):
    t = np.arange(seq_len)
    rows = (t // GRID_W).astype(np.float64)
    cols = (t % GRID_W).astype(np.float64)

    def table(head_dim):
        half = head_dim // 2
        nf = half // 2
        inv = ROPE_BASE ** (-np.arange(nf, dtype=np.float64) / nf)
        lane = np.arange(LANES) % head_dim
        pos = np.where((lane < half)[None, :], rows[:, None], cols[:, None])
        ang = pos * inv[lane % nf][None, :]
        sign = np.where((lane % half) < nf, -1.0, 1.0)[None, :]
        return jnp.asarray(np.cos(ang), F32), jnp.asarray(np.sin(ang) * sign, F32)

    cos_a, sin_a = table(HEAD_DIM)
    cos_r, sin_r = table(RET_DK)
    return cos_a, sin_a, cos_r, sin_r


def kernel(x_prompt, x_sample, cache_k, cache_v, state_ret_fwd, state_ret_bwd, c, c_ctx, norm1_g, norm2_g, final_norm_g, w_mod, b_mod, w_in, attn_sink, ret_decay_fwd, ret_decay_bwd, ret_gn_g, w_branch_attn, w_branch_ret, w_out, w_ff1, w_ff2):
    n_ctx, ctx_len, _ = x_prompt.shape
    n_lat, lat_len, _ = x_sample.shape
    xp = x_prompt.reshape(n_ctx * ctx_len, D_MODEL)
    xs = x_sample.reshape(n_lat * lat_len, D_MODEL)

    cond = jnp.concatenate([c_ctx[None, :], c], axis=0)
    cond8 = jnp.pad(cond, ((0, -cond.shape[0] % 8), (0, 0)))
    mod = _modulation(cond8, w_mod, b_mod).reshape(DEPTH, cond8.shape[0], 1, MOD_W)
    dec_rows = jnp.broadcast_to(jnp.stack([ret_decay_fwd, ret_decay_bwd], axis=1)[..., None],
                                (DEPTH, 2, RET_HEADS, LANES))
    decay = _decay_tables(dec_rows)
    tables = _rope_tables(lat_len)
    final_gain = final_norm_g.reshape(1, D_MODEL)

    n1 = norm1_g.reshape(DEPTH, 1, D_MODEL)
    n2 = norm2_g.reshape(DEPTH, 1, D_MODEL)
    gn = ret_gn_g.reshape(DEPTH, 1, RET_VW)
    f32_weights = (w_in, w_branch_attn, w_branch_ret, w_out, w_ff1, w_ff2)
    w_in_b, *post_w = [w[0].astype(BF16) for w in f32_weights]

    ctx_outputs = None
    for l in range(DEPTH):
        last = l == DEPTH - 1
        attn, ret, gates, *ctx_outputs = _context_mixer(xp, mod, n1, w_in_b, attn_sink, decay, gn, l,
                                                        ctx_len, ctx_outputs)
        xp, *next_weights = _post(attn, ret, gates, 0, xp, mod, n2, post_w, l, ctx_len, False,
                                  final_gain if last else None, () if last else f32_weights)
        q, k, v, rq, rk, rv, gates = _latent_in_projection(xs, mod, n1, w_in_b, l, lat_len, tables)
        attn = _latent_attention(q, k, v, cache_k, cache_v, attn_sink, l, lat_len)
        ret = _latent_retention(rq, rk, rv, gates, decay, gn, l, lat_len,
                                (state_ret_fwd, state_ret_bwd))
        (xs,) = _post(attn, ret, gates, (OFF_GA - OFF_RGF) // D_MODEL, xs, mod, n2, post_w, l, lat_len,
                      True, final_gain if last else None)
        if not last:
            w_in_b, *post_w = next_weights

    return (xp.reshape(n_ctx, ctx_len, D_MODEL), xs.reshape(n_lat, lat_len, D_MODEL), *ctx_outputs)
```

```python
import functools

import jax
import jax.numpy as jnp
import numpy as np
from jax import lax
from jax.experimental import pallas as pl
from jax.experimental.pallas import tpu as pltpu

D_MODEL = 1024
DEPTH = 4
GRID_W = 64
N_HEADS = 8
HEAD_DIM = 64
KV_HEADS = 2
GROUP = N_HEADS // KV_HEADS
WINDOW = 128
Q_BLOCK = 128
RET_HEADS = 4
RET_DK = 128
RET_DV = 128
ATTN_W = N_HEADS * HEAD_DIM
KV_W = KV_HEADS * HEAD_DIM
RET_W = RET_HEADS * RET_DK
RET_VW = RET_HEADS * RET_DV
GATE_W = 2 * RET_VW + 2 * D_MODEL
IN_WIDTH = ATTN_W + 2 * KV_W + 2 * RET_W + RET_VW + GATE_W
D_FF = 4 * D_MODEL
MOD_W = 6 * D_MODEL
ROPE_BASE = 10000.0
NORM_EPS = 1e-6
NEG_INF = -1e30

OFF_Q = 0
OFF_K = OFF_Q + ATTN_W
OFF_V = OFF_K + KV_W
OFF_RQ = OFF_V + KV_W
OFF_RK = OFF_RQ + RET_W
OFF_RV = OFF_RK + RET_W
OFF_RGF = OFF_RV + RET_VW
OFF_RGB = OFF_RGF + RET_VW
OFF_GA = OFF_RGB + RET_VW

LANES = 128
ROW_TILE = 512
RET_BLOCK = 256
LAT_BLOCKS_PER_STEP = 2
MOD_TILE_N = 1536
FF_CHUNK = 1024
VMEM_LIMIT = 58 * 1024 * 1024

BF16 = jnp.bfloat16
F32 = jnp.float32


def _params(n_axes):
    return pltpu.CompilerParams(dimension_semantics=("arbitrary",) * n_axes,
                                vmem_limit_bytes=VMEM_LIMIT)


def _const_spec(shape):
    nd = len(shape)
    return pl.BlockSpec(shape, lambda *_: (0,) * nd, pipeline_mode=pl.Buffered(1))


def _layer_spec(shape, layer):
    nd = len(shape)
    return pl.BlockSpec((None,) + tuple(shape), lambda *_: (layer,) + (0,) * nd,
                        pipeline_mode=pl.Buffered(1))


def _mod_spec(layer, tiles_per_batch):
    if tiles_per_batch is None:
        return pl.BlockSpec((None, None, 1, MOD_W), lambda *_: (layer, 0, 0, 0))
    return pl.BlockSpec((None, None, 1, MOD_W), lambda i: (layer, 1 + i // tiles_per_batch, 0, 0))


def _zero_other_layers(out_refs, layer):
    n_slots = out_refs[0].shape[1]
    if n_slots == 1:
        return 0
    for ref in out_refs:
        for other in range(n_slots):
            if other != layer:
                ref[:, other] = jnp.zeros((ref.shape[0],) + ref.shape[2:], ref.dtype)
    return layer


def _dot(a, b):
    return jnp.dot(a, b, preferred_element_type=F32)


def _dot_nt(a, b):
    return lax.dot_general(a, b, (((1,), (1,)), ((), ())), preferred_element_type=F32)


def _dot_tn(a, b):
    return lax.dot_general(a, b, (((0,), (0,)), ((), ())), preferred_element_type=F32)


def _mod_kernel(cond_ref, w_ref, b_ref, out_ref):
    cond = cond_ref[...]
    s = (cond * jax.nn.sigmoid(cond)).astype(BF16)
    out_ref[0] = _dot(s, w_ref[0].astype(BF16)) + b_ref[0]


def _modulation(cond8, w_mod, b_mod):
    n_rows = cond8.shape[0]
    return pl.pallas_call(
        _mod_kernel,
        grid=(DEPTH, MOD_W // MOD_TILE_N),
        in_specs=[
            pl.BlockSpec((n_rows, D_MODEL), lambda l, j: (0, 0)),
            pl.BlockSpec((1, D_MODEL, MOD_TILE_N), lambda l, j: (l, 0, j)),
            pl.BlockSpec((1, 1, MOD_TILE_N), lambda l, j: (l, 0, j)),
        ],
        out_specs=pl.BlockSpec((1, n_rows, MOD_TILE_N), lambda l, j: (l, 0, j)),
        out_shape=jax.ShapeDtypeStruct((DEPTH, n_rows, MOD_W), F32),
        compiler_params=_params(2),
        name="adaln_modulation",
    )(cond8, w_mod, b_mod.reshape(DEPTH, 1, MOD_W))


def _decay_kernel(dec_ref, intra_ref, cross_ref, kdec_ref, gblk_ref):
    backward = pl.program_id(1) == 1
    log_g = jnp.log(jax.nn.sigmoid(dec_ref[...]))
    n = RET_BLOCK
    row = lax.broadcasted_iota(jnp.int32, (n, n), 0).astype(F32)
    col = lax.broadcasted_iota(jnp.int32, (n, n), 1).astype(F32)
    diff = jnp.where(backward, col - row, row - col)
    row1 = row[:, :LANES]
    cross_pow = jnp.where(backward, n - row1, row1 + 1.0)
    kdec_pow = jnp.where(backward, row1, n - 1.0 - row1)
    for h in range(RET_HEADS):
        lg = log_g[h:h + 1, :]
        lg_wide = jnp.concatenate([lg] * (n // LANES), axis=1)
        intra_ref[h] = jnp.where(diff >= 0, jnp.exp(jnp.maximum(diff, 0.0) * lg_wide), 0.0)
        cross_ref[h] = jnp.exp(cross_pow * lg)
        kdec_ref[h] = jnp.exp(kdec_pow * lg)
        gblk_ref[h] = jnp.exp(n * lg)


def _decay_tables(dec_rows):
    lead = lambda *tail: pl.BlockSpec((None, None, RET_HEADS) + tail,
                                      lambda l, d: (l, d, 0) + (0,) * len(tail))
    shape = lambda *tail: jax.ShapeDtypeStruct((DEPTH, 2, RET_HEADS) + tail, F32)
    return pl.pallas_call(
        _decay_kernel,
        grid=(DEPTH, 2),
        in_specs=[lead(LANES)],
        out_specs=[lead(RET_BLOCK, RET_BLOCK), lead(RET_BLOCK, LANES), lead(RET_BLOCK, LANES),
                   lead(1, LANES)],
        out_shape=[shape(RET_BLOCK, RET_BLOCK), shape(RET_BLOCK, LANES), shape(RET_BLOCK, LANES),
                   shape(1, LANES)],
        compiler_params=_params(2),
        name="retention_decay_tables",
    )(dec_rows)


def _decay_specs(layer, n_heads, index_map):
    mode = {"pipeline_mode": pl.Buffered(1)} if n_heads == RET_HEADS else {}
    tab = lambda *tail: pl.BlockSpec((None, 2, n_heads) + tail,
                                     lambda *idx: (layer, 0, index_map(*idx)) + (0,) * len(tail), **mode)
    return [tab(RET_BLOCK, RET_BLOCK), tab(RET_BLOCK, LANES), tab(RET_BLOCK, LANES), tab(1, LANES)]


def _modulated_norm(x, gain, shift, scale):
    y = x * lax.rsqrt(jnp.mean(x * x, axis=-1, keepdims=True) + NORM_EPS) * gain
    return y * (1.0 + scale) + shift


def _normed_input(x_ref, mod_ref, n1_ref):
    mod = mod_ref[...]
    h = _modulated_norm(x_ref[...], n1_ref[...], mod[:, 0:D_MODEL], mod[:, D_MODEL:2 * D_MODEL])
    return h.astype(BF16)


def _rope_chunk(xc, cos, sin_signed, first_half, nf):
    partner = jnp.where(first_half, pltpu.roll(xc, LANES - nf, 1), pltpu.roll(xc, nf, 1))
    return xc * cos + partner * sin_signed


def _rope(p, cos, sin_signed, nf):
    lane = lax.broadcasted_iota(jnp.int32, (p.shape[0], LANES), 1)
    first_half = (lane % (2 * nf)) < nf
    parts = [_rope_chunk(p[:, c:c + LANES], cos, sin_signed, first_half, nf)
             for c in range(0, p.shape[1], LANES)]
    return parts[0] if len(parts) == 1 else jnp.concatenate(parts, axis=1)


def _softmax_t(score_blocks, sink):
    m = sink
    for s in score_blocks:
        m = jnp.maximum(jnp.max(s, axis=0, keepdims=True), m)
    es = [jnp.exp(s - m) for s in score_blocks]
    denom = jnp.exp(sink - m)
    for e in es:
        denom = denom + jnp.sum(e, axis=0, keepdims=True)
    return [e.astype(BF16) for e in es], 1.0 / denom


def _store_heads(o_ref, rows, heads, o_t, inv):
    for pair in range(0, len(heads), 2):
        halves = [o_t[:, g * Q_BLOCK:(g + 1) * Q_BLOCK] * inv[g] for g in (pair, pair + 1)]
        both = jnp.transpose(jnp.concatenate(halves, axis=0))
        h0 = heads[pair]
        o_ref[rows, h0 * HEAD_DIM:(h0 + 2) * HEAD_DIM] = both.astype(BF16)


def _ctx_attention_scores(q_ref, k_ref, seq_len):
    scale = HEAD_DIM ** -0.5
    out = []
    for b in range(0, q_ref.shape[0], seq_len):
        for kh in range(KV_HEADS):
            for c in range(b, b + seq_len, Q_BLOCK):
                heads = [kh * GROUP + g for g in range(GROUP)]
                q4 = jnp.concatenate(
                    [q_ref[c:c + Q_BLOCK, h * HEAD_DIM:(h + 1) * HEAD_DIM] for h in heads], axis=0)
                keys = k_ref[b:b + seq_len, kh * HEAD_DIM:(kh + 1) * HEAD_DIM]
                out.append(((b, kh, c), _dot_nt(keys, q4) * scale))
    return out


def _ctx_attention_outputs(scores, b, v_ref, sink_ref, o_ref, layer, seq_len):
    v_t = jnp.transpose(v_ref[b:b + seq_len, :].astype(F32)).astype(BF16)
    for (sb, kh, c), s4 in scores:
        if sb != b:
            continue
        heads = [kh * GROUP + g for g in range(GROUP)]
        es, inv = [], []
        for g, h in enumerate(heads):
            (e,), r = _softmax_t([s4[:, g * Q_BLOCK:(g + 1) * Q_BLOCK]], sink_ref[layer, h])
            es.append(e)
            inv.append(r)
        o_t = _dot(v_t[kh * HEAD_DIM:(kh + 1) * HEAD_DIM, :], jnp.concatenate(es, axis=1))
        _store_heads(o_ref, slice(c, c + Q_BLOCK), heads, o_t, inv)


BAND_BLOCKS = 1 + 2 * WINDOW // Q_BLOCK


def _lat_attention_scores(j, q_ref, k_ref, ckb_ref):
    scale = HEAD_DIM ** -0.5
    n_blocks = k_ref.shape[0] // Q_BLOCK
    band = BAND_BLOCKS * Q_BLOCK
    first = jnp.clip(j - WINDOW // Q_BLOCK, 0, n_blocks - BAND_BLOCKS)
    start = pl.multiple_of(first * Q_BLOCK, Q_BLOCK)
    kpos = start + lax.broadcasted_iota(jnp.int32, (band, Q_BLOCK), 0)
    qpos = j * Q_BLOCK + lax.broadcasted_iota(jnp.int32, (band, Q_BLOCK), 1)
    valid = jnp.abs(qpos - kpos) <= WINDOW
    kb = k_ref[pl.ds(start, band), :]
    rows = pl.ds(pl.multiple_of(j * Q_BLOCK, Q_BLOCK), Q_BLOCK)
    scores = []
    for kh in range(KV_HEADS):
        hd = slice(kh * HEAD_DIM, (kh + 1) * HEAD_DIM)
        heads = [kh * GROUP + g for g in range(GROUP)]
        q4 = jnp.concatenate([q_ref[rows, h * HEAD_DIM:(h + 1) * HEAD_DIM] for h in heads], axis=0)
        scores.append((_dot_nt(kb[:, hd], q4) * scale, _dot_nt(ckb_ref[kh], q4) * scale))
    return first, valid, scores


def _lat_attention_outputs(j, first, valid, scores, vt_ref, cvt_ref, sink_ref, o_ref, layer):
    rows = pl.ds(pl.multiple_of(j * Q_BLOCK, Q_BLOCK), Q_BLOCK)
    for kh, (s_loc4, s_ctx4) in enumerate(scores):
        hd = slice(kh * HEAD_DIM, (kh + 1) * HEAD_DIM)
        heads = [kh * GROUP + g for g in range(GROUP)]
        e_loc, e_ctx, inv = [], [], []
        for g, h in enumerate(heads):
            cols = slice(g * Q_BLOCK, (g + 1) * Q_BLOCK)
            (el, ec), r = _softmax_t([jnp.where(valid, s_loc4[:, cols], NEG_INF), s_ctx4[:, cols]],
                                     sink_ref[layer, h])
            e_loc.append(el)
            e_ctx.append(ec)
            inv.append(r)
        e_loc4 = jnp.concatenate(e_loc, axis=1)
        o_t = _dot(cvt_ref[hd, :], jnp.concatenate(e_ctx, axis=1))
        for t in range(BAND_BLOCKS):
            o_t = o_t + _dot(vt_ref[first + t, hd, :], e_loc4[t * Q_BLOCK:(t + 1) * Q_BLOCK, :])
        _store_heads(o_ref, rows, heads, o_t, inv)


def _retention_products(rq_ref, rk_ref, rv_ref, intra_ref, kdec_ref, n_heads, final_state):
    n_blocks = rq_ref.shape[0] // RET_BLOCK
    blocks = [slice(i * RET_BLOCK, (i + 1) * RET_BLOCK) for i in range(n_blocks)]
    head_cols = [slice(h * RET_DK, (h + 1) * RET_DK) for h in range(n_heads)]
    qk = {(h, i): _dot_nt(rq_ref[blocks[i], head_cols[h]], rk_ref[blocks[i], head_cols[h]].astype(BF16))
          for h in range(n_heads) for i in range(n_blocks)}
    o_intra, kv = {}, {}
    for h in range(n_heads):
        for d in range(2):
            for i in range(n_blocks):
                v_i = rv_ref[blocks[i], head_cols[h]]
                o_intra[h, d, i] = _dot((qk[h, i] * intra_ref[d, h]).astype(BF16), v_i)
                last = i == (0 if d else n_blocks - 1)
                if final_state or not last:
                    k_i = rk_ref[blocks[i], head_cols[h]]
                    kv[h, d, i] = _dot_tn((k_i * kdec_ref[d, h]).astype(BF16), v_i)
    return o_intra, kv


def _retention_outputs(products, rq_ref, gf_ref, gb_ref, cross_ref, gblk_ref, gn_ref, ret_ref, acc_ref,
                       n_heads, initial_state, store_state):
    o_intra, kv = products
    n_blocks = rq_ref.shape[0] // RET_BLOCK
    blocks = [slice(i * RET_BLOCK, (i + 1) * RET_BLOCK) for i in range(n_blocks)]
    for h in range(n_heads):
        hs = slice(h * RET_DK, (h + 1) * RET_DK)
        for d, backward in enumerate((False, True)):
            g_ref = gb_ref if backward else gf_ref
            cross = cross_ref[d, h]
            g_block = gblk_ref[d, h]
            state = initial_state(backward, h)
            order = list(range(n_blocks))
            if backward:
                order.reverse()
            for i in order:
                o = o_intra[h, d, i]
                if state is not None:
                    o = o + _dot(rq_ref[blocks[i], hs], state.astype(BF16)) * cross
                if (h, d, i) in kv:
                    state = kv[h, d, i] if state is None else state * g_block + kv[h, d, i]
                mu = jnp.mean(o, axis=-1, keepdims=True)
                dev = o - mu
                var = jnp.mean(dev * dev, axis=-1, keepdims=True)
                y = dev * lax.rsqrt(var + NORM_EPS) * gn_ref[:, hs]
                gate = g_ref[blocks[i], hs]
                contrib = gate * jax.nn.sigmoid(gate) * y
                if backward:
                    ret_ref[blocks[i], hs] = (acc_ref[blocks[i], hs] + contrib).astype(BF16)
                else:
                    acc_ref[blocks[i], hs] = contrib
            if store_state is not None:
                store_state(backward, h, state)


def _lat_mix_kernel(sink_ref, x_ref, mod_ref, n1_ref, w_ref, cos_a, sin_a, cos_r, sin_r, ck_ref, cv_ref,
                    intra_ref, cross_ref, kdec_ref, gblk_ref, gn_ref, s0f_ref, s0b_ref,
                    attn_ref, ret_ref,
                    hb_s, q_s, k_s, vt_s, ckb_s, cvt_s, rq_s, rk_s, rv_s, gf_s, gb_s, acc_s, *, layer):
    seq_len = x_ref.shape[0]
    nf_a = HEAD_DIM // 4
    nf_r = RET_DK // 4

    hb_s[...] = _normed_input(x_ref, mod_ref, n1_ref)
    for r0 in range(0, seq_len, ROW_TILE):
        rows = slice(r0, r0 + ROW_TILE)
        hb = hb_s[rows, :]
        q_s[rows, :] = _rope(_dot(hb, w_ref[:, OFF_Q:OFF_Q + ATTN_W]), cos_a[rows, :], sin_a[rows, :],
                             nf_a).astype(BF16)
        k_s[rows, :] = _rope(_dot(hb, w_ref[:, OFF_K:OFF_K + KV_W]), cos_a[rows, :], sin_a[rows, :],
                             nf_a).astype(BF16)
        v = _dot(hb, w_ref[:, OFF_V:OFF_V + KV_W])
        for t in range(ROW_TILE // Q_BLOCK):
            vt_s[r0 // Q_BLOCK + t] = jnp.transpose(v[t * Q_BLOCK:(t + 1) * Q_BLOCK, :]).astype(BF16)
    cv = jnp.concatenate([cv_ref[0, 0, kh] for kh in range(KV_HEADS)], axis=1)
    cvt_s[...] = jnp.transpose(cv).astype(BF16)
    for kh in range(KV_HEADS):
        ckb_s[kh] = ck_ref[0, 0, kh].astype(BF16)

    def query_blocks(i, carry):
        js = [i * LAT_BLOCKS_PER_STEP + n for n in range(LAT_BLOCKS_PER_STEP)]
        attention = [_lat_attention_scores(j, q_s, k_s, ckb_s) for j in js]
        n_rows = LAT_BLOCKS_PER_STEP * Q_BLOCK
        rows = pl.ds(pl.multiple_of(i * n_rows, n_rows), n_rows)
        hb = hb_s[rows, :]
        cos = cos_r[rows, :]
        sin = sin_r[rows, :]
        rq_s[rows, :] = _rope(_dot(hb, w_ref[:, OFF_RQ:OFF_RQ + RET_W]), cos, sin, nf_r).astype(BF16)
        rk_s[rows, :] = _rope(_dot(hb, w_ref[:, OFF_RK:OFF_RK + RET_W]), cos, sin, nf_r) * (RET_DK ** -0.5)
        rv_s[rows, :] = _dot(hb, w_ref[:, OFF_RV:OFF_RV + RET_VW]).astype(BF16)
        gf_s[rows, :] = _dot(hb, w_ref[:, OFF_RGF:OFF_RGF + RET_VW])
        gb_s[rows, :] = _dot(hb, w_ref[:, OFF_RGB:OFF_RGB + RET_VW])
        for j, (first, valid, scores) in zip(js, attention):
            _lat_attention_outputs(j, first, valid, scores, vt_s, cvt_s, sink_ref, attn_ref, layer)
        return carry

    lax.fori_loop(0, seq_len // (LAT_BLOCKS_PER_STEP * Q_BLOCK), query_blocks, 0)

    products = _retention_products(rq_s, rk_s, rv_s, intra_ref, kdec_ref, RET_HEADS, False)
    _retention_outputs(products, rq_s, gf_s, gb_s, cross_ref, gblk_ref, gn_ref, ret_ref, acc_s, RET_HEADS,
                       lambda backward, h: (s0b_ref if backward else s0f_ref)[0, 0, h], None)


def _latent_mixer(x, mod, n1, w_in, tables, cache_k, cache_v, sink, decay, gn, states0, layer, seq_len):
    n_tok = x.shape[0]
    past_len = cache_k.shape[3]
    seq = lambda w, **kw: pl.BlockSpec((seq_len, w), lambda b: (b, 0), **kw)
    per_batch = lambda dims: pl.BlockSpec((1, 1) + dims, lambda b: (b, layer) + (0,) * len(dims))
    cache = per_batch((KV_HEADS, past_len, HEAD_DIM))
    state = per_batch((RET_HEADS, RET_DK, RET_DV))
    n_blocks = seq_len // Q_BLOCK
    scratch = [pltpu.VMEM((seq_len, D_MODEL), BF16),
               pltpu.VMEM((seq_len, ATTN_W), BF16), pltpu.VMEM((seq_len, KV_W), BF16),
               pltpu.VMEM((n_blocks, KV_W, Q_BLOCK), BF16),
               pltpu.VMEM((KV_HEADS, past_len, HEAD_DIM), BF16),
               pltpu.VMEM((KV_W, past_len), BF16),
               pltpu.VMEM((seq_len, RET_W), BF16), pltpu.VMEM((seq_len, RET_W), F32),
               pltpu.VMEM((seq_len, RET_VW), BF16),
               pltpu.VMEM((seq_len, RET_VW), F32), pltpu.VMEM((seq_len, RET_VW), F32),
               pltpu.VMEM((seq_len, RET_VW), F32)]
    return pl.pallas_call(
        functools.partial(_lat_mix_kernel, layer=layer),
        grid=(n_tok // seq_len,),
        in_specs=[pl.BlockSpec(memory_space=pltpu.SMEM), seq(D_MODEL, pipeline_mode=pl.Buffered(1)),
                  pl.BlockSpec((None, None, 1, MOD_W), lambda b: (layer, 1 + b, 0, 0)),
                  _layer_spec((1, D_MODEL), layer), _const_spec((D_MODEL, IN_WIDTH)),
                  *[_const_spec((seq_len, LANES))] * 4, cache, cache,
                  *_decay_specs(layer, RET_HEADS, lambda b: 0), _layer_spec((1, RET_VW), layer),
                  state, state],
        out_specs=[seq(ATTN_W), seq(RET_VW)],
        out_shape=[jax.ShapeDtypeStruct((n_tok, ATTN_W), BF16), jax.ShapeDtypeStruct((n_tok, RET_VW), BF16)],
        scratch_shapes=scratch,
        compiler_params=_params(1),
        name="latent_mixer",
    )(sink, x, mod, n1, w_in, *tables, cache_k, cache_v, *decay, gn, *states0)


N_CTX_MIX_OUT = 7
N_CTX_MIX_SCRATCH = 9


def _ctx_mix_kernel(*refs, layer, seq_len):
    n_in = len(refs) - N_CTX_MIX_OUT - N_CTX_MIX_SCRATCH
    (sink_ref, x_ref, mod_ref, n1_ref, w_ref, intra_ref, cross_ref, kdec_ref, gblk_ref,
     gn_ref) = refs[:10]
    attn_ref, ret_ref, g_ref, kc_ref, vc_ref, sf_ref, sb_ref = refs[n_in:n_in + N_CTX_MIX_OUT]
    q_s, k_s, v_s, rq_s, rk_s, rv_s, gf_s, gb_s, acc_s = refs[n_in + N_CTX_MIX_OUT:]
    hb = _normed_input(x_ref, mod_ref, n1_ref)

    def proj(off, width):
        return _dot(hb, w_ref[:, off:off + width])

    batch_rows = [slice(b, b + seq_len) for b in range(0, x_ref.shape[0], seq_len)]
    q_s[...] = proj(OFF_Q, ATTN_W).astype(BF16)
    k = proj(OFF_K, KV_W)
    v = proj(OFF_V, KV_W)
    k_s[...] = k.astype(BF16)
    v_s[...] = v.astype(BF16)
    slot = _zero_other_layers((kc_ref, vc_ref), layer)
    for b, rows in enumerate(batch_rows):
        for kh in range(KV_HEADS):
            kc_ref[b, slot, kh] = k[rows, kh * HEAD_DIM:(kh + 1) * HEAD_DIM]
            vc_ref[b, slot, kh] = v[rows, kh * HEAD_DIM:(kh + 1) * HEAD_DIM]
    scores = _ctx_attention_scores(q_s, k_s, seq_len)
    rq_s[...] = proj(OFF_RQ, RET_W).astype(BF16)
    rk_s[...] = proj(OFF_RK, RET_W) * (RET_DK ** -0.5)
    rv_s[...] = proj(OFF_RV, RET_VW).astype(BF16)
    products = [_retention_products(rq_s.at[rows], rk_s.at[rows], rv_s.at[rows], intra_ref, kdec_ref,
                                    RET_HEADS, True) for rows in batch_rows]
    gf_s[...] = proj(OFF_RGF, RET_VW)
    gb_s[...] = proj(OFF_RGB, RET_VW)
    state_slot = _zero_other_layers((sf_ref, sb_ref), layer)

    def attention_outputs(b):
        _ctx_attention_outputs(scores, batch_rows[b].start, v_s, sink_ref, attn_ref, layer, seq_len)

    def retention_outputs(b):
        rows = batch_rows[b]

        def store_state(backward, h, state):
            (sb_ref if backward else sf_ref)[b, state_slot, h] = state

        _retention_outputs(products[b], rq_s.at[rows], gf_s.at[rows], gb_s.at[rows], cross_ref, gblk_ref,
                           gn_ref, ret_ref.at[rows], acc_s.at[rows], RET_HEADS,
                           lambda backward, h: None, store_state)

    stages = ([functools.partial(attention_outputs, b) for b in range(len(batch_rows))]
              + [functools.partial(retention_outputs, b) for b in range(len(batch_rows))])
    chunk = 2 * D_MODEL // len(stages)
    for n, stage in enumerate(stages):
        g_ref[:, n * chunk:(n + 1) * chunk] = proj(OFF_GA + n * chunk, chunk)
        stage()


def _context_mixer(x, mod, n1, w_in, sink, decay, gn, layer, seq_len, prev_outputs):
    n_tok = x.shape[0]
    n_batch = n_tok // seq_len
    per_tile = ROW_TILE // seq_len
    row = lambda w: pl.BlockSpec((ROW_TILE, w), lambda i: (i, 0))
    in_specs = [pl.BlockSpec(memory_space=pltpu.SMEM), row(D_MODEL), _mod_spec(layer, None),
                _layer_spec((1, D_MODEL), layer), _const_spec((D_MODEL, IN_WIDTH)),
                *_decay_specs(layer, RET_HEADS, lambda i: 0), _layer_spec((1, RET_VW), layer)]
    args = [sink, x, mod, n1, w_in, *decay, gn]
    out_specs = [row(ATTN_W), row(RET_VW), row(2 * D_MODEL)]
    out_shape = [jax.ShapeDtypeStruct((n_tok, ATTN_W), BF16), jax.ShapeDtypeStruct((n_tok, RET_VW), BF16),
                 jax.ShapeDtypeStruct((n_tok, 2 * D_MODEL), F32)]
    cache_dims = (KV_HEADS, seq_len, HEAD_DIM)
    state_dims = (RET_HEADS, RET_DK, RET_DV)
    aliases = {}
    for idx, dims in enumerate((cache_dims, cache_dims, state_dims, state_dims)):
        if prev_outputs is None:
            assert layer == 0
            out_specs.append(pl.BlockSpec((per_tile, DEPTH) + dims, lambda i: (i, 0, 0, 0, 0)))
        else:
            aliases[len(args)] = len(out_shape)
            in_specs.append(pl.BlockSpec(memory_space=pl.ANY))
            args.append(prev_outputs[idx])
            out_specs.append(pl.BlockSpec((per_tile, 1) + dims, lambda i: (i, layer, 0, 0, 0)))
        out_shape.append(jax.ShapeDtypeStruct((n_batch, DEPTH) + dims, F32))
    assert len(out_shape) == N_CTX_MIX_OUT
    scratch = [pltpu.VMEM((ROW_TILE, ATTN_W), BF16), pltpu.VMEM((ROW_TILE, KV_W), BF16),
               pltpu.VMEM((ROW_TILE, KV_W), BF16), pltpu.VMEM((ROW_TILE, RET_W), BF16),
               pltpu.VMEM((ROW_TILE, RET_W), F32), pltpu.VMEM((ROW_TILE, RET_VW), BF16),
               pltpu.VMEM((ROW_TILE, RET_VW), F32), pltpu.VMEM((ROW_TILE, RET_VW), F32),
               pltpu.VMEM((ROW_TILE, RET_VW), F32)]
    assert len(scratch) == N_CTX_MIX_SCRATCH
    return pl.pallas_call(
        functools.partial(_ctx_mix_kernel, layer=layer, seq_len=seq_len),
        grid=(n_tok // ROW_TILE,),
        in_specs=in_specs,
        out_specs=out_specs,
        out_shape=out_shape,
        scratch_shapes=scratch,
        input_output_aliases=aliases,
        compiler_params=_params(1),
        name="context_mixer",
    )(*args)


def _post_kernel(*refs, final, n_cast, own_gates):
    (attn_ref, ret_ref, g0_ref, g1_ref, x_ref, mod_ref, n2_ref,
     wba_ref, wbr_ref, wo_ref, w1_ref, w2_ref) = refs[:12]
    n_in = 12 + (1 if final else 0) + n_cast
    fn_ref = refs[12] if final else None
    out_ref = refs[n_in]
    for src, dst in zip(refs[n_in - n_cast:n_in], refs[n_in + 1:]):
        dst[...] = src[...].astype(BF16)
    mod = mod_ref[...]
    part = lambda k: mod[:, k * D_MODEL:(k + 1) * D_MODEL]
    if own_gates:
        hb = _normed_input(x_ref, mod_ref, g0_ref)
        gate_a = _dot(hb, g1_ref[:, OFF_GA:OFF_GA + D_MODEL])
        gate_r = _dot(hb, g1_ref[:, OFF_GA + D_MODEL:OFF_GA + 2 * D_MODEL])
    else:
        gate_a = g0_ref[...]
        gate_r = g1_ref[...]
    merged = (jax.nn.sigmoid(gate_a) * _dot(attn_ref[...], wba_ref[...])
              + jax.nn.sigmoid(gate_r) * _dot(ret_ref[...], wbr_ref[...]))
    x = x_ref[...] + part(2) * _dot(merged.astype(BF16), wo_ref[...])
    h2 = _modulated_norm(x, n2_ref[...], part(3), part(4)).astype(BF16)
    ff = None
    for c in range(0, D_FF, FF_CHUNK):
        a = jnp.maximum(_dot(h2, w1_ref[:, c:c + FF_CHUNK]), 0.0)
        t = _dot((a * a).astype(BF16), w2_ref[c:c + FF_CHUNK, :])
        ff = t if ff is None else ff + t
    x = x + part(5) * ff
    if final:
        x = x * lax.rsqrt(jnp.mean(x * x, axis=-1, keepdims=True) + NORM_EPS) * fn_ref[...]
    out_ref[...] = x


def _post(attn, ret, gates, x, mod, n2, weights, layer, seq_len, latent, final_gain, cast_next=()):
    n_tok = x.shape[0]
    n_steps = n_tok // ROW_TILE
    tiles_per_batch = seq_len // ROW_TILE
    final = final_gain is not None
    own_gates = isinstance(gates, tuple)
    row = lambda w, c=0: pl.BlockSpec((ROW_TILE, w), lambda i, c=c: (i, c))
    if own_gates:
        gate_specs = [_layer_spec((1, D_MODEL), layer), _const_spec((D_MODEL, IN_WIDTH))]
        gate_args = list(gates)
    else:
        gate_specs = [row(D_MODEL, 0), row(D_MODEL, 1)]
        gate_args = [gates, gates]
    in_specs = [
        row(ATTN_W), row(RET_VW), *gate_specs, row(D_MODEL),
        _mod_spec(layer, tiles_per_batch if latent else None),
        _layer_spec((1, D_MODEL), layer),
        *[_const_spec(w.shape) for w in weights],
    ]
    args = [attn, ret, *gate_args, x, mod, n2, *weights]
    if final:
        in_specs.append(_const_spec((1, D_MODEL)))
        args.append(final_gain)
    out_specs = [row(D_MODEL)]
    out_shape = [jax.ShapeDtypeStruct((n_tok, D_MODEL), F32)]
    for w in cast_next:
        _, k, n = w.shape
        rows = k // n_steps
        assert rows * n_steps == k and rows % 16 == 0
        in_specs.append(pl.BlockSpec((None, rows, n), lambda i: (layer + 1, i, 0)))
        args.append(w)
        out_specs.append(pl.BlockSpec((rows, n), lambda i: (i, 0)))
        out_shape.append(jax.ShapeDtypeStruct((k, n), BF16))
    return pl.pallas_call(
        functools.partial(_post_kernel, final=final, n_cast=len(cast_next), own_gates=own_gates),
        grid=(n_steps,),
        in_specs=in_specs,
        out_specs=out_specs,
        out_shape=out_shape,
        compiler_params=_params(1),
        name="post_final" if final else "post",
    )(*args)


def _rope_tables(seq_len):
    t = np.arange(seq_len)
    rows = (t // GRID_W).astype(np.float64)
    cols = (t % GRID_W).astype(np.float64)

    def table(head_dim):
        half = head_dim // 2
        nf = half // 2
        inv = ROPE_BASE ** (-np.arange(nf, dtype=np.float64) / nf)
        lane = np.arange(LANES) % head_dim
        pos = np.where((lane < half)[None, :], rows[:, None], cols[:, None])
        ang = pos * inv[lane % nf][None, :]
        sign = np.where((lane % half) < nf, -1.0, 1.0)[None, :]
        return jnp.asarray(np.cos(ang), F32), jnp.asarray(np.sin(ang) * sign, F32)

    cos_a, sin_a = table(HEAD_DIM)
    cos_r, sin_r = table(RET_DK)
    return cos_a, sin_a, cos_r, sin_r


def kernel(x_prompt, x_sample, cache_k, cache_v, state_ret_fwd, state_ret_bwd, c, c_ctx, norm1_g, norm2_g, final_norm_g, w_mod, b_mod, w_in, attn_sink, ret_decay_fwd, ret_decay_bwd, ret_gn_g, w_branch_attn, w_branch_ret, w_out, w_ff1, w_ff2):
    n_ctx, ctx_len, _ = x_prompt.shape
    n_lat, lat_len, _ = x_sample.shape
    xp = x_prompt.reshape(n_ctx * ctx_len, D_MODEL)
    xs = x_sample.reshape(n_lat * lat_len, D_MODEL)

    cond = jnp.concatenate([c_ctx[None, :], c], axis=0)
    cond8 = jnp.pad(cond, ((0, -cond.shape[0] % 8), (0, 0)))
    mod = _modulation(cond8, w_mod, b_mod).reshape(DEPTH, cond8.shape[0], 1, MOD_W)
    dec_rows = jnp.broadcast_to(jnp.stack([ret_decay_fwd, ret_decay_bwd], axis=1)[..., None],
                                (DEPTH, 2, RET_HEADS, LANES))
    decay = _decay_tables(dec_rows)
    tables = _rope_tables(lat_len)
    final_gain = final_norm_g.reshape(1, D_MODEL)

    n1 = norm1_g.reshape(DEPTH, 1, D_MODEL)
    n2 = norm2_g.reshape(DEPTH, 1, D_MODEL)
    gn = ret_gn_g.reshape(DEPTH, 1, RET_VW)
    f32_weights = (w_in, w_branch_attn, w_branch_ret, w_out, w_ff1, w_ff2)
    w_in_b, *post_w = [w[0].astype(BF16) for w in f32_weights]

    ctx_outputs = None
    for l in range(DEPTH):
        last = l == DEPTH - 1
        attn, ret, gates, *ctx_outputs = _context_mixer(xp, mod, n1, w_in_b, attn_sink, decay, gn, l,
                                                        ctx_len, ctx_outputs)
        xp, *next_weights = _post(attn, ret, gates, xp, mod, n2, post_w, l, ctx_len, False,
                                  final_gain if last else None, () if last else f32_weights)
        attn, ret = _latent_mixer(xs, mod, n1, w_in_b, tables, cache_k, cache_v, attn_sink, decay, gn,
                                  (state_ret_fwd, state_ret_bwd), l, lat_len)
        (xs,) = _post(attn, ret, (n1, w_in_b), xs, mod, n2, post_w, l, lat_len, True,
                      final_gain if last else None)
        if not last:
            w_in_b, *post_w = next_weights

    return (xp.reshape(n_ctx, ctx_len, D_MODEL), xs.reshape(n_lat, lat_len, D_MODEL), *ctx_outputs)
```

```python
import functools

import jax
import jax.numpy as jnp
import numpy as np
from jax import lax
from jax.experimental import pallas as pl
from jax.experimental.pallas import tpu as pltpu

D_MODEL = 1024
DEPTH = 4
GRID_W = 64
N_HEADS = 8
HEAD_DIM = 64
KV_HEADS = 2
GROUP = N_HEADS // KV_HEADS
WINDOW = 128
Q_BLOCK = 128
RET_HEADS = 4
RET_DK = 128
RET_DV = 128
ATTN_W = N_HEADS * HEAD_DIM
KV_W = KV_HEADS * HEAD_DIM
RET_W = RET_HEADS * RET_DK
RET_VW = RET_HEADS * RET_DV
GATE_W = 2 * RET_VW + 2 * D_MODEL
IN_WIDTH = ATTN_W + 2 * KV_W + 2 * RET_W + RET_VW + GATE_W
D_FF = 4 * D_MODEL
MOD_W = 6 * D_MODEL
ROPE_BASE = 10000.0
NORM_EPS = 1e-6
NEG_INF = -1e30

OFF_Q = 0
OFF_K = OFF_Q + ATTN_W
OFF_V = OFF_K + KV_W
OFF_RQ = OFF_V + KV_W
OFF_RK = OFF_RQ + RET_W
OFF_RV = OFF_RK + RET_W
OFF_RGF = OFF_RV + RET_VW
OFF_RGB = OFF_RGF + RET_VW
OFF_GA = OFF_RGB + RET_VW

LANES = 128
ROW_TILE = 512
RET_BLOCK = 256
LAT_BLOCKS_PER_STEP = 2
MOD_TILE_N = 1536
FF_CHUNK = 1024
VMEM_LIMIT = 58 * 1024 * 1024

BF16 = jnp.bfloat16
F32 = jnp.float32


def _params(n_axes):
    return pltpu.CompilerParams(dimension_semantics=("arbitrary",) * n_axes,
                                vmem_limit_bytes=VMEM_LIMIT)


def _const_spec(shape):
    nd = len(shape)
    return pl.BlockSpec(shape, lambda *_: (0,) * nd, pipeline_mode=pl.Buffered(1))


def _layer_spec(shape, layer):
    nd = len(shape)
    return pl.BlockSpec((None,) + tuple(shape), lambda *_: (layer,) + (0,) * nd,
                        pipeline_mode=pl.Buffered(1))


def _mod_spec(tiles_per_batch):
    if tiles_per_batch is None:
        return pl.BlockSpec((None, 1, MOD_W), lambda *_: (0, 0, 0))
    return pl.BlockSpec((None, 1, MOD_W), lambda i: (1 + i // tiles_per_batch, 0, 0))


def _zero_other_layers(out_refs, layer):
    n_slots = out_refs[0].shape[1]
    if n_slots == 1:
        return 0
    for ref in out_refs:
        for other in range(n_slots):
            if other != layer:
                ref[:, other] = jnp.zeros((ref.shape[0],) + ref.shape[2:], ref.dtype)
    return layer


def _dot(a, b):
    return jnp.dot(a, b, preferred_element_type=F32)


def _dot_nt(a, b):
    return lax.dot_general(a, b, (((1,), (1,)), ((), ())), preferred_element_type=F32)


def _dot_tn(a, b):
    return lax.dot_general(a, b, (((0,), (0,)), ((), ())), preferred_element_type=F32)


def _modulation_columns(cond_ref, w_ref, b_ref):
    cond = cond_ref[...]
    s = (cond * jax.nn.sigmoid(cond)).astype(BF16)
    return _dot(s, w_ref[...].astype(BF16)) + b_ref[...]


def _mod_kernel(cond_ref, w_ref, b_ref, out_ref):
    out_ref[...] = _modulation_columns(cond_ref, w_ref, b_ref)


def _mod_specs(layer, n_rows, tile_n):
    return ([pl.BlockSpec((n_rows, D_MODEL), lambda j: (0, 0)),
             pl.BlockSpec((None, D_MODEL, tile_n), lambda j: (layer, 0, j)),
             pl.BlockSpec((None, 1, tile_n), lambda j: (layer, 0, j))],
            pl.BlockSpec((n_rows, tile_n), lambda j: (0, j)))


def _modulation(cond8, w_mod, b_mod, layer):
    n_rows = cond8.shape[0]
    in_specs, out_spec = _mod_specs(layer, n_rows, MOD_TILE_N)
    return pl.pallas_call(
        _mod_kernel,
        grid=(MOD_W // MOD_TILE_N,),
        in_specs=in_specs,
        out_specs=out_spec,
        out_shape=jax.ShapeDtypeStruct((n_rows, MOD_W), F32),
        compiler_params=_params(1),
        name="adaln_modulation",
    )(cond8, w_mod, b_mod)


def _decay_kernel(dec_ref, intra_ref, cross_ref, kdec_ref, gblk_ref):
    backward = pl.program_id(1) == 1
    log_g = jnp.log(jax.nn.sigmoid(dec_ref[...]))
    n = RET_BLOCK
    row = lax.broadcasted_iota(jnp.int32, (n, n), 0).astype(F32)
    col = lax.broadcasted_iota(jnp.int32, (n, n), 1).astype(F32)
    diff = jnp.where(backward, col - row, row - col)
    row1 = row[:, :LANES]
    cross_pow = jnp.where(backward, n - row1, row1 + 1.0)
    kdec_pow = jnp.where(backward, row1, n - 1.0 - row1)
    for h in range(RET_HEADS):
        lg = log_g[h:h + 1, :]
        lg_wide = jnp.concatenate([lg] * (n // LANES), axis=1)
        intra_ref[h] = jnp.where(diff >= 0, jnp.exp(jnp.maximum(diff, 0.0) * lg_wide), 0.0)
        cross_ref[h] = jnp.exp(cross_pow * lg)
        kdec_ref[h] = jnp.exp(kdec_pow * lg)
        gblk_ref[h] = jnp.exp(n * lg)


def _decay_tables(dec_rows):
    lead = lambda *tail: pl.BlockSpec((None, None, RET_HEADS) + tail,
                                      lambda l, d: (l, d, 0) + (0,) * len(tail))
    shape = lambda *tail: jax.ShapeDtypeStruct((DEPTH, 2, RET_HEADS) + tail, F32)
    return pl.pallas_call(
        _decay_kernel,
        grid=(DEPTH, 2),
        in_specs=[lead(LANES)],
        out_specs=[lead(RET_BLOCK, RET_BLOCK), lead(RET_BLOCK, LANES), lead(RET_BLOCK, LANES),
                   lead(1, LANES)],
        out_shape=[shape(RET_BLOCK, RET_BLOCK), shape(RET_BLOCK, LANES), shape(RET_BLOCK, LANES),
                   shape(1, LANES)],
        compiler_params=_params(2),
        name="retention_decay_tables",
    )(dec_rows)


def _decay_specs(layer, n_heads, index_map):
    mode = {"pipeline_mode": pl.Buffered(1)} if n_heads == RET_HEADS else {}
    tab = lambda *tail: pl.BlockSpec((None, 2, n_heads) + tail,
                                     lambda *idx: (layer, 0, index_map(*idx)) + (0,) * len(tail), **mode)
    return [tab(RET_BLOCK, RET_BLOCK), tab(RET_BLOCK, LANES), tab(RET_BLOCK, LANES), tab(1, LANES)]


def _modulated_norm(x, gain, shift, scale):
    y = x * lax.rsqrt(jnp.mean(x * x, axis=-1, keepdims=True) + NORM_EPS) * gain
    return y * (1.0 + scale) + shift


def _normed_input(x_ref, mod_ref, n1_ref):
    mod = mod_ref[...]
    h = _modulated_norm(x_ref[...], n1_ref[...], mod[:, 0:D_MODEL], mod[:, D_MODEL:2 * D_MODEL])
    return h.astype(BF16)


def _rope_chunk(xc, cos, sin_signed, first_half, nf):
    partner = jnp.where(first_half, pltpu.roll(xc, LANES - nf, 1), pltpu.roll(xc, nf, 1))
    return xc * cos + partner * sin_signed


def _rope(p, cos, sin_signed, nf):
    lane = lax.broadcasted_iota(jnp.int32, (p.shape[0], LANES), 1)
    first_half = (lane % (2 * nf)) < nf
    parts = [_rope_chunk(p[:, c:c + LANES], cos, sin_signed, first_half, nf)
             for c in range(0, p.shape[1], LANES)]
    return parts[0] if len(parts) == 1 else jnp.concatenate(parts, axis=1)


def _softmax_t(score_blocks, sink):
    m = sink
    for s in score_blocks:
        m = jnp.maximum(jnp.max(s, axis=0, keepdims=True), m)
    es = [jnp.exp(s - m) for s in score_blocks]
    denom = jnp.exp(sink - m)
    for e in es:
        denom = denom + jnp.sum(e, axis=0, keepdims=True)
    return [e.astype(BF16) for e in es], 1.0 / denom


def _store_heads(o_ref, rows, heads, o_t, inv):
    for pair in range(0, len(heads), 2):
        halves = [o_t[:, g * Q_BLOCK:(g + 1) * Q_BLOCK] * inv[g] for g in (pair, pair + 1)]
        both = jnp.transpose(jnp.concatenate(halves, axis=0))
        h0 = heads[pair]
        o_ref[rows, h0 * HEAD_DIM:(h0 + 2) * HEAD_DIM] = both.astype(BF16)


def _ctx_attention_scores(q_ref, k_ref, seq_len):
    scale = HEAD_DIM ** -0.5
    out = []
    for b in range(0, q_ref.shape[0], seq_len):
        for kh in range(KV_HEADS):
            for c in range(b, b + seq_len, Q_BLOCK):
                heads = [kh * GROUP + g for g in range(GROUP)]
                q4 = jnp.concatenate(
                    [q_ref[c:c + Q_BLOCK, h * HEAD_DIM:(h + 1) * HEAD_DIM] for h in heads], axis=0)
                keys = k_ref[b:b + seq_len, kh * HEAD_DIM:(kh + 1) * HEAD_DIM]
                out.append(((b, kh, c), _dot_nt(keys, q4) * scale))
    return out


def _ctx_attention_outputs(scores, b, v_ref, sink_ref, o_ref, layer, seq_len):
    v_t = jnp.transpose(v_ref[b:b + seq_len, :].astype(F32)).astype(BF16)
    for (sb, kh, c), s4 in scores:
        if sb != b:
            continue
        heads = [kh * GROUP + g for g in range(GROUP)]
        es, inv = [], []
        for g, h in enumerate(heads):
            (e,), r = _softmax_t([s4[:, g * Q_BLOCK:(g + 1) * Q_BLOCK]], sink_ref[layer, h])
            es.append(e)
            inv.append(r)
        o_t = _dot(v_t[kh * HEAD_DIM:(kh + 1) * HEAD_DIM, :], jnp.concatenate(es, axis=1))
        _store_heads(o_ref, slice(c, c + Q_BLOCK), heads, o_t, inv)


BAND_BLOCKS = 1 + 2 * WINDOW // Q_BLOCK


def _lat_attention_scores(j, q_ref, k_ref, ckb_ref):
    scale = HEAD_DIM ** -0.5
    n_blocks = k_ref.shape[0] // Q_BLOCK
    band = BAND_BLOCKS * Q_BLOCK
    first = jnp.clip(j - WINDOW // Q_BLOCK, 0, n_blocks - BAND_BLOCKS)
    start = pl.multiple_of(first * Q_BLOCK, Q_BLOCK)
    kpos = start + lax.broadcasted_iota(jnp.int32, (band, Q_BLOCK), 0)
    qpos = j * Q_BLOCK + lax.broadcasted_iota(jnp.int32, (band, Q_BLOCK), 1)
    valid = jnp.abs(qpos - kpos) <= WINDOW
    kb = k_ref[pl.ds(start, band), :]
    rows = pl.ds(pl.multiple_of(j * Q_BLOCK, Q_BLOCK), Q_BLOCK)
    scores = []
    for kh in range(KV_HEADS):
        hd = slice(kh * HEAD_DIM, (kh + 1) * HEAD_DIM)
        heads = [kh * GROUP + g for g in range(GROUP)]
        q4 = jnp.concatenate([q_ref[rows, h * HEAD_DIM:(h + 1) * HEAD_DIM] for h in heads], axis=0)
        scores.append((_dot_nt(kb[:, hd], q4) * scale, _dot_nt(ckb_ref[kh], q4) * scale))
    return first, valid, scores


def _lat_attention_outputs(j, first, valid, scores, vt_ref, cvt_ref, sink_ref, o_ref, layer):
    rows = pl.ds(pl.multiple_of(j * Q_BLOCK, Q_BLOCK), Q_BLOCK)
    for kh, (s_loc4, s_ctx4) in enumerate(scores):
        hd = slice(kh * HEAD_DIM, (kh + 1) * HEAD_DIM)
        heads = [kh * GROUP + g for g in range(GROUP)]
        e_loc, e_ctx, inv = [], [], []
        for g, h in enumerate(heads):
            cols = slice(g * Q_BLOCK, (g + 1) * Q_BLOCK)
            (el, ec), r = _softmax_t([jnp.where(valid, s_loc4[:, cols], NEG_INF), s_ctx4[:, cols]],
                                     sink_ref[layer, h])
            e_loc.append(el)
            e_ctx.append(ec)
            inv.append(r)
        e_loc4 = jnp.concatenate(e_loc, axis=1)
        o_t = _dot(cvt_ref[hd, :], jnp.concatenate(e_ctx, axis=1))
        for t in range(BAND_BLOCKS):
            o_t = o_t + _dot(vt_ref[first + t, hd, :], e_loc4[t * Q_BLOCK:(t + 1) * Q_BLOCK, :])
        _store_heads(o_ref, rows, heads, o_t, inv)


def _retention_products(rq_ref, rk_ref, rv_ref, intra_ref, kdec_ref, n_heads, final_state):
    n_blocks = rq_ref.shape[0] // RET_BLOCK
    blocks = [slice(i * RET_BLOCK, (i + 1) * RET_BLOCK) for i in range(n_blocks)]
    head_cols = [slice(h * RET_DK, (h + 1) * RET_DK) for h in range(n_heads)]
    qk = {(h, i): _dot_nt(rq_ref[blocks[i], head_cols[h]], rk_ref[blocks[i], head_cols[h]].astype(BF16))
          for h in range(n_heads) for i in range(n_blocks)}
    o_intra, kv = {}, {}
    for h in range(n_heads):
        for d in range(2):
            for i in range(n_blocks):
                v_i = rv_ref[blocks[i], head_cols[h]]
                o_intra[h, d, i] = _dot((qk[h, i] * intra_ref[d, h]).astype(BF16), v_i)
                last = i == (0 if d else n_blocks - 1)
                if final_state or not last:
                    k_i = rk_ref[blocks[i], head_cols[h]]
                    kv[h, d, i] = _dot_tn((k_i * kdec_ref[d, h]).astype(BF16), v_i)
    return o_intra, kv


def _retention_outputs(products, rq_ref, gf_ref, gb_ref, cross_ref, gblk_ref, gn_ref, ret_ref, acc_ref,
                       n_heads, initial_state, store_state):
    o_intra, kv = products
    n_blocks = rq_ref.shape[0] // RET_BLOCK
    blocks = [slice(i * RET_BLOCK, (i + 1) * RET_BLOCK) for i in range(n_blocks)]
    for h in range(n_heads):
        hs = slice(h * RET_DK, (h + 1) * RET_DK)
        for d, backward in enumerate((False, True)):
            g_ref = gb_ref if backward else gf_ref
            cross = cross_ref[d, h]
            g_block = gblk_ref[d, h]
            state = initial_state(backward, h)
            order = list(range(n_blocks))
            if backward:
                order.reverse()
            for i in order:
                o = o_intra[h, d, i]
                if state is not None:
                    o = o + _dot(rq_ref[blocks[i], hs], state.astype(BF16)) * cross
                if (h, d, i) in kv:
                    state = kv[h, d, i] if state is None else state * g_block + kv[h, d, i]
                mu = jnp.mean(o, axis=-1, keepdims=True)
                dev = o - mu
                var = jnp.mean(dev * dev, axis=-1, keepdims=True)
                y = dev * lax.rsqrt(var + NORM_EPS) * gn_ref[:, hs]
                gate = g_ref[blocks[i], hs]
                contrib = gate * jax.nn.sigmoid(gate) * y
                if backward:
                    ret_ref[blocks[i], hs] = (acc_ref[blocks[i], hs] + contrib).astype(BF16)
                else:
                    acc_ref[blocks[i], hs] = contrib
            if store_state is not None:
                store_state(backward, h, state)


def _lat_mix_kernel(sink_ref, x_ref, mod_ref, n1_ref, w_ref, cos_a, sin_a, cos_r, sin_r, ck_ref, cv_ref,
                    intra_ref, cross_ref, kdec_ref, gblk_ref, gn_ref, s0f_ref, s0b_ref,
                    attn_ref, ret_ref,
                    hb_s, q_s, k_s, vt_s, ckb_s, cvt_s, rq_s, rk_s, rv_s, gf_s, gb_s, acc_s, *, layer):
    seq_len = x_ref.shape[0]
    nf_a = HEAD_DIM // 4
    nf_r = RET_DK // 4

    hb_s[...] = _normed_input(x_ref, mod_ref, n1_ref)
    for r0 in range(0, seq_len, ROW_TILE):
        rows = slice(r0, r0 + ROW_TILE)
        hb = hb_s[rows, :]
        q_s[rows, :] = _rope(_dot(hb, w_ref[:, OFF_Q:OFF_Q + ATTN_W]), cos_a[rows, :], sin_a[rows, :],
                             nf_a).astype(BF16)
        k_s[rows, :] = _rope(_dot(hb, w_ref[:, OFF_K:OFF_K + KV_W]), cos_a[rows, :], sin_a[rows, :],
                             nf_a).astype(BF16)
        v = _dot(hb, w_ref[:, OFF_V:OFF_V + KV_W])
        for t in range(ROW_TILE // Q_BLOCK):
            vt_s[r0 // Q_BLOCK + t] = jnp.transpose(v[t * Q_BLOCK:(t + 1) * Q_BLOCK, :]).astype(BF16)
    cvt_s[...] = jnp.concatenate([cv_ref[0, 0, kh] for kh in range(KV_HEADS)], axis=0).astype(BF16)
    ck = jnp.transpose(jnp.concatenate([ck_ref[0, 0, kh] for kh in range(KV_HEADS)], axis=0))
    for kh in range(KV_HEADS):
        ckb_s[kh] = ck[:, kh * HEAD_DIM:(kh + 1) * HEAD_DIM].astype(BF16)

    def query_blocks(i, carry):
        js = [i * LAT_BLOCKS_PER_STEP + n for n in range(LAT_BLOCKS_PER_STEP)]
        attention = [_lat_attention_scores(j, q_s, k_s, ckb_s) for j in js]
        n_rows = LAT_BLOCKS_PER_STEP * Q_BLOCK
        rows = pl.ds(pl.multiple_of(i * n_rows, n_rows), n_rows)
        hb = hb_s[rows, :]
        cos = cos_r[rows, :]
        sin = sin_r[rows, :]
        rq_s[rows, :] = _rope(_dot(hb, w_ref[:, OFF_RQ:OFF_RQ + RET_W]), cos, sin, nf_r).astype(BF16)
        rk_s[rows, :] = _rope(_dot(hb, w_ref[:, OFF_RK:OFF_RK + RET_W]), cos, sin, nf_r) * (RET_DK ** -0.5)
        rv_s[rows, :] = _dot(hb, w_ref[:, OFF_RV:OFF_RV + RET_VW]).astype(BF16)
        gf_s[rows, :] = _dot(hb, w_ref[:, OFF_RGF:OFF_RGF + RET_VW])
        gb_s[rows, :] = _dot(hb, w_ref[:, OFF_RGB:OFF_RGB + RET_VW])
        for j, (first, valid, scores) in zip(js, attention):
            _lat_attention_outputs(j, first, valid, scores, vt_s, cvt_s, sink_ref, attn_ref, layer)
        return carry

    lax.fori_loop(0, seq_len // (LAT_BLOCKS_PER_STEP * Q_BLOCK), query_blocks, 0)

    products = _retention_products(rq_s, rk_s, rv_s, intra_ref, kdec_ref, RET_HEADS, False)
    _retention_outputs(products, rq_s, gf_s, gb_s, cross_ref, gblk_ref, gn_ref, ret_ref, acc_s, RET_HEADS,
                       lambda backward, h: (s0b_ref if backward else s0f_ref)[0, 0, h], None)


def _latent_mixer(x, mod, n1, w_in, tables, cache_k, cache_v, sink, decay, gn, states0, layer, seq_len):
    n_tok = x.shape[0]
    past_len = cache_k.shape[4]
    seq = lambda w, **kw: pl.BlockSpec((seq_len, w), lambda b: (b, 0), **kw)
    per_batch = lambda dims: pl.BlockSpec((1, 1) + dims, lambda b: (b, layer) + (0,) * len(dims))
    cache = per_batch((KV_HEADS, HEAD_DIM, past_len))
    state = per_batch((RET_HEADS, RET_DK, RET_DV))
    n_blocks = seq_len // Q_BLOCK
    scratch = [pltpu.VMEM((seq_len, D_MODEL), BF16),
               pltpu.VMEM((seq_len, ATTN_W), BF16), pltpu.VMEM((seq_len, KV_W), BF16),
               pltpu.VMEM((n_blocks, KV_W, Q_BLOCK), BF16),
               pltpu.VMEM((KV_HEADS, past_len, HEAD_DIM), BF16),
               pltpu.VMEM((KV_W, past_len), BF16),
               pltpu.VMEM((seq_len, RET_W), BF16), pltpu.VMEM((seq_len, RET_W), F32),
               pltpu.VMEM((seq_len, RET_VW), BF16),
               pltpu.VMEM((seq_len, RET_VW), F32), pltpu.VMEM((seq_len, RET_VW), F32),
               pltpu.VMEM((seq_len, RET_VW), F32)]
    return pl.pallas_call(
        functools.partial(_lat_mix_kernel, layer=layer),
        grid=(n_tok // seq_len,),
        in_specs=[pl.BlockSpec(memory_space=pltpu.SMEM), seq(D_MODEL, pipeline_mode=pl.Buffered(1)),
                  _mod_spec(1),
                  _layer_spec((1, D_MODEL), layer), _const_spec((D_MODEL, IN_WIDTH)),
                  *[_const_spec((seq_len, LANES))] * 4, cache, cache,
                  *_decay_specs(layer, RET_HEADS, lambda b: 0), _layer_spec((1, RET_VW), layer),
                  state, state],
        out_specs=[seq(ATTN_W), seq(RET_VW)],
        out_shape=[jax.ShapeDtypeStruct((n_tok, ATTN_W), BF16), jax.ShapeDtypeStruct((n_tok, RET_VW), BF16)],
        scratch_shapes=scratch,
        compiler_params=_params(1),
        name="latent_mixer",
    )(sink, x, mod, n1, w_in, *tables, cache_k, cache_v, *decay, gn, *states0)


N_CTX_MIX_OUT = 7
N_CTX_MIX_SCRATCH = 9


def _ctx_mix_kernel(*refs, layer, seq_len):
    n_in = len(refs) - N_CTX_MIX_OUT - N_CTX_MIX_SCRATCH
    (sink_ref, x_ref, mod_ref, n1_ref, w_ref, intra_ref, cross_ref, kdec_ref, gblk_ref,
     gn_ref) = refs[:10]
    attn_ref, ret_ref, g_ref, kc_ref, vc_ref, sf_ref, sb_ref = refs[n_in:n_in + N_CTX_MIX_OUT]
    q_s, k_s, v_s, rq_s, rk_s, rv_s, gf_s, gb_s, acc_s = refs[n_in + N_CTX_MIX_OUT:]
    hb = _normed_input(x_ref, mod_ref, n1_ref)

    def proj(off, width):
        return _dot(hb, w_ref[:, off:off + width])

    batch_rows = [slice(b, b + seq_len) for b in range(0, x_ref.shape[0], seq_len)]
    q_s[...] = proj(OFF_Q, ATTN_W).astype(BF16)
    k = proj(OFF_K, KV_W)
    v = proj(OFF_V, KV_W)
    k_s[...] = k.astype(BF16)
    v_s[...] = v.astype(BF16)
    slot = _zero_other_layers((kc_ref, vc_ref), layer)
    for b, rows in enumerate(batch_rows):
        for kh in range(KV_HEADS):
            kc_ref[b, slot, kh] = k[rows, kh * HEAD_DIM:(kh + 1) * HEAD_DIM]
            vc_ref[b, slot, kh] = v[rows, kh * HEAD_DIM:(kh + 1) * HEAD_DIM]
    scores = _ctx_attention_scores(q_s, k_s, seq_len)
    rq_s[...] = proj(OFF_RQ, RET_W).astype(BF16)
    rk_s[...] = proj(OFF_RK, RET_W) * (RET_DK ** -0.5)
    rv_s[...] = proj(OFF_RV, RET_VW).astype(BF16)
    products = [_retention_products(rq_s.at[rows], rk_s.at[rows], rv_s.at[rows], intra_ref, kdec_ref,
                                    RET_HEADS, True) for rows in batch_rows]
    gf_s[...] = proj(OFF_RGF, RET_VW)
    gb_s[...] = proj(OFF_RGB, RET_VW)
    state_slot = _zero_other_layers((sf_ref, sb_ref), layer)

    def attention_outputs(b):
        _ctx_attention_outputs(scores, batch_rows[b].start, v_s, sink_ref, attn_ref, layer, seq_len)

    def retention_outputs(b):
        rows = batch_rows[b]

        def store_state(backward, h, state):
            (sb_ref if backward else sf_ref)[b, state_slot, h] = state

        _retention_outputs(products[b], rq_s.at[rows], gf_s.at[rows], gb_s.at[rows], cross_ref, gblk_ref,
                           gn_ref, ret_ref.at[rows], acc_s.at[rows], RET_HEADS,
                           lambda backward, h: None, store_state)

    stages = ([functools.partial(attention_outputs, b) for b in range(len(batch_rows))]
              + [functools.partial(retention_outputs, b) for b in range(len(batch_rows))])
    chunk = 2 * D_MODEL // len(stages)
    for n, stage in enumerate(stages):
        g_ref[:, n * chunk:(n + 1) * chunk] = proj(OFF_GA + n * chunk, chunk)
        stage()


def _context_mixer(x, mod, n1, w_in, sink, decay, gn, layer, seq_len, prev_outputs):
    n_tok = x.shape[0]
    n_batch = n_tok // seq_len
    per_tile = ROW_TILE // seq_len
    row = lambda w: pl.BlockSpec((ROW_TILE, w), lambda i: (i, 0))
    in_specs = [pl.BlockSpec(memory_space=pltpu.SMEM), row(D_MODEL), _mod_spec(None),
                _layer_spec((1, D_MODEL), layer), _const_spec((D_MODEL, IN_WIDTH)),
                *_decay_specs(layer, RET_HEADS, lambda i: 0), _layer_spec((1, RET_VW), layer)]
    args = [sink, x, mod, n1, w_in, *decay, gn]
    out_specs = [row(ATTN_W), row(RET_VW), row(2 * D_MODEL)]
    out_shape = [jax.ShapeDtypeStruct((n_tok, ATTN_W), BF16), jax.ShapeDtypeStruct((n_tok, RET_VW), BF16),
                 jax.ShapeDtypeStruct((n_tok, 2 * D_MODEL), F32)]
    cache_dims = (KV_HEADS, seq_len, HEAD_DIM)
    state_dims = (RET_HEADS, RET_DK, RET_DV)
    aliases = {}
    for idx, dims in enumerate((cache_dims, cache_dims, state_dims, state_dims)):
        if prev_outputs is None:
            assert layer == 0
            out_specs.append(pl.BlockSpec((per_tile, DEPTH) + dims, lambda i: (i, 0, 0, 0, 0)))
        else:
            aliases[len(args)] = len(out_shape)
            in_specs.append(pl.BlockSpec(memory_space=pl.ANY))
            args.append(prev_outputs[idx])
            out_specs.append(pl.BlockSpec((per_tile, 1) + dims, lambda i: (i, layer, 0, 0, 0)))
        out_shape.append(jax.ShapeDtypeStruct((n_batch, DEPTH) + dims, F32))
    assert len(out_shape) == N_CTX_MIX_OUT
    scratch = [pltpu.VMEM((ROW_TILE, ATTN_W), BF16), pltpu.VMEM((ROW_TILE, KV_W), BF16),
               pltpu.VMEM((ROW_TILE, KV_W), BF16), pltpu.VMEM((ROW_TILE, RET_W), BF16),
               pltpu.VMEM((ROW_TILE, RET_W), F32), pltpu.VMEM((ROW_TILE, RET_VW), BF16),
               pltpu.VMEM((ROW_TILE, RET_VW), F32), pltpu.VMEM((ROW_TILE, RET_VW), F32),
               pltpu.VMEM((ROW_TILE, RET_VW), F32)]
    assert len(scratch) == N_CTX_MIX_SCRATCH
    return pl.pallas_call(
        functools.partial(_ctx_mix_kernel, layer=layer, seq_len=seq_len),
        grid=(n_tok // ROW_TILE,),
        in_specs=in_specs,
        out_specs=out_specs,
        out_shape=out_shape,
        scratch_shapes=scratch,
        input_output_aliases=aliases,
        compiler_params=_params(1),
        name="context_mixer",
    )(*args)


def _post_kernel(*refs, final, n_cast, next_mod, own_gates):
    (attn_ref, ret_ref, g0_ref, g1_ref, x_ref, mod_ref, n2_ref,
     wba_ref, wbr_ref, wo_ref, w1_ref, w2_ref) = refs[:12]
    n_side_in = n_cast + (3 if next_mod else 0)
    n_in = 12 + (1 if final else 0) + n_side_in
    fn_ref = refs[12] if final else None
    out_ref = refs[n_in]
    side_in = refs[n_in - n_side_in:n_in]
    side_out = refs[n_in + 1:]
    for src, dst in zip(side_in[:n_cast], side_out[:n_cast]):
        dst[...] = src[...].astype(BF16)
    if next_mod:
        side_out[n_cast][...] = _modulation_columns(*side_in[n_cast:])
    mod = mod_ref[...]
    part = lambda k: mod[:, k * D_MODEL:(k + 1) * D_MODEL]
    if own_gates:
        hb = _normed_input(x_ref, mod_ref, g0_ref)
        gate_a = _dot(hb, g1_ref[:, OFF_GA:OFF_GA + D_MODEL])
        gate_r = _dot(hb, g1_ref[:, OFF_GA + D_MODEL:OFF_GA + 2 * D_MODEL])
    else:
        gate_a = g0_ref[...]
        gate_r = g1_ref[...]
    merged = (jax.nn.sigmoid(gate_a) * _dot(attn_ref[...], wba_ref[...])
              + jax.nn.sigmoid(gate_r) * _dot(ret_ref[...], wbr_ref[...]))
    x = x_ref[...] + part(2) * _dot(merged.astype(BF16), wo_ref[...])
    h2 = _modulated_norm(x, n2_ref[...], part(3), part(4)).astype(BF16)
    ff = None
    for c in range(0, D_FF, FF_CHUNK):
        a = jnp.maximum(_dot(h2, w1_ref[:, c:c + FF_CHUNK]), 0.0)
        t = _dot((a * a).astype(BF16), w2_ref[c:c + FF_CHUNK, :])
        ff = t if ff is None else ff + t
    x = x + part(5) * ff
    if final:
        x = x * lax.rsqrt(jnp.mean(x * x, axis=-1, keepdims=True) + NORM_EPS) * fn_ref[...]
    out_ref[...] = x


def _post(attn, ret, gates, x, mod, n2, weights, layer, seq_len, latent, final_gain, cast_next=(),
          mod_next=None):
    n_tok = x.shape[0]
    n_steps = n_tok // ROW_TILE
    tiles_per_batch = seq_len // ROW_TILE
    final = final_gain is not None
    own_gates = isinstance(gates, tuple)
    row = lambda w, c=0: pl.BlockSpec((ROW_TILE, w), lambda i, c=c: (i, c))
    if own_gates:
        gate_specs = [_layer_spec((1, D_MODEL), layer), _const_spec((D_MODEL, IN_WIDTH))]
        gate_args = list(gates)
    else:
        gate_specs = [row(D_MODEL, 0), row(D_MODEL, 1)]
        gate_args = [gates, gates]
    in_specs = [
        row(ATTN_W), row(RET_VW), *gate_specs, row(D_MODEL),
        _mod_spec(tiles_per_batch if latent else None),
        _layer_spec((1, D_MODEL), layer),
        *[_const_spec(w.shape) for w in weights],
    ]
    args = [attn, ret, *gate_args, x, mod, n2, *weights]
    if final:
        in_specs.append(_const_spec((1, D_MODEL)))
        args.append(final_gain)
    out_specs = [row(D_MODEL)]
    out_shape = [jax.ShapeDtypeStruct((n_tok, D_MODEL), F32)]
    for w in cast_next:
        _, k, n = w.shape
        rows = k // n_steps
        assert rows * n_steps == k and rows % 16 == 0
        in_specs.append(pl.BlockSpec((None, rows, n), lambda i: (layer + 1, i, 0)))
        args.append(w)
        out_specs.append(pl.BlockSpec((rows, n), lambda i: (i, 0)))
        out_shape.append(jax.ShapeDtypeStruct((k, n), BF16))
    if mod_next is not None:
        n_rows = mod_next[0].shape[0]
        mod_in_specs, mod_out_spec = _mod_specs(layer + 1, n_rows, MOD_W // n_steps)
        in_specs += mod_in_specs
        args += list(mod_next)
        out_specs.append(mod_out_spec)
        out_shape.append(jax.ShapeDtypeStruct((n_rows, MOD_W), F32))
    return pl.pallas_call(
        functools.partial(_post_kernel, final=final, n_cast=len(cast_next),
                          next_mod=mod_next is not None, own_gates=own_gates),
        grid=(n_steps,),
        in_specs=in_specs,
        out_specs=out_specs,
        out_shape=out_shape,
        compiler_params=_params(1),
        name="post_final" if final else "post",
    )(*args)


def _rope_tables(seq_len):
    t = np.arange(seq_len)
    rows = (t // GRID_W).astype(np.float64)
    cols = (t % GRID_W).astype(np.float64)

    def table(head_dim):
        half = head_dim // 2
        nf = half // 2
        inv = ROPE_BASE ** (-np.arange(nf, dtype=np.float64) / nf)
        lane = np.arange(LANES) % head_dim
        pos = np.where((lane < half)[None, :], rows[:, None], cols[:, None])
        ang = pos * inv[lane % nf][None, :]
        sign = np.where((lane % half) < nf, -1.0, 1.0)[None, :]
        return jnp.asarray(np.cos(ang), F32), jnp.asarray(np.sin(ang) * sign, F32)

    cos_a, sin_a = table(HEAD_DIM)
    cos_r, sin_r = table(RET_DK)
    return cos_a, sin_a, cos_r, sin_r


def kernel(x_prompt, x_sample, cache_k, cache_v, state_ret_fwd, state_ret_bwd, c, c_ctx, norm1_g, norm2_g, final_norm_g, w_mod, b_mod, w_in, attn_sink, ret_decay_fwd, ret_decay_bwd, ret_gn_g, w_branch_attn, w_branch_ret, w_out, w_ff1, w_ff2):
    n_ctx, ctx_len, _ = x_prompt.shape
    n_lat, lat_len, _ = x_sample.shape
    xp = x_prompt.reshape(n_ctx * ctx_len, D_MODEL)
    xs = x_sample.reshape(n_lat * lat_len, D_MODEL)

    cond = jnp.concatenate([c_ctx[None, :], c], axis=0)
    cond8 = jnp.pad(cond, ((0, -cond.shape[0] % 8), (0, 0)))
    mod_inputs = (cond8, w_mod, b_mod.reshape(DEPTH, 1, MOD_W))
    mod = _modulation(*mod_inputs, 0).reshape(cond8.shape[0], 1, MOD_W)
    cache_kt = jnp.swapaxes(cache_k, 3, 4)
    cache_vt = jnp.swapaxes(cache_v, 3, 4)
    dec_rows = jnp.broadcast_to(jnp.stack([ret_decay_fwd, ret_decay_bwd], axis=1)[..., None],
                                (DEPTH, 2, RET_HEADS, LANES))
    decay = _decay_tables(dec_rows)
    tables = _rope_tables(lat_len)
    final_gain = final_norm_g.reshape(1, D_MODEL)

    n1 = norm1_g.reshape(DEPTH, 1, D_MODEL)
    n2 = norm2_g.reshape(DEPTH, 1, D_MODEL)
    gn = ret_gn_g.reshape(DEPTH, 1, RET_VW)
    f32_weights = (w_in, w_branch_attn, w_branch_ret, w_out, w_ff1, w_ff2)
    w_in_b, *post_w = [w[0].astype(BF16) for w in f32_weights]

    ctx_outputs = None
    for l in range(DEPTH):
        last = l == DEPTH - 1
        attn, ret, gates, *ctx_outputs = _context_mixer(xp, mod, n1, w_in_b, attn_sink, decay, gn, l,
                                                        ctx_len, ctx_outputs)
        xp, *next_layer = _post(attn, ret, gates, xp, mod, n2, post_w, l, ctx_len, False,
                                final_gain if last else None, () if last else f32_weights,
                                None if last else mod_inputs)
        attn, ret = _latent_mixer(xs, mod, n1, w_in_b, tables, cache_kt, cache_vt, attn_sink, decay, gn,
                                  (state_ret_fwd, state_ret_bwd), l, lat_len)
        (xs,) = _post(attn, ret, (n1, w_in_b), xs, mod, n2, post_w, l, lat_len, True,
                      final_gain if last else None)
        if not last:
            w_in_b, *post_w, mod = next_layer
            mod = mod.reshape(cond8.shape[0], 1, MOD_W)

    return (xp.reshape(n_ctx, ctx_len, D_MODEL), xs.reshape(n_lat, lat_len, D_MODEL), *ctx_outputs)
```

```python
import functools

import jax
import jax.numpy as jnp
import numpy as np
from jax import lax
from jax.experimental import pallas as pl
from jax.experimental.pallas import tpu as pltpu

D_MODEL = 1024
DEPTH = 4
GRID_W = 64
N_HEADS = 8
HEAD_DIM = 64
KV_HEADS = 2
GROUP = N_HEADS // KV_HEADS
WINDOW = 128
Q_BLOCK = 128
RET_HEADS = 4
RET_DK = 128
RET_DV = 128
ATTN_W = N_HEADS * HEAD_DIM
KV_W = KV_HEADS * HEAD_DIM
RET_W = RET_HEADS * RET_DK
RET_VW = RET_HEADS * RET_DV
GATE_W = 2 * RET_VW + 2 * D_MODEL
IN_WIDTH = ATTN_W + 2 * KV_W + 2 * RET_W + RET_VW + GATE_W
D_FF = 4 * D_MODEL
MOD_W = 6 * D_MODEL
ROPE_BASE = 10000.0
NORM_EPS = 1e-6
NEG_INF = -1e30

OFF_Q = 0
OFF_K = OFF_Q + ATTN_W
OFF_V = OFF_K + KV_W
OFF_RQ = OFF_V + KV_W
OFF_RK = OFF_RQ + RET_W
OFF_RV = OFF_RK + RET_W
OFF_RGF = OFF_RV + RET_VW
OFF_RGB = OFF_RGF + RET_VW
OFF_GA = OFF_RGB + RET_VW

LANES = 128
ROW_TILE = 512
RET_BLOCK = 256
LAT_BLOCKS_PER_STEP = 2
MOD_TILE_N = 1536
FF_CHUNK = 1024
VMEM_LIMIT = 58 * 1024 * 1024

BF16 = jnp.bfloat16
F32 = jnp.float32


def _params(n_axes):
    return pltpu.CompilerParams(dimension_semantics=("arbitrary",) * n_axes,
                                vmem_limit_bytes=VMEM_LIMIT)


def _const_spec(shape):
    nd = len(shape)
    return pl.BlockSpec(shape, lambda *_: (0,) * nd, pipeline_mode=pl.Buffered(1))


def _layer_spec(shape, layer):
    nd = len(shape)
    return pl.BlockSpec((None,) + tuple(shape), lambda *_: (layer,) + (0,) * nd,
                        pipeline_mode=pl.Buffered(1))


def _mod_spec(tiles_per_batch):
    if tiles_per_batch is None:
        return pl.BlockSpec((None, 1, MOD_W), lambda *_: (0, 0, 0))
    return pl.BlockSpec((None, 1, MOD_W), lambda i: (1 + i // tiles_per_batch, 0, 0))


def _zero_other_layers(out_refs, layer):
    n_slots = out_refs[0].shape[1]
    if n_slots == 1:
        return 0
    for ref in out_refs:
        for other in range(n_slots):
            if other != layer:
                ref[:, other] = jnp.zeros((ref.shape[0],) + ref.shape[2:], ref.dtype)
    return layer


def _dot(a, b):
    return jnp.dot(a, b, preferred_element_type=F32)


def _dot_nt(a, b):
    return lax.dot_general(a, b, (((1,), (1,)), ((), ())), preferred_element_type=F32)


def _dot_tn(a, b):
    return lax.dot_general(a, b, (((0,), (0,)), ((), ())), preferred_element_type=F32)


def _modulation_columns(cond_ref, w_ref, b_ref):
    cond = cond_ref[...]
    s = (cond * jax.nn.sigmoid(cond)).astype(BF16)
    return _dot(s, w_ref[...].astype(BF16)) + b_ref[...]


def _mod_kernel(cond_ref, w_ref, b_ref, out_ref):
    out_ref[...] = _modulation_columns(cond_ref, w_ref, b_ref)


def _mod_specs(layer, n_rows, tile_n):
    return ([pl.BlockSpec((n_rows, D_MODEL), lambda j: (0, 0)),
             pl.BlockSpec((None, D_MODEL, tile_n), lambda j: (layer, 0, j)),
             pl.BlockSpec((None, 1, tile_n), lambda j: (layer, 0, j))],
            pl.BlockSpec((n_rows, tile_n), lambda j: (0, j)))


def _modulation(cond8, w_mod, b_mod, layer):
    n_rows = cond8.shape[0]
    in_specs, out_spec = _mod_specs(layer, n_rows, MOD_TILE_N)
    return pl.pallas_call(
        _mod_kernel,
        grid=(MOD_W // MOD_TILE_N,),
        in_specs=in_specs,
        out_specs=out_spec,
        out_shape=jax.ShapeDtypeStruct((n_rows, MOD_W), F32),
        compiler_params=_params(1),
        name="adaln_modulation",
    )(cond8, w_mod, b_mod)


def _decay_kernel(dec_ref, intra_ref, cross_ref, kdec_ref, gblk_ref):
    backward = pl.program_id(1) == 1
    log_g = jnp.log(jax.nn.sigmoid(dec_ref[...]))
    n = RET_BLOCK
    row = lax.broadcasted_iota(jnp.int32, (n, n), 0).astype(F32)
    col = lax.broadcasted_iota(jnp.int32, (n, n), 1).astype(F32)
    diff = jnp.where(backward, col - row, row - col)
    row1 = row[:, :LANES]
    cross_pow = jnp.where(backward, n - row1, row1 + 1.0)
    kdec_pow = jnp.where(backward, row1, n - 1.0 - row1)
    for h in range(RET_HEADS):
        lg = log_g[h:h + 1, :]
        lg_wide = jnp.concatenate([lg] * (n // LANES), axis=1)
        intra_ref[h] = jnp.where(diff >= 0, jnp.exp(jnp.maximum(diff, 0.0) * lg_wide), 0.0)
        cross_ref[h] = jnp.exp(cross_pow * lg)
        kdec_ref[h] = jnp.exp(kdec_pow * lg)
        gblk_ref[h] = jnp.exp(n * lg)


def _decay_tables(dec_rows):
    lead = lambda *tail: pl.BlockSpec((None, None, RET_HEADS) + tail,
                                      lambda l, d: (l, d, 0) + (0,) * len(tail))
    shape = lambda *tail: jax.ShapeDtypeStruct((DEPTH, 2, RET_HEADS) + tail, F32)
    return pl.pallas_call(
        _decay_kernel,
        grid=(DEPTH, 2),
        in_specs=[lead(LANES)],
        out_specs=[lead(RET_BLOCK, RET_BLOCK), lead(RET_BLOCK, LANES), lead(RET_BLOCK, LANES),
                   lead(1, LANES)],
        out_shape=[shape(RET_BLOCK, RET_BLOCK), shape(RET_BLOCK, LANES), shape(RET_BLOCK, LANES),
                   shape(1, LANES)],
        compiler_params=_params(2),
        name="retention_decay_tables",
    )(dec_rows)


def _decay_specs(layer, n_heads, index_map):
    mode = {"pipeline_mode": pl.Buffered(1)} if n_heads == RET_HEADS else {}
    tab = lambda *tail: pl.BlockSpec((None, 2, n_heads) + tail,
                                     lambda *idx: (layer, 0, index_map(*idx)) + (0,) * len(tail), **mode)
    return [tab(RET_BLOCK, RET_BLOCK), tab(RET_BLOCK, LANES), tab(RET_BLOCK, LANES), tab(1, LANES)]


def _modulated_norm(x, gain, shift, scale):
    y = x * lax.rsqrt(jnp.mean(x * x, axis=-1, keepdims=True) + NORM_EPS) * gain
    return y * (1.0 + scale) + shift


def _normed_input(x_ref, mod_ref, n1_ref):
    mod = mod_ref[...]
    h = _modulated_norm(x_ref[...], n1_ref[...], mod[:, 0:D_MODEL], mod[:, D_MODEL:2 * D_MODEL])
    return h.astype(BF16)


def _rope_chunk(xc, cos, sin_signed, first_half, nf):
    partner = jnp.where(first_half, pltpu.roll(xc, LANES - nf, 1), pltpu.roll(xc, nf, 1))
    return xc * cos + partner * sin_signed


def _rope(p, cos, sin_signed, nf):
    lane = lax.broadcasted_iota(jnp.int32, (p.shape[0], LANES), 1)
    first_half = (lane % (2 * nf)) < nf
    parts = [_rope_chunk(p[:, c:c + LANES], cos, sin_signed, first_half, nf)
             for c in range(0, p.shape[1], LANES)]
    return parts[0] if len(parts) == 1 else jnp.concatenate(parts, axis=1)


def _softmax_t(score_blocks, sink):
    m = sink
    for s in score_blocks:
        m = jnp.maximum(jnp.max(s, axis=0, keepdims=True), m)
    es = [jnp.exp(s - m) for s in score_blocks]
    denom = jnp.exp(sink - m)
    for e in es:
        denom = denom + jnp.sum(e, axis=0, keepdims=True)
    return [e.astype(BF16) for e in es], 1.0 / denom


def _store_heads(o_ref, rows, heads, o_t, inv):
    for pair in range(0, len(heads), 2):
        halves = [o_t[:, g * Q_BLOCK:(g + 1) * Q_BLOCK] * inv[g] for g in (pair, pair + 1)]
        both = jnp.transpose(jnp.concatenate(halves, axis=0))
        h0 = heads[pair]
        o_ref[rows, h0 * HEAD_DIM:(h0 + 2) * HEAD_DIM] = both.astype(BF16)


def _ctx_attention_scores(q_ref, k_ref, seq_len):
    scale = HEAD_DIM ** -0.5
    out = []
    for b in range(0, q_ref.shape[0], seq_len):
        for kh in range(KV_HEADS):
            for c in range(b, b + seq_len, Q_BLOCK):
                heads = [kh * GROUP + g for g in range(GROUP)]
                q4 = jnp.concatenate(
                    [q_ref[c:c + Q_BLOCK, h * HEAD_DIM:(h + 1) * HEAD_DIM] for h in heads], axis=0)
                keys = k_ref[b:b + seq_len, kh * HEAD_DIM:(kh + 1) * HEAD_DIM]
                out.append(((b, kh, c), _dot_nt(keys, q4) * scale))
    return out


def _ctx_attention_outputs(scores, b, v_ref, sink_ref, o_ref, layer, seq_len):
    v_t = jnp.transpose(v_ref[b:b + seq_len, :].astype(F32)).astype(BF16)
    for (sb, kh, c), s4 in scores:
        if sb != b:
            continue
        heads = [kh * GROUP + g for g in range(GROUP)]
        es, inv = [], []
        for g, h in enumerate(heads):
            (e,), r = _softmax_t([s4[:, g * Q_BLOCK:(g + 1) * Q_BLOCK]], sink_ref[layer, h])
            es.append(e)
            inv.append(r)
        o_t = _dot(v_t[kh * HEAD_DIM:(kh + 1) * HEAD_DIM, :], jnp.concatenate(es, axis=1))
        _store_heads(o_ref, slice(c, c + Q_BLOCK), heads, o_t, inv)


BAND_BLOCKS = 1 + 2 * WINDOW // Q_BLOCK


def _lat_attention_scores(j, q_ref, k_ref, ckb_ref):
    scale = HEAD_DIM ** -0.5
    n_blocks = k_ref.shape[0] // Q_BLOCK
    band = BAND_BLOCKS * Q_BLOCK
    first = jnp.clip(j - WINDOW // Q_BLOCK, 0, n_blocks - BAND_BLOCKS)
    start = pl.multiple_of(first * Q_BLOCK, Q_BLOCK)
    kpos = start + lax.broadcasted_iota(jnp.int32, (band, Q_BLOCK), 0)
    qpos = j * Q_BLOCK + lax.broadcasted_iota(jnp.int32, (band, Q_BLOCK), 1)
    valid = jnp.abs(qpos - kpos) <= WINDOW
    kb = k_ref[pl.ds(start, band), :]
    rows = pl.ds(pl.multiple_of(j * Q_BLOCK, Q_BLOCK), Q_BLOCK)
    scores = []
    for kh in range(KV_HEADS):
        hd = slice(kh * HEAD_DIM, (kh + 1) * HEAD_DIM)
        heads = [kh * GROUP + g for g in range(GROUP)]
        q4 = jnp.concatenate([q_ref[rows, h * HEAD_DIM:(h + 1) * HEAD_DIM] for h in heads], axis=0)
        scores.append((_dot_nt(kb[:, hd], q4) * scale, _dot_nt(ckb_ref[kh], q4) * scale))
    return first, valid, scores


def _lat_attention_outputs(j, first, valid, scores, vt_ref, cvt_ref, sink_ref, o_ref, layer):
    rows = pl.ds(pl.multiple_of(j * Q_BLOCK, Q_BLOCK), Q_BLOCK)
    for kh, (s_loc4, s_ctx4) in enumerate(scores):
        hd = slice(kh * HEAD_DIM, (kh + 1) * HEAD_DIM)
        heads = [kh * GROUP + g for g in range(GROUP)]
        e_loc, e_ctx, inv = [], [], []
        for g, h in enumerate(heads):
            cols = slice(g * Q_BLOCK, (g + 1) * Q_BLOCK)
            (el, ec), r = _softmax_t([jnp.where(valid, s_loc4[:, cols], NEG_INF), s_ctx4[:, cols]],
                                     sink_ref[layer, h])
            e_loc.append(el)
            e_ctx.append(ec)
            inv.append(r)
        e_loc4 = jnp.concatenate(e_loc, axis=1)
        o_t = _dot(cvt_ref[hd, :], jnp.concatenate(e_ctx, axis=1))
        for t in range(BAND_BLOCKS):
            o_t = o_t + _dot(vt_ref[first + t, hd, :], e_loc4[t * Q_BLOCK:(t + 1) * Q_BLOCK, :])
        _store_heads(o_ref, rows, heads, o_t, inv)


def _retention_products(rq_ref, rk_ref, rv_ref, intra_ref, kdec_ref, heads, final_state):
    n_blocks = rq_ref.shape[0] // RET_BLOCK
    blocks = [slice(i * RET_BLOCK, (i + 1) * RET_BLOCK) for i in range(n_blocks)]
    head_cols = {h: slice(h * RET_DK, (h + 1) * RET_DK) for h in heads}
    qk = {(h, i): _dot_nt(rq_ref[blocks[i], head_cols[h]], rk_ref[blocks[i], head_cols[h]].astype(BF16))
          for h in heads for i in range(n_blocks)}
    o_intra, kv = {}, {}
    for h in heads:
        for d in range(2):
            for i in range(n_blocks):
                v_i = rv_ref[blocks[i], head_cols[h]]
                o_intra[h, d, i] = _dot((qk[h, i] * intra_ref[d, h]).astype(BF16), v_i)
                last = i == (0 if d else n_blocks - 1)
                if final_state or not last:
                    k_i = rk_ref[blocks[i], head_cols[h]]
                    kv[h, d, i] = _dot_tn((k_i * kdec_ref[d, h]).astype(BF16), v_i)
    return o_intra, kv


def _retention_outputs(products, rq_ref, gf_ref, gb_ref, cross_ref, gblk_ref, gn_ref, ret_ref, acc_ref,
                       heads, initial_state, store_state):
    o_intra, kv = products
    n_blocks = rq_ref.shape[0] // RET_BLOCK
    blocks = [slice(i * RET_BLOCK, (i + 1) * RET_BLOCK) for i in range(n_blocks)]
    for h in heads:
        hs = slice(h * RET_DK, (h + 1) * RET_DK)
        for d, backward in enumerate((False, True)):
            g_ref = gb_ref if backward else gf_ref
            cross = cross_ref[d, h]
            g_block = gblk_ref[d, h]
            state = initial_state(backward, h)
            order = list(range(n_blocks))
            if backward:
                order.reverse()
            for i in order:
                o = o_intra[h, d, i]
                if state is not None:
                    o = o + _dot(rq_ref[blocks[i], hs], state.astype(BF16)) * cross
                if (h, d, i) in kv:
                    state = kv[h, d, i] if state is None else state * g_block + kv[h, d, i]
                mu = jnp.mean(o, axis=-1, keepdims=True)
                dev = o - mu
                var = jnp.mean(dev * dev, axis=-1, keepdims=True)
                y = dev * lax.rsqrt(var + NORM_EPS) * gn_ref[:, hs]
                gate = g_ref[blocks[i], hs]
                contrib = gate * jax.nn.sigmoid(gate) * y
                if backward:
                    ret_ref[blocks[i], hs] = (acc_ref[blocks[i], hs] + contrib).astype(BF16)
                else:
                    acc_ref[blocks[i], hs] = contrib
            if store_state is not None:
                store_state(backward, h, state)


def _lat_mix_kernel(sink_ref, x_ref, mod_ref, n1_ref, w_ref, cos_a, sin_a, cos_r, sin_r, ck_ref, cv_ref,
                    intra_ref, cross_ref, kdec_ref, gblk_ref, gn_ref, s0f_ref, s0b_ref,
                    attn_ref, ret_ref,
                    hb_s, q_s, k_s, vt_s, ckb_s, cvt_s, rq_s, rk_s, rv_s, gf_s, gb_s, acc_s, *, layer):
    seq_len = x_ref.shape[0]
    nf_a = HEAD_DIM // 4
    nf_r = RET_DK // 4

    hb_s[...] = _normed_input(x_ref, mod_ref, n1_ref)
    for r0 in range(0, seq_len, ROW_TILE):
        rows = slice(r0, r0 + ROW_TILE)
        hb = hb_s[rows, :]
        q_s[rows, :] = _rope(_dot(hb, w_ref[:, OFF_Q:OFF_Q + ATTN_W]), cos_a[rows, :], sin_a[rows, :],
                             nf_a).astype(BF16)
        k_s[rows, :] = _rope(_dot(hb, w_ref[:, OFF_K:OFF_K + KV_W]), cos_a[rows, :], sin_a[rows, :],
                             nf_a).astype(BF16)
        v = _dot(hb, w_ref[:, OFF_V:OFF_V + KV_W])
        for t in range(ROW_TILE // Q_BLOCK):
            vt_s[r0 // Q_BLOCK + t] = jnp.transpose(v[t * Q_BLOCK:(t + 1) * Q_BLOCK, :]).astype(BF16)
    cvt_s[...] = jnp.concatenate([cv_ref[0, 0, kh] for kh in range(KV_HEADS)], axis=0).astype(BF16)
    ck = jnp.transpose(jnp.concatenate([ck_ref[0, 0, kh] for kh in range(KV_HEADS)], axis=0))
    for kh in range(KV_HEADS):
        ckb_s[kh] = ck[:, kh * HEAD_DIM:(kh + 1) * HEAD_DIM].astype(BF16)

    def query_blocks(i, carry):
        js = [i * LAT_BLOCKS_PER_STEP + n for n in range(LAT_BLOCKS_PER_STEP)]
        attention = [_lat_attention_scores(j, q_s, k_s, ckb_s) for j in js]
        n_rows = LAT_BLOCKS_PER_STEP * Q_BLOCK
        rows = pl.ds(pl.multiple_of(i * n_rows, n_rows), n_rows)
        hb = hb_s[rows, :]
        cos = cos_r[rows, :]
        sin = sin_r[rows, :]
        rq_s[rows, :] = _rope(_dot(hb, w_ref[:, OFF_RQ:OFF_RQ + RET_W]), cos, sin, nf_r).astype(BF16)
        rk_s[rows, :] = _rope(_dot(hb, w_ref[:, OFF_RK:OFF_RK + RET_W]), cos, sin, nf_r) * (RET_DK ** -0.5)
        rv_s[rows, :] = _dot(hb, w_ref[:, OFF_RV:OFF_RV + RET_VW]).astype(BF16)
        gf_s[rows, :] = _dot(hb, w_ref[:, OFF_RGF:OFF_RGF + RET_VW])
        gb_s[rows, :] = _dot(hb, w_ref[:, OFF_RGB:OFF_RGB + RET_VW])
        for j, (first, valid, scores) in zip(js, attention):
            _lat_attention_outputs(j, first, valid, scores, vt_s, cvt_s, sink_ref, attn_ref, layer)
        return carry

    lax.fori_loop(0, seq_len // (LAT_BLOCKS_PER_STEP * Q_BLOCK), query_blocks, 0)

    products = _retention_products(rq_s, rk_s, rv_s, intra_ref, kdec_ref, [0], False)
    for h in range(RET_HEADS):
        following = (_retention_products(rq_s, rk_s, rv_s, intra_ref, kdec_ref, [h + 1], False)
                     if h + 1 < RET_HEADS else None)
        _retention_outputs(products, rq_s, gf_s, gb_s, cross_ref, gblk_ref, gn_ref, ret_ref, acc_s, [h],
                           lambda backward, h: (s0b_ref if backward else s0f_ref)[0, 0, h], None)
        products = following


def _latent_mixer(x, mod, n1, w_in, tables, cache_k, cache_v, sink, decay, gn, states0, layer, seq_len):
    n_tok = x.shape[0]
    past_len = cache_k.shape[4]
    seq = lambda w, **kw: pl.BlockSpec((seq_len, w), lambda b: (b, 0), **kw)
    per_batch = lambda dims: pl.BlockSpec((1, 1) + dims, lambda b: (b, layer) + (0,) * len(dims))
    cache = per_batch((KV_HEADS, HEAD_DIM, past_len))
    state = per_batch((RET_HEADS, RET_DK, RET_DV))
    n_blocks = seq_len // Q_BLOCK
    scratch = [pltpu.VMEM((seq_len, D_MODEL), BF16),
               pltpu.VMEM((seq_len, ATTN_W), BF16), pltpu.VMEM((seq_len, KV_W), BF16),
               pltpu.VMEM((n_blocks, KV_W, Q_BLOCK), BF16),
               pltpu.VMEM((KV_HEADS, past_len, HEAD_DIM), BF16),
               pltpu.VMEM((KV_W, past_len), BF16),
               pltpu.VMEM((seq_len, RET_W), BF16), pltpu.VMEM((seq_len, RET_W), F32),
               pltpu.VMEM((seq_len, RET_VW), BF16),
               pltpu.VMEM((seq_len, RET_VW), F32), pltpu.VMEM((seq_len, RET_VW), F32),
               pltpu.VMEM((seq_len, RET_VW), F32)]
    return pl.pallas_call(
        functools.partial(_lat_mix_kernel, layer=layer),
        grid=(n_tok // seq_len,),
        in_specs=[pl.BlockSpec(memory_space=pltpu.SMEM), seq(D_MODEL),
                  _mod_spec(1),
                  _layer_spec((1, D_MODEL), layer), _const_spec((D_MODEL, IN_WIDTH)),
                  *[_const_spec((seq_len, LANES))] * 4, cache, cache,
                  *_decay_specs(layer, RET_HEADS, lambda b: 0), _layer_spec((1, RET_VW), layer),
                  state, state],
        out_specs=[seq(ATTN_W), seq(RET_VW)],
        out_shape=[jax.ShapeDtypeStruct((n_tok, ATTN_W), BF16), jax.ShapeDtypeStruct((n_tok, RET_VW), BF16)],
        scratch_shapes=scratch,
        compiler_params=_params(1),
        name="latent_mixer",
    )(sink, x, mod, n1, w_in, *tables, cache_k, cache_v, *decay, gn, *states0)


N_CTX_MIX_OUT = 7
N_CTX_MIX_SCRATCH = 9


def _ctx_mix_kernel(*refs, layer, seq_len):
    n_in = len(refs) - N_CTX_MIX_OUT - N_CTX_MIX_SCRATCH
    (sink_ref, x_ref, mod_ref, n1_ref, w_ref, intra_ref, cross_ref, kdec_ref, gblk_ref,
     gn_ref) = refs[:10]
    attn_ref, ret_ref, g_ref, kc_ref, vc_ref, sf_ref, sb_ref = refs[n_in:n_in + N_CTX_MIX_OUT]
    q_s, k_s, v_s, rq_s, rk_s, rv_s, gf_s, gb_s, acc_s = refs[n_in + N_CTX_MIX_OUT:]
    hb = _normed_input(x_ref, mod_ref, n1_ref)

    def proj(off, width):
        return _dot(hb, w_ref[:, off:off + width])

    batch_rows = [slice(b, b + seq_len) for b in range(0, x_ref.shape[0], seq_len)]
    q_s[...] = proj(OFF_Q, ATTN_W).astype(BF16)
    k = proj(OFF_K, KV_W)
    v = proj(OFF_V, KV_W)
    k_s[...] = k.astype(BF16)
    v_s[...] = v.astype(BF16)
    slot = _zero_other_layers((kc_ref, vc_ref), layer)
    for b, rows in enumerate(batch_rows):
        for kh in range(KV_HEADS):
            kc_ref[b, slot, kh] = k[rows, kh * HEAD_DIM:(kh + 1) * HEAD_DIM]
            vc_ref[b, slot, kh] = v[rows, kh * HEAD_DIM:(kh + 1) * HEAD_DIM]
    scores = _ctx_attention_scores(q_s, k_s, seq_len)
    rq_s[...] = proj(OFF_RQ, RET_W).astype(BF16)
    rk_s[...] = proj(OFF_RK, RET_W) * (RET_DK ** -0.5)
    rv_s[...] = proj(OFF_RV, RET_VW).astype(BF16)
    products = [_retention_products(rq_s.at[rows], rk_s.at[rows], rv_s.at[rows], intra_ref, kdec_ref,
                                    range(RET_HEADS), True) for rows in batch_rows]
    gf_s[...] = proj(OFF_RGF, RET_VW)
    gb_s[...] = proj(OFF_RGB, RET_VW)
    state_slot = _zero_other_layers((sf_ref, sb_ref), layer)

    def attention_outputs(b):
        _ctx_attention_outputs(scores, batch_rows[b].start, v_s, sink_ref, attn_ref, layer, seq_len)

    def retention_outputs(b):
        rows = batch_rows[b]

        def store_state(backward, h, state):
            (sb_ref if backward else sf_ref)[b, state_slot, h] = state

        _retention_outputs(products[b], rq_s.at[rows], gf_s.at[rows], gb_s.at[rows], cross_ref, gblk_ref,
                           gn_ref, ret_ref.at[rows], acc_s.at[rows], range(RET_HEADS),
                           lambda backward, h: None, store_state)

    stages = ([functools.partial(attention_outputs, b) for b in range(len(batch_rows))]
              + [functools.partial(retention_outputs, b) for b in range(len(batch_rows))])
    chunk = 2 * D_MODEL // len(stages)
    for n, stage in enumerate(stages):
        g_ref[:, n * chunk:(n + 1) * chunk] = proj(OFF_GA + n * chunk, chunk)
        stage()


def _context_mixer(x, mod, n1, w_in, sink, decay, gn, layer, seq_len, prev_outputs):
    n_tok = x.shape[0]
    n_batch = n_tok // seq_len
    per_tile = ROW_TILE // seq_len
    row = lambda w: pl.BlockSpec((ROW_TILE, w), lambda i: (i, 0))
    in_specs = [pl.BlockSpec(memory_space=pltpu.SMEM), row(D_MODEL), _mod_spec(None),
                _layer_spec((1, D_MODEL), layer), _const_spec((D_MODEL, IN_WIDTH)),
                *_decay_specs(layer, RET_HEADS, lambda i: 0), _layer_spec((1, RET_VW), layer)]
    args = [sink, x, mod, n1, w_in, *decay, gn]
    out_specs = [row(ATTN_W), row(RET_VW), row(2 * D_MODEL)]
    out_shape = [jax.ShapeDtypeStruct((n_tok, ATTN_W), BF16), jax.ShapeDtypeStruct((n_tok, RET_VW), BF16),
                 jax.ShapeDtypeStruct((n_tok, 2 * D_MODEL), F32)]
    cache_dims = (KV_HEADS, seq_len, HEAD_DIM)
    state_dims = (RET_HEADS, RET_DK, RET_DV)
    aliases = {}
    for idx, dims in enumerate((cache_dims, cache_dims, state_dims, state_dims)):
        if prev_outputs is None:
            assert layer == 0
            out_specs.append(pl.BlockSpec((per_tile, DEPTH) + dims, lambda i: (i, 0, 0, 0, 0)))
        else:
            aliases[len(args)] = len(out_shape)
            in_specs.append(pl.BlockSpec(memory_space=pl.ANY))
            args.append(prev_outputs[idx])
            out_specs.append(pl.BlockSpec((per_tile, 1) + dims, lambda i: (i, layer, 0, 0, 0)))
        out_shape.append(jax.ShapeDtypeStruct((n_batch, DEPTH) + dims, F32))
    assert len(out_shape) == N_CTX_MIX_OUT
    scratch = [pltpu.VMEM((ROW_TILE, ATTN_W), BF16), pltpu.VMEM((ROW_TILE, KV_W), BF16),
               pltpu.VMEM((ROW_TILE, KV_W), BF16), pltpu.VMEM((ROW_TILE, RET_W), BF16),
               pltpu.VMEM((ROW_TILE, RET_W), F32), pltpu.VMEM((ROW_TILE, RET_VW), BF16),
               pltpu.VMEM((ROW_TILE, RET_VW), F32), pltpu.VMEM((ROW_TILE, RET_VW), F32),
               pltpu.VMEM((ROW_TILE, RET_VW), F32)]
    assert len(scratch) == N_CTX_MIX_SCRATCH
    return pl.pallas_call(
        functools.partial(_ctx_mix_kernel, layer=layer, seq_len=seq_len),
        grid=(n_tok // ROW_TILE,),
        in_specs=in_specs,
        out_specs=out_specs,
        out_shape=out_shape,
        scratch_shapes=scratch,
        input_output_aliases=aliases,
        compiler_params=_params(1),
        name="context_mixer",
    )(*args)


def _post_kernel(*refs, final, n_cast, next_mod, own_gates):
    (attn_ref, ret_ref, g0_ref, g1_ref, x_ref, mod_ref, n2_ref,
     wba_ref, wbr_ref, wo_ref, w1_ref, w2_ref) = refs[:12]
    n_side_in = n_cast + (3 if next_mod else 0)
    n_in = 12 + (1 if final else 0) + n_side_in
    fn_ref = refs[12] if final else None
    out_ref = refs[n_in]
    side_in = refs[n_in - n_side_in:n_in]
    side_out = refs[n_in + 1:]
    for src, dst in zip(side_in[:n_cast], side_out[:n_cast]):
        dst[...] = src[...].astype(BF16)
    if next_mod:
        side_out[n_cast][...] = _modulation_columns(*side_in[n_cast:])
    mod = mod_ref[...]
    part = lambda k: mod[:, k * D_MODEL:(k + 1) * D_MODEL]
    if own_gates:
        hb = _normed_input(x_ref, mod_ref, g0_ref)
        gate_a = _dot(hb, g1_ref[:, OFF_GA:OFF_GA + D_MODEL])
        gate_r = _dot(hb, g1_ref[:, OFF_GA + D_MODEL:OFF_GA + 2 * D_MODEL])
    else:
        gate_a = g0_ref[...]
        gate_r = g1_ref[...]
    merged = (jax.nn.sigmoid(gate_a) * _dot(attn_ref[...], wba_ref[...])
              + jax.nn.sigmoid(gate_r) * _dot(ret_ref[...], wbr_ref[...]))
    x = x_ref[...] + part(2) * _dot(merged.astype(BF16), wo_ref[...])
    h2 = _modulated_norm(x, n2_ref[...], part(3), part(4)).astype(BF16)
    ff = None
    for c in range(0, D_FF, FF_CHUNK):
        a = jnp.maximum(_dot(h2, w1_ref[:, c:c + FF_CHUNK]), 0.0)
        t = _dot((a * a).astype(BF16), w2_ref[c:c + FF_CHUNK, :])
        ff = t if ff is None else ff + t
    x = x + part(5) * ff
    if final:
        x = x * lax.rsqrt(jnp.mean(x * x, axis=-1, keepdims=True) + NORM_EPS) * fn_ref[...]
    out_ref[...] = x


def _post(attn, ret, gates, x, mod, n2, weights, layer, seq_len, latent, final_gain, cast_next=(),
          mod_next=None):
    n_tok = x.shape[0]
    n_steps = n_tok // ROW_TILE
    tiles_per_batch = seq_len // ROW_TILE
    final = final_gain is not None
    own_gates = isinstance(gates, tuple)
    row = lambda w, c=0: pl.BlockSpec((ROW_TILE, w), lambda i, c=c: (i, c))
    if own_gates:
        gate_specs = [_layer_spec((1, D_MODEL), layer), _const_spec((D_MODEL, IN_WIDTH))]
        gate_args = list(gates)
    else:
        gate_specs = [row(D_MODEL, 0), row(D_MODEL, 1)]
        gate_args = [gates, gates]
    in_specs = [
        row(ATTN_W), row(RET_VW), *gate_specs, row(D_MODEL),
        _mod_spec(tiles_per_batch if latent else None),
        _layer_spec((1, D_MODEL), layer),
        *[_const_spec(w.shape) for w in weights],
    ]
    args = [attn, ret, *gate_args, x, mod, n2, *weights]
    if final:
        in_specs.append(_const_spec((1, D_MODEL)))
        args.append(final_gain)
    out_specs = [row(D_MODEL)]
    out_shape = [jax.ShapeDtypeStruct((n_tok, D_MODEL), F32)]
    for w in cast_next:
        _, k, n = w.shape
        rows = k // n_steps
        assert rows * n_steps == k and rows % 16 == 0
        in_specs.append(pl.BlockSpec((None, rows, n), lambda i: (layer + 1, i, 0)))
        args.append(w)
        out_specs.append(pl.BlockSpec((rows, n), lambda i: (i, 0)))
        out_shape.append(jax.ShapeDtypeStruct((k, n), BF16))
    if mod_next is not None:
        n_rows = mod_next[0].shape[0]
        mod_in_specs, mod_out_spec = _mod_specs(layer + 1, n_rows, MOD_W // n_steps)
        in_specs += mod_in_specs
        args += list(mod_next)
        out_specs.append(mod_out_spec)
        out_shape.append(jax.ShapeDtypeStruct((n_rows, MOD_W), F32))
    return pl.pallas_call(
        functools.partial(_post_kernel, final=final, n_cast=len(cast_next),
                          next_mod=mod_next is not None, own_gates=own_gates),
        grid=(n_steps,),
        in_specs=in_specs,
        out_specs=out_specs,
        out_shape=out_shape,
        compiler_params=_params(1),
        name="post_final" if final else "post",
    )(*args)


def _rope_tables(seq_len):
    t = np.arange(seq_len)
    rows = (t // GRID_W).astype(np.float64)
    cols = (t % GRID_W).astype(np.float64)

    def table(head_dim):
        half = head_dim // 2
        nf = half // 2
        inv = ROPE_BASE ** (-np.arange(nf, dtype=np.float64) / nf)
        lane = np.arange(LANES) % head_dim
        pos = np.where((lane < half)[None, :], rows[:, None], cols[:, None])
        ang = pos * inv[lane % nf][None, :]
        sign = np.where((lane % half) < nf, -1.0, 1.0)[None, :]
        return jnp.asarray(np.cos(ang), F32), jnp.asarray(np.sin(ang) * sign, F32)

    cos_a, sin_a = table(HEAD_DIM)
    cos_r, sin_r = table(RET_DK)
    return cos_a, sin_a, cos_r, sin_r


def kernel(x_prompt, x_sample, cache_k, cache_v, state_ret_fwd, state_ret_bwd, c, c_ctx, norm1_g, norm2_g, final_norm_g, w_mod, b_mod, w_in, attn_sink, ret_decay_fwd, ret_decay_bwd, ret_gn_g, w_branch_attn, w_branch_ret, w_out, w_ff1, w_ff2):
    n_ctx, ctx_len, _ = x_prompt.shape
    n_lat, lat_len, _ = x_sample.shape
    xp = x_prompt.reshape(n_ctx * ctx_len, D_MODEL)
    xs = x_sample.reshape(n_lat * lat_len, D_MODEL)

    cond = jnp.concatenate([c_ctx[None, :], c], axis=0)
    cond8 = jnp.pad(cond, ((0, -cond.shape[0] % 8), (0, 0)))
    mod_inputs = (cond8, w_mod, b_mod.reshape(DEPTH, 1, MOD_W))
    mod = _modulation(*mod_inputs, 0).reshape(cond8.shape[0], 1, MOD_W)
    cache_kt = jnp.swapaxes(cache_k, 3, 4)
    cache_vt = jnp.swapaxes(cache_v, 3, 4)
    dec_rows = jnp.broadcast_to(jnp.stack([ret_decay_fwd, ret_decay_bwd], axis=1)[..., None],
                                (DEPTH, 2, RET_HEADS, LANES))
    decay = _decay_tables(dec_rows)
    tables = _rope_tables(lat_len)
    final_gain = final_norm_g.reshape(1, D_MODEL)

    n1 = norm1_g.reshape(DEPTH, 1, D_MODEL)
    n2 = norm2_g.reshape(DEPTH, 1, D_MODEL)
    gn = ret_gn_g.reshape(DEPTH, 1, RET_VW)
    f32_weights = (w_in, w_branch_attn, w_branch_ret, w_out, w_ff1, w_ff2)
    w_in_b, *post_w = [w[0].astype(BF16) for w in f32_weights]

    ctx_outputs = None
    for l in range(DEPTH):
        last = l == DEPTH - 1
        attn, ret, gates, *ctx_outputs = _context_mixer(xp, mod, n1, w_in_b, attn_sink, decay, gn, l,
                                                        ctx_len, ctx_outputs)
        xp, *next_layer = _post(attn, ret, gates, xp, mod, n2, post_w, l, ctx_len, False,
                                final_gain if last else None, () if last else f32_weights,
                                None if last else mod_inputs)
        attn, ret = _latent_mixer(xs, mod, n1, w_in_b, tables, cache_kt, cache_vt, attn_sink, decay, gn,
                                  (state_ret_fwd, state_ret_bwd), l, lat_len)
        (xs,) = _post(attn, ret, (n1, w_in_b), xs, mod, n2, post_w, l, lat_len, True,
                      final_gain if last else None)
        if not last:
            w_in_b, *post_w, mod = next_layer
            mod = mod.reshape(cond8.shape[0], 1, MOD_W)

    return (xp.reshape(n_ctx, ctx_len, D_MODEL), xs.reshape(n_lat, lat_len, D_MODEL), *ctx_outputs)
```

```python
import functools

import jax
import jax.numpy as jnp
import numpy as np
from jax import lax
from jax.experimental import pallas as pl
from jax.experimental.pallas import tpu as pltpu

D_MODEL = 1024
DEPTH = 4
GRID_W = 64
N_HEADS = 8
HEAD_DIM = 64
KV_HEADS = 2
GROUP = N_HEADS // KV_HEADS
WINDOW = 128
Q_BLOCK = 128
RET_HEADS = 4
RET_DK = 128
RET_DV = 128
ATTN_W = N_HEADS * HEAD_DIM
KV_W = KV_HEADS * HEAD_DIM
RET_W = RET_HEADS * RET_DK
RET_VW = RET_HEADS * RET_DV
GATE_W = 2 * RET_VW + 2 * D_MODEL
IN_WIDTH = ATTN_W + 2 * KV_W + 2 * RET_W + RET_VW + GATE_W
D_FF = 4 * D_MODEL
MOD_W = 6 * D_MODEL
ROPE_BASE = 10000.0
NORM_EPS = 1e-6
NEG_INF = -1e30

OFF_Q = 0
OFF_K = OFF_Q + ATTN_W
OFF_V = OFF_K + KV_W
OFF_RQ = OFF_V + KV_W
OFF_RK = OFF_RQ + RET_W
OFF_RV = OFF_RK + RET_W
OFF_RGF = OFF_RV + RET_VW
OFF_RGB = OFF_RGF + RET_VW
OFF_GA = OFF_RGB + RET_VW

LANES = 128
ROW_TILE = 512
RET_BLOCK = 256
LAT_BLOCKS_PER_STEP = 2
MOD_TILE_N = 1536
FF_CHUNK = 1024
VMEM_LIMIT = 58 * 1024 * 1024

BF16 = jnp.bfloat16
F32 = jnp.float32


def _params(n_axes):
    return pltpu.CompilerParams(dimension_semantics=("arbitrary",) * n_axes,
                                vmem_limit_bytes=VMEM_LIMIT)


def _const_spec(shape):
    nd = len(shape)
    return pl.BlockSpec(shape, lambda *_: (0,) * nd, pipeline_mode=pl.Buffered(1))


def _layer_spec(shape, layer):
    nd = len(shape)
    return pl.BlockSpec((None,) + tuple(shape), lambda *_: (layer,) + (0,) * nd,
                        pipeline_mode=pl.Buffered(1))


def _mod_spec(tiles_per_batch):
    if tiles_per_batch is None:
        return pl.BlockSpec((None, 1, MOD_W), lambda *_: (0, 0, 0))
    return pl.BlockSpec((None, 1, MOD_W), lambda i: (1 + i // tiles_per_batch, 0, 0))


def _zero_other_layers(out_refs, layer):
    n_slots = out_refs[0].shape[1]
    if n_slots == 1:
        return 0
    for ref in out_refs:
        for other in range(n_slots):
            if other != layer:
                ref[:, other] = jnp.zeros((ref.shape[0],) + ref.shape[2:], ref.dtype)
    return layer


def _dot(a, b):
    return jnp.dot(a, b, preferred_element_type=F32)


def _dot_nt(a, b):
    return lax.dot_general(a, b, (((1,), (1,)), ((), ())), preferred_element_type=F32)


def _dot_tn(a, b):
    return lax.dot_general(a, b, (((0,), (0,)), ((), ())), preferred_element_type=F32)


def _modulation_columns(cond_ref, w_ref, b_ref):
    cond = cond_ref[...]
    s = (cond * jax.nn.sigmoid(cond)).astype(BF16)
    return _dot(s, w_ref[...].astype(BF16)) + b_ref[...]


def _mod_kernel(cond_ref, w_ref, b_ref, out_ref):
    out_ref[...] = _modulation_columns(cond_ref, w_ref, b_ref)


def _mod_specs(layer, n_rows, tile_n):
    return ([pl.BlockSpec((n_rows, D_MODEL), lambda j: (0, 0)),
             pl.BlockSpec((None, D_MODEL, tile_n), lambda j: (layer, 0, j)),
             pl.BlockSpec((None, 1, tile_n), lambda j: (layer, 0, j))],
            pl.BlockSpec((n_rows, tile_n), lambda j: (0, j)))


def _modulation(cond8, w_mod, b_mod, layer):
    n_rows = cond8.shape[0]
    in_specs, out_spec = _mod_specs(layer, n_rows, MOD_TILE_N)
    return pl.pallas_call(
        _mod_kernel,
        grid=(MOD_W // MOD_TILE_N,),
        in_specs=in_specs,
        out_specs=out_spec,
        out_shape=jax.ShapeDtypeStruct((n_rows, MOD_W), F32),
        compiler_params=_params(1),
        name="adaln_modulation",
    )(cond8, w_mod, b_mod)


def _decay_kernel(dec_ref, intra_ref, cross_ref, kdec_ref, gblk_ref):
    backward = pl.program_id(1) == 1
    log_g = jnp.log(jax.nn.sigmoid(dec_ref[...]))
    n = RET_BLOCK
    row = lax.broadcasted_iota(jnp.int32, (n, n), 0).astype(F32)
    col = lax.broadcasted_iota(jnp.int32, (n, n), 1).astype(F32)
    diff = jnp.where(backward, col - row, row - col)
    row1 = row[:, :LANES]
    cross_pow = jnp.where(backward, n - row1, row1 + 1.0)
    kdec_pow = jnp.where(backward, row1, n - 1.0 - row1)
    for h in range(RET_HEADS):
        lg = log_g[h:h + 1, :]
        lg_wide = jnp.concatenate([lg] * (n // LANES), axis=1)
        intra_ref[h] = jnp.where(diff >= 0, jnp.exp(jnp.maximum(diff, 0.0) * lg_wide), 0.0)
        cross_ref[h] = jnp.exp(cross_pow * lg)
        kdec_ref[h] = jnp.exp(kdec_pow * lg)
        gblk_ref[h] = jnp.exp(n * lg)


def _decay_tables(dec_rows):
    lead = lambda *tail: pl.BlockSpec((None, None, RET_HEADS) + tail,
                                      lambda l, d: (l, d, 0) + (0,) * len(tail))
    shape = lambda *tail: jax.ShapeDtypeStruct((DEPTH, 2, RET_HEADS) + tail, F32)
    return pl.pallas_call(
        _decay_kernel,
        grid=(DEPTH, 2),
        in_specs=[lead(LANES)],
        out_specs=[lead(RET_BLOCK, RET_BLOCK), lead(RET_BLOCK, LANES), lead(RET_BLOCK, LANES),
                   lead(1, LANES)],
        out_shape=[shape(RET_BLOCK, RET_BLOCK), shape(RET_BLOCK, LANES), shape(RET_BLOCK, LANES),
                   shape(1, LANES)],
        compiler_params=_params(2),
        name="retention_decay_tables",
    )(dec_rows)


def _decay_specs(layer, n_heads, index_map):
    mode = {"pipeline_mode": pl.Buffered(1)} if n_heads == RET_HEADS else {}
    tab = lambda *tail: pl.BlockSpec((None, 2, n_heads) + tail,
                                     lambda *idx: (layer, 0, index_map(*idx)) + (0,) * len(tail), **mode)
    return [tab(RET_BLOCK, RET_BLOCK), tab(RET_BLOCK, LANES), tab(RET_BLOCK, LANES), tab(1, LANES)]


def _modulated_norm(x, gain, shift, scale):
    y = x * lax.rsqrt(jnp.mean(x * x, axis=-1, keepdims=True) + NORM_EPS) * gain
    return y * (1.0 + scale) + shift


def _normed_input(x_ref, mod_ref, n1_ref):
    mod = mod_ref[...]
    h = _modulated_norm(x_ref[...], n1_ref[...], mod[:, 0:D_MODEL], mod[:, D_MODEL:2 * D_MODEL])
    return h.astype(BF16)


def _rope_chunk(xc, cos, sin_signed, first_half, nf):
    partner = jnp.where(first_half, pltpu.roll(xc, LANES - nf, 1), pltpu.roll(xc, nf, 1))
    return xc * cos + partner * sin_signed


def _rope(p, cos, sin_signed, nf):
    lane = lax.broadcasted_iota(jnp.int32, (p.shape[0], LANES), 1)
    first_half = (lane % (2 * nf)) < nf
    parts = [_rope_chunk(p[:, c:c + LANES], cos, sin_signed, first_half, nf)
             for c in range(0, p.shape[1], LANES)]
    return parts[0] if len(parts) == 1 else jnp.concatenate(parts, axis=1)


def _softmax_t(score_blocks, sink):
    m = sink
    for s in score_blocks:
        m = jnp.maximum(jnp.max(s, axis=0, keepdims=True), m)
    es = [jnp.exp(s - m) for s in score_blocks]
    denom = jnp.exp(sink - m)
    for e in es:
        denom = denom + jnp.sum(e, axis=0, keepdims=True)
    return [e.astype(BF16) for e in es], 1.0 / denom


def _store_heads(o_ref, rows, heads, o_t, inv):
    for pair in range(0, len(heads), 2):
        halves = [o_t[:, g * Q_BLOCK:(g + 1) * Q_BLOCK] * inv[g] for g in (pair, pair + 1)]
        both = jnp.transpose(jnp.concatenate(halves, axis=0))
        h0 = heads[pair]
        o_ref[rows, h0 * HEAD_DIM:(h0 + 2) * HEAD_DIM] = both.astype(BF16)


def _ctx_attention_scores(q_ref, k_ref, seq_len):
    scale = HEAD_DIM ** -0.5
    out = []
    for b in range(0, q_ref.shape[0], seq_len):
        for kh in range(KV_HEADS):
            for c in range(b, b + seq_len, Q_BLOCK):
                heads = [kh * GROUP + g for g in range(GROUP)]
                q4 = jnp.concatenate(
                    [q_ref[c:c + Q_BLOCK, h * HEAD_DIM:(h + 1) * HEAD_DIM] for h in heads], axis=0)
                keys = k_ref[b:b + seq_len, kh * HEAD_DIM:(kh + 1) * HEAD_DIM]
                out.append(((b, kh, c), _dot_nt(keys, q4) * scale))
    return out


def _ctx_attention_outputs(scores, b, v_ref, sink_ref, o_ref, layer, seq_len):
    v_t = jnp.transpose(v_ref[b:b + seq_len, :].astype(F32)).astype(BF16)
    for (sb, kh, c), s4 in scores:
        if sb != b:
            continue
        heads = [kh * GROUP + g for g in range(GROUP)]
        es, inv = [], []
        for g, h in enumerate(heads):
            (e,), r = _softmax_t([s4[:, g * Q_BLOCK:(g + 1) * Q_BLOCK]], sink_ref[layer, h])
            es.append(e)
            inv.append(r)
        o_t = _dot(v_t[kh * HEAD_DIM:(kh + 1) * HEAD_DIM, :], jnp.concatenate(es, axis=1))
        _store_heads(o_ref, slice(c, c + Q_BLOCK), heads, o_t, inv)


def _lat_attention_scores(j, q_ref, k_ref, ckb_ref):
    scale = HEAD_DIM ** -0.5
    n_blocks = k_ref.shape[0] // Q_BLOCK
    reach = WINDOW // Q_BLOCK
    first = max(j - reach, 0)
    band = (min(j + reach, n_blocks - 1) - first + 1) * Q_BLOCK
    start = first * Q_BLOCK
    kpos = start + lax.broadcasted_iota(jnp.int32, (band, Q_BLOCK), 0)
    qpos = j * Q_BLOCK + lax.broadcasted_iota(jnp.int32, (band, Q_BLOCK), 1)
    valid = jnp.abs(qpos - kpos) <= WINDOW
    kb = k_ref[start:start + band, :]
    rows = slice(j * Q_BLOCK, (j + 1) * Q_BLOCK)
    scores = []
    for kh in range(KV_HEADS):
        hd = slice(kh * HEAD_DIM, (kh + 1) * HEAD_DIM)
        heads = [kh * GROUP + g for g in range(GROUP)]
        q4 = jnp.concatenate([q_ref[rows, h * HEAD_DIM:(h + 1) * HEAD_DIM] for h in heads], axis=0)
        scores.append((_dot_nt(kb[:, hd], q4) * scale, _dot_nt(ckb_ref[kh], q4) * scale))
    return first, valid, scores


def _lat_attention_outputs(j, first, valid, scores, vt_ref, cvt_ref, sink_ref, o_ref, layer):
    rows = slice(j * Q_BLOCK, (j + 1) * Q_BLOCK)
    for kh, (s_loc4, s_ctx4) in enumerate(scores):
        hd = slice(kh * HEAD_DIM, (kh + 1) * HEAD_DIM)
        heads = [kh * GROUP + g for g in range(GROUP)]
        e_loc, e_ctx, inv = [], [], []
        for g, h in enumerate(heads):
            cols = slice(g * Q_BLOCK, (g + 1) * Q_BLOCK)
            (el, ec), r = _softmax_t([jnp.where(valid, s_loc4[:, cols], NEG_INF), s_ctx4[:, cols]],
                                     sink_ref[layer, h])
            e_loc.append(el)
            e_ctx.append(ec)
            inv.append(r)
        e_loc4 = jnp.concatenate(e_loc, axis=1)
        o_t = _dot(cvt_ref[hd, :], jnp.concatenate(e_ctx, axis=1))
        for t in range(e_loc4.shape[0] // Q_BLOCK):
            o_t = o_t + _dot(vt_ref[first + t, hd, :], e_loc4[t * Q_BLOCK:(t + 1) * Q_BLOCK, :])
        _store_heads(o_ref, rows, heads, o_t, inv)


def _retention_products(rq_ref, rk_ref, rv_ref, intra_ref, kdec_ref, heads, final_state):
    n_blocks = rq_ref.shape[0] // RET_BLOCK
    blocks = [slice(i * RET_BLOCK, (i + 1) * RET_BLOCK) for i in range(n_blocks)]
    head_cols = {h: slice(h * RET_DK, (h + 1) * RET_DK) for h in heads}
    qk = {(h, i): _dot_nt(rq_ref[blocks[i], head_cols[h]], rk_ref[blocks[i], head_cols[h]].astype(BF16))
          for h in heads for i in range(n_blocks)}
    o_intra, kv = {}, {}
    for h in heads:
        for d in range(2):
            for i in range(n_blocks):
                v_i = rv_ref[blocks[i], head_cols[h]]
                o_intra[h, d, i] = _dot((qk[h, i] * intra_ref[d, h]).astype(BF16), v_i)
                last = i == (0 if d else n_blocks - 1)
                if final_state or not last:
                    k_i = rk_ref[blocks[i], head_cols[h]]
                    kv[h, d, i] = _dot_tn((k_i * kdec_ref[d, h]).astype(BF16), v_i)
    return o_intra, kv


def _retention_outputs(products, rq_ref, gf_ref, gb_ref, cross_ref, gblk_ref, gn_ref, ret_ref, acc_ref,
                       heads, initial_state, store_state):
    o_intra, kv = products
    n_blocks = rq_ref.shape[0] // RET_BLOCK
    blocks = [slice(i * RET_BLOCK, (i + 1) * RET_BLOCK) for i in range(n_blocks)]
    for h in heads:
        hs = slice(h * RET_DK, (h + 1) * RET_DK)
        for d, backward in enumerate((False, True)):
            g_ref = gb_ref if backward else gf_ref
            cross = cross_ref[d, h]
            g_block = gblk_ref[d, h]
            state = initial_state(backward, h)
            order = list(range(n_blocks))
            if backward:
                order.reverse()
            for i in order:
                o = o_intra[h, d, i]
                if state is not None:
                    o = o + _dot(rq_ref[blocks[i], hs], state.astype(BF16)) * cross
                if (h, d, i) in kv:
                    state = kv[h, d, i] if state is None else state * g_block + kv[h, d, i]
                mu = jnp.mean(o, axis=-1, keepdims=True)
                dev = o - mu
                var = jnp.mean(dev * dev, axis=-1, keepdims=True)
                y = dev * lax.rsqrt(var + NORM_EPS) * gn_ref[:, hs]
                gate = g_ref[blocks[i], hs]
                contrib = gate * jax.nn.sigmoid(gate) * y
                if backward:
                    ret_ref[blocks[i], hs] = (acc_ref[blocks[i], hs] + contrib).astype(BF16)
                else:
                    acc_ref[blocks[i], hs] = contrib
            if store_state is not None:
                store_state(backward, h, state)


def _lat_mix_kernel(sink_ref, x_ref, mod_ref, n1_ref, w_ref, cos_a, sin_a, cos_r, sin_r, ck_ref, cv_ref,
                    intra_ref, cross_ref, kdec_ref, gblk_ref, gn_ref, s0f_ref, s0b_ref,
                    attn_ref, ret_ref,
                    hb_s, q_s, k_s, vt_s, ckb_s, cvt_s, rq_s, rk_s, rv_s, gf_s, gb_s, acc_s, *, layer):
    seq_len = x_ref.shape[0]
    nf_a = HEAD_DIM // 4
    nf_r = RET_DK // 4

    hb_s[...] = _normed_input(x_ref, mod_ref, n1_ref)
    for r0 in range(0, seq_len, ROW_TILE):
        rows = slice(r0, r0 + ROW_TILE)
        hb = hb_s[rows, :]
        q_s[rows, :] = _rope(_dot(hb, w_ref[:, OFF_Q:OFF_Q + ATTN_W]), cos_a[rows, :], sin_a[rows, :],
                             nf_a).astype(BF16)
        k_s[rows, :] = _rope(_dot(hb, w_ref[:, OFF_K:OFF_K + KV_W]), cos_a[rows, :], sin_a[rows, :],
                             nf_a).astype(BF16)
        v = _dot(hb, w_ref[:, OFF_V:OFF_V + KV_W])
        for t in range(ROW_TILE // Q_BLOCK):
            vt_s[r0 // Q_BLOCK + t] = jnp.transpose(v[t * Q_BLOCK:(t + 1) * Q_BLOCK, :]).astype(BF16)
    cvt_s[...] = jnp.concatenate([cv_ref[0, 0, kh] for kh in range(KV_HEADS)], axis=0).astype(BF16)
    ck = jnp.transpose(jnp.concatenate([ck_ref[0, 0, kh] for kh in range(KV_HEADS)], axis=0))
    for kh in range(KV_HEADS):
        ckb_s[kh] = ck[:, kh * HEAD_DIM:(kh + 1) * HEAD_DIM].astype(BF16)

    def query_blocks(i, carry):
        js = [i * LAT_BLOCKS_PER_STEP + n for n in range(LAT_BLOCKS_PER_STEP)]
        attention = [_lat_attention_scores(j, q_s, k_s, ckb_s) for j in js]
        n_rows = LAT_BLOCKS_PER_STEP * Q_BLOCK
        rows = slice(i * n_rows, (i + 1) * n_rows)
        hb = hb_s[rows, :]
        cos = cos_r[rows, :]
        sin = sin_r[rows, :]
        rq_s[rows, :] = _rope(_dot(hb, w_ref[:, OFF_RQ:OFF_RQ + RET_W]), cos, sin, nf_r).astype(BF16)
        rk_s[rows, :] = _rope(_dot(hb, w_ref[:, OFF_RK:OFF_RK + RET_W]), cos, sin, nf_r) * (RET_DK ** -0.5)
        rv_s[rows, :] = _dot(hb, w_ref[:, OFF_RV:OFF_RV + RET_VW]).astype(BF16)
        gf_s[rows, :] = _dot(hb, w_ref[:, OFF_RGF:OFF_RGF + RET_VW])
        gb_s[rows, :] = _dot(hb, w_ref[:, OFF_RGB:OFF_RGB + RET_VW])
        for j, (first, valid, scores) in zip(js, attention):
            _lat_attention_outputs(j, first, valid, scores, vt_s, cvt_s, sink_ref, attn_ref, layer)
        return carry

    for i in range(seq_len // (LAT_BLOCKS_PER_STEP * Q_BLOCK)):
        query_blocks(i, 0)

    products = _retention_products(rq_s, rk_s, rv_s, intra_ref, kdec_ref, [0], False)
    for h in range(RET_HEADS):
        following = (_retention_products(rq_s, rk_s, rv_s, intra_ref, kdec_ref, [h + 1], False)
                     if h + 1 < RET_HEADS else None)
        _retention_outputs(products, rq_s, gf_s, gb_s, cross_ref, gblk_ref, gn_ref, ret_ref, acc_s, [h],
                           lambda backward, h: (s0b_ref if backward else s0f_ref)[0, 0, h], None)
        products = following


def _latent_mixer(x, mod, n1, w_in, tables, cache_k, cache_v, sink, decay, gn, states0, layer, seq_len):
    n_tok = x.shape[0]
    past_len = cache_k.shape[4]
    seq = lambda w, **kw: pl.BlockSpec((seq_len, w), lambda b: (b, 0), **kw)
    per_batch = lambda dims: pl.BlockSpec((1, 1) + dims, lambda b: (b, layer) + (0,) * len(dims))
    cache = per_batch((KV_HEADS, HEAD_DIM, past_len))
    state = per_batch((RET_HEADS, RET_DK, RET_DV))
    n_blocks = seq_len // Q_BLOCK
    scratch = [pltpu.VMEM((seq_len, D_MODEL), BF16),
               pltpu.VMEM((seq_len, ATTN_W), BF16), pltpu.VMEM((seq_len, KV_W), BF16),
               pltpu.VMEM((n_blocks, KV_W, Q_BLOCK), BF16),
               pltpu.VMEM((KV_HEADS, past_len, HEAD_DIM), BF16),
               pltpu.VMEM((KV_W, past_len), BF16),
               pltpu.VMEM((seq_len, RET_W), BF16), pltpu.VMEM((seq_len, RET_W), F32),
               pltpu.VMEM((seq_len, RET_VW), BF16),
               pltpu.VMEM((seq_len, RET_VW), F32), pltpu.VMEM((seq_len, RET_VW), F32),
               pltpu.VMEM((seq_len, RET_VW), F32)]
    return pl.pallas_call(
        functools.partial(_lat_mix_kernel, layer=layer),
        grid=(n_tok // seq_len,),
        in_specs=[pl.BlockSpec(memory_space=pltpu.SMEM), seq(D_MODEL),
                  _mod_spec(1),
                  _layer_spec((1, D_MODEL), layer), _const_spec((D_MODEL, IN_WIDTH)),
                  *[_const_spec((seq_len, LANES))] * 4, cache, cache,
                  *_decay_specs(layer, RET_HEADS, lambda b: 0), _layer_spec((1, RET_VW), layer),
                  state, state],
        out_specs=[seq(ATTN_W), seq(RET_VW)],
        out_shape=[jax.ShapeDtypeStruct((n_tok, ATTN_W), BF16), jax.ShapeDtypeStruct((n_tok, RET_VW), BF16)],
        scratch_shapes=scratch,
        compiler_params=_params(1),
        name="latent_mixer",
    )(sink, x, mod, n1, w_in, *tables, cache_k, cache_v, *decay, gn, *states0)


N_CTX_MIX_OUT = 7
N_CTX_MIX_SCRATCH = 9


def _ctx_mix_kernel(*refs, layer, seq_len):
    n_in = len(refs) - N_CTX_MIX_OUT - N_CTX_MIX_SCRATCH
    (sink_ref, x_ref, mod_ref, n1_ref, w_ref, intra_ref, cross_ref, kdec_ref, gblk_ref,
     gn_ref) = refs[:10]
    attn_ref, ret_ref, g_ref, kc_ref, vc_ref, sf_ref, sb_ref = refs[n_in:n_in + N_CTX_MIX_OUT]
    q_s, k_s, v_s, rq_s, rk_s, rv_s, gf_s, gb_s, acc_s = refs[n_in + N_CTX_MIX_OUT:]
    hb = _normed_input(x_ref, mod_ref, n1_ref)

    def proj(off, width):
        return _dot(hb, w_ref[:, off:off + width])

    batch_rows = [slice(b, b + seq_len) for b in range(0, x_ref.shape[0], seq_len)]
    q_s[...] = proj(OFF_Q, ATTN_W).astype(BF16)
    k = proj(OFF_K, KV_W)
    v = proj(OFF_V, KV_W)
    k_s[...] = k.astype(BF16)
    v_s[...] = v.astype(BF16)
    slot = _zero_other_layers((kc_ref, vc_ref), layer)
    for b, rows in enumerate(batch_rows):
        for kh in range(KV_HEADS):
            kc_ref[b, slot, kh] = k[rows, kh * HEAD_DIM:(kh + 1) * HEAD_DIM]
            vc_ref[b, slot, kh] = v[rows, kh * HEAD_DIM:(kh + 1) * HEAD_DIM]
    scores = _ctx_attention_scores(q_s, k_s, seq_len)
    rq_s[...] = proj(OFF_RQ, RET_W).astype(BF16)
    rk_s[...] = proj(OFF_RK, RET_W) * (RET_DK ** -0.5)
    rv_s[...] = proj(OFF_RV, RET_VW).astype(BF16)
    products = [_retention_products(rq_s.at[rows], rk_s.at[rows], rv_s.at[rows], intra_ref, kdec_ref,
                                    range(RET_HEADS), True) for rows in batch_rows]
    gf_s[...] = proj(OFF_RGF, RET_VW)
    gb_s[...] = proj(OFF_RGB, RET_VW)
    state_slot = _zero_other_layers((sf_ref, sb_ref), layer)

    def attention_outputs(b):
        _ctx_attention_outputs(scores, batch_rows[b].start, v_s, sink_ref, attn_ref, layer, seq_len)

    def retention_outputs(b):
        rows = batch_rows[b]

        def store_state(backward, h, state):
            (sb_ref if backward else sf_ref)[b, state_slot, h] = state

        _retention_outputs(products[b], rq_s.at[rows], gf_s.at[rows], gb_s.at[rows], cross_ref, gblk_ref,
                           gn_ref, ret_ref.at[rows], acc_s.at[rows], range(RET_HEADS),
                           lambda backward, h: None, store_state)

    stages = ([functools.partial(attention_outputs, b) for b in range(len(batch_rows))]
              + [functools.partial(retention_outputs, b) for b in range(len(batch_rows))])
    chunk = 2 * D_MODEL // len(stages)
    for n, stage in enumerate(stages):
        g_ref[:, n * chunk:(n + 1) * chunk] = proj(OFF_GA + n * chunk, chunk)
        stage()


def _context_mixer(x, mod, n1, w_in, sink, decay, gn, layer, seq_len, prev_outputs):
    n_tok = x.shape[0]
    n_batch = n_tok // seq_len
    per_tile = ROW_TILE // seq_len
    row = lambda w: pl.BlockSpec((ROW_TILE, w), lambda i: (i, 0))
    in_specs = [pl.BlockSpec(memory_space=pltpu.SMEM), row(D_MODEL), _mod_spec(None),
                _layer_spec((1, D_MODEL), layer), _const_spec((D_MODEL, IN_WIDTH)),
                *_decay_specs(layer, RET_HEADS, lambda i: 0), _layer_spec((1, RET_VW), layer)]
    args = [sink, x, mod, n1, w_in, *decay, gn]
    out_specs = [row(ATTN_W), row(RET_VW), row(2 * D_MODEL)]
    out_shape = [jax.ShapeDtypeStruct((n_tok, ATTN_W), BF16), jax.ShapeDtypeStruct((n_tok, RET_VW), BF16),
                 jax.ShapeDtypeStruct((n_tok, 2 * D_MODEL), F32)]
    cache_dims = (KV_HEADS, seq_len, HEAD_DIM)
    state_dims = (RET_HEADS, RET_DK, RET_DV)
    aliases = {}
    for idx, dims in enumerate((cache_dims, cache_dims, state_dims, state_dims)):
        if prev_outputs is None:
            assert layer == 0
            out_specs.append(pl.BlockSpec((per_tile, DEPTH) + dims, lambda i: (i, 0, 0, 0, 0)))
        else:
            aliases[len(args)] = len(out_shape)
            in_specs.append(pl.BlockSpec(memory_space=pl.ANY))
            args.append(prev_outputs[idx])
            out_specs.append(pl.BlockSpec((per_tile, 1) + dims, lambda i: (i, layer, 0, 0, 0)))
        out_shape.append(jax.ShapeDtypeStruct((n_batch, DEPTH) + dims, F32))
    assert len(out_shape) == N_CTX_MIX_OUT
    scratch = [pltpu.VMEM((ROW_TILE, ATTN_W), BF16), pltpu.VMEM((ROW_TILE, KV_W), BF16),
               pltpu.VMEM((ROW_TILE, KV_W), BF16), pltpu.VMEM((ROW_TILE, RET_W), BF16),
               pltpu.VMEM((ROW_TILE, RET_W), F32), pltpu.VMEM((ROW_TILE, RET_VW), BF16),
               pltpu.VMEM((ROW_TILE, RET_VW), F32), pltpu.VMEM((ROW_TILE, RET_VW), F32),
               pltpu.VMEM((ROW_TILE, RET_VW), F32)]
    assert len(scratch) == N_CTX_MIX_SCRATCH
    return pl.pallas_call(
        functools.partial(_ctx_mix_kernel, layer=layer, seq_len=seq_len),
        grid=(n_tok // ROW_TILE,),
        in_specs=in_specs,
        out_specs=out_specs,
        out_shape=out_shape,
        scratch_shapes=scratch,
        input_output_aliases=aliases,
        compiler_params=_params(1),
        name="context_mixer",
    )(*args)


def _post_kernel(*refs, final, n_cast, next_mod, own_gates):
    (attn_ref, ret_ref, g0_ref, g1_ref, x_ref, mod_ref, n2_ref,
     wba_ref, wbr_ref, wo_ref, w1_ref, w2_ref) = refs[:12]
    n_side_in = n_cast + (3 if next_mod else 0)
    n_in = 12 + (1 if final else 0) + n_side_in
    fn_ref = refs[12] if final else None
    out_ref = refs[n_in]
    side_in = refs[n_in - n_side_in:n_in]
    side_out = refs[n_in + 1:]
    for src, dst in zip(side_in[:n_cast], side_out[:n_cast]):
        dst[...] = src[...].astype(BF16)
    if next_mod:
        side_out[n_cast][...] = _modulation_columns(*side_in[n_cast:])
    mod = mod_ref[...]
    part = lambda k: mod[:, k * D_MODEL:(k + 1) * D_MODEL]
    if own_gates:
        hb = _normed_input(x_ref, mod_ref, g0_ref)
        gate_a = _dot(hb, g1_ref[:, OFF_GA:OFF_GA + D_MODEL])
        gate_r = _dot(hb, g1_ref[:, OFF_GA + D_MODEL:OFF_GA + 2 * D_MODEL])
    else:
        gate_a = g0_ref[...]
        gate_r = g1_ref[...]
    merged = (jax.nn.sigmoid(gate_a) * _dot(attn_ref[...], wba_ref[...])
              + jax.nn.sigmoid(gate_r) * _dot(ret_ref[...], wbr_ref[...]))
    x = x_ref[...] + part(2) * _dot(merged.astype(BF16), wo_ref[...])
    h2 = _modulated_norm(x, n2_ref[...], part(3), part(4)).astype(BF16)
    ff = None
    for c in range(0, D_FF, FF_CHUNK):
        a = jnp.maximum(_dot(h2, w1_ref[:, c:c + FF_CHUNK]), 0.0)
        t = _dot((a * a).astype(BF16), w2_ref[c:c + FF_CHUNK, :])
        ff = t if ff is None else ff + t
    x = x + part(5) * ff
    if final:
        x = x * lax.rsqrt(jnp.mean(x * x, axis=-1, keepdims=True) + NORM_EPS) * fn_ref[...]
    out_ref[...] = x


def _post(attn, ret, gates, x, mod, n2, weights, layer, seq_len, latent, final_gain, cast_next=(),
          mod_next=None):
    n_tok = x.shape[0]
    n_steps = n_tok // ROW_TILE
    tiles_per_batch = seq_len // ROW_TILE
    final = final_gain is not None
    own_gates = isinstance(gates, tuple)
    row = lambda w, c=0: pl.BlockSpec((ROW_TILE, w), lambda i, c=c: (i, c))
    if own_gates:
        gate_specs = [_layer_spec((1, D_MODEL), layer), _const_spec((D_MODEL, IN_WIDTH))]
        gate_args = list(gates)
    else:
        gate_specs = [row(D_MODEL, 0), row(D_MODEL, 1)]
        gate_args = [gates, gates]
    in_specs = [
        row(ATTN_W), row(RET_VW), *gate_specs, row(D_MODEL),
        _mod_spec(tiles_per_batch if latent else None),
        _layer_spec((1, D_MODEL), layer),
        *[_const_spec(w.shape) for w in weights],
    ]
    args = [attn, ret, *gate_args, x, mod, n2, *weights]
    if final:
        in_specs.append(_const_spec((1, D_MODEL)))
        args.append(final_gain)
    out_specs = [row(D_MODEL)]
    out_shape = [jax.ShapeDtypeStruct((n_tok, D_MODEL), F32)]
    for w in cast_next:
        _, k, n = w.shape
        rows = k // n_steps
        assert rows * n_steps == k and rows % 16 == 0
        in_specs.append(pl.BlockSpec((None, rows, n), lambda i: (layer + 1, i, 0)))
        args.append(w)
        out_specs.append(pl.BlockSpec((rows, n), lambda i: (i, 0)))
        out_shape.append(jax.ShapeDtypeStruct((k, n), BF16))
    if mod_next is not None:
        n_rows = mod_next[0].shape[0]
        mod_in_specs, mod_out_spec = _mod_specs(layer + 1, n_rows, MOD_W // n_steps)
        in_specs += mod_in_specs
        args += list(mod_next)
        out_specs.append(mod_out_spec)
        out_shape.append(jax.ShapeDtypeStruct((n_rows, MOD_W), F32))
    return pl.pallas_call(
        functools.partial(_post_kernel, final=final, n_cast=len(cast_next),
                          next_mod=mod_next is not None, own_gates=own_gates),
        grid=(n_steps,),
        in_specs=in_specs,
        out_specs=out_specs,
        out_shape=out_shape,
        compiler_params=_params(1),
        name="post_final" if final else "post",
    )(*args)


def _rope_tables(seq_len):
    t = np.arange(seq_len)
    rows = (t // GRID_W).astype(np.float64)
    cols = (t % GRID_W).astype(np.float64)

    def table(head_dim):
        half = head_dim // 2
        nf = half // 2
        inv = ROPE_BASE ** (-np.arange(nf, dtype=np.float64) / nf)
        lane = np.arange(LANES) % head_dim
        pos = np.where((lane < half)[None, :], rows[:, None], cols[:, None])
        ang = pos * inv[lane % nf][None, :]
        sign = np.where((lane % half) < nf, -1.0, 1.0)[None, :]
        return jnp.asarray(np.cos(ang), F32), jnp.asarray(np.sin(ang) * sign, F32)

    cos_a, sin_a = table(HEAD_DIM)
    cos_r, sin_r = table(RET_DK)
    return cos_a, sin_a, cos_r, sin_r


def kernel(x_prompt, x_sample, cache_k, cache_v, state_ret_fwd, state_ret_bwd, c, c_ctx, norm1_g, norm2_g, final_norm_g, w_mod, b_mod, w_in, attn_sink, ret_decay_fwd, ret_decay_bwd, ret_gn_g, w_branch_attn, w_branch_ret, w_out, w_ff1, w_ff2):
    n_ctx, ctx_len, _ = x_prompt.shape
    n_lat, lat_len, _ = x_sample.shape
    xp = x_prompt.reshape(n_ctx * ctx_len, D_MODEL)
    xs = x_sample.reshape(n_lat * lat_len, D_MODEL)

    cond = jnp.concatenate([c_ctx[None, :], c], axis=0)
    cond8 = jnp.pad(cond, ((0, -cond.shape[0] % 8), (0, 0)))
    mod_inputs = (cond8, w_mod, b_mod.reshape(DEPTH, 1, MOD_W))
    mod = _modulation(*mod_inputs, 0).reshape(cond8.shape[0], 1, MOD_W)
    cache_kt = jnp.swapaxes(cache_k, 3, 4)
    cache_vt = jnp.swapaxes(cache_v, 3, 4)
    dec_rows = jnp.broadcast_to(jnp.stack([ret_decay_fwd, ret_decay_bwd], axis=1)[..., None],
                                (DEPTH, 2, RET_HEADS, LANES))
    decay = _decay_tables(dec_rows)
    tables = _rope_tables(lat_len)
    final_gain = final_norm_g.reshape(1, D_MODEL)

    n1 = norm1_g.reshape(DEPTH, 1, D_MODEL)
    n2 = norm2_g.reshape(DEPTH, 1, D_MODEL)
    gn = ret_gn_g.reshape(DEPTH, 1, RET_VW)
    f32_weights = (w_in, w_branch_attn, w_branch_ret, w_out, w_ff1, w_ff2)
    w_in_b, *post_w = [w[0].astype(BF16) for w in f32_weights]

    ctx_outputs = None
    for l in range(DEPTH):
        last = l == DEPTH - 1
        attn, ret, gates, *ctx_outputs = _context_mixer(xp, mod, n1, w_in_b, attn_sink, decay, gn, l,
                                                        ctx_len, ctx_outputs)
        xp, *next_layer = _post(attn, ret, gates, xp, mod, n2, post_w, l, ctx_len, False,
                                final_gain if last else None, () if last else f32_weights,
                                None if last else mod_inputs)
        attn, ret = _latent_mixer(xs, mod, n1, w_in_b, tables, cache_kt, cache_vt, attn_sink, decay, gn,
                                  (state_ret_fwd, state_ret_bwd), l, lat_len)
        (xs,) = _post(attn, ret, (n1, w_in_b), xs, mod, n2, post_w, l, lat_len, True,
                      final_gain if last else None)
        if not last:
            w_in_b, *post_w, mod = next_layer
            mod = mod.reshape(cond8.shape[0], 1, MOD_W)

    return (xp.reshape(n_ctx, ctx_len, D_MODEL), xs.reshape(n_lat, lat_len, D_MODEL), *ctx_outputs)
```

```python
import functools

import jax
import jax.numpy as jnp
import numpy as np
from jax import lax
from jax.experimental import pallas as pl
from jax.experimental.pallas import tpu as pltpu

D_MODEL = 1024
DEPTH = 4
GRID_W = 64
N_HEADS = 8
HEAD_DIM = 64
KV_HEADS = 2
GROUP = N_HEADS // KV_HEADS
WINDOW = 128
Q_BLOCK = 128
RET_HEADS = 4
RET_DK = 128
RET_DV = 128
ATTN_W = N_HEADS * HEAD_DIM
KV_W = KV_HEADS * HEAD_DIM
RET_W = RET_HEADS * RET_DK
RET_VW = RET_HEADS * RET_DV
GATE_W = 2 * RET_VW + 2 * D_MODEL
IN_WIDTH = ATTN_W + 2 * KV_W + 2 * RET_W + RET_VW + GATE_W
D_FF = 4 * D_MODEL
MOD_W = 6 * D_MODEL
ROPE_BASE = 10000.0
NORM_EPS = 1e-6
NEG_INF = -1e30

OFF_Q = 0
OFF_K = OFF_Q + ATTN_W
OFF_V = OFF_K + KV_W
OFF_RQ = OFF_V + KV_W
OFF_RK = OFF_RQ + RET_W
OFF_RV = OFF_RK + RET_W
OFF_RGF = OFF_RV + RET_VW
OFF_RGB = OFF_RGF + RET_VW
OFF_GA = OFF_RGB + RET_VW

LANES = 128
ROW_TILE = 512
RET_BLOCK = 256
LAT_BLOCKS_PER_STEP = 4
MOD_TILE_N = 1536
FF_CHUNK = 1024
VMEM_LIMIT = 58 * 1024 * 1024

BF16 = jnp.bfloat16
F32 = jnp.float32


def _params(n_axes):
    return pltpu.CompilerParams(dimension_semantics=("arbitrary",) * n_axes,
                                vmem_limit_bytes=VMEM_LIMIT)


def _const_spec(shape):
    nd = len(shape)
    return pl.BlockSpec(shape, lambda *_: (0,) * nd, pipeline_mode=pl.Buffered(1))


def _layer_spec(shape, layer):
    nd = len(shape)
    return pl.BlockSpec((None,) + tuple(shape), lambda *_: (layer,) + (0,) * nd,
                        pipeline_mode=pl.Buffered(1))


def _mod_spec(tiles_per_batch):
    if tiles_per_batch is None:
        return pl.BlockSpec((None, 1, MOD_W), lambda *_: (0, 0, 0))
    return pl.BlockSpec((None, 1, MOD_W), lambda i: (1 + i // tiles_per_batch, 0, 0))


def _zero_other_layers(out_refs, layer):
    n_slots = out_refs[0].shape[1]
    if n_slots == 1:
        return 0
    for ref in out_refs:
        for other in range(n_slots):
            if other != layer:
                ref[:, other] = jnp.zeros((ref.shape[0],) + ref.shape[2:], ref.dtype)
    return layer


def _dot(a, b):
    return jnp.dot(a, b, preferred_element_type=F32)


def _dot_nt(a, b):
    return lax.dot_general(a, b, (((1,), (1,)), ((), ())), preferred_element_type=F32)


def _dot_tn(a, b):
    return lax.dot_general(a, b, (((0,), (0,)), ((), ())), preferred_element_type=F32)


def _modulation_columns(cond_ref, w_ref, b_ref):
    cond = cond_ref[...]
    s = (cond * jax.nn.sigmoid(cond)).astype(BF16)
    return _dot(s, w_ref[...].astype(BF16)) + b_ref[...]


def _mod_kernel(cond_ref, w_ref, b_ref, out_ref):
    out_ref[...] = _modulation_columns(cond_ref, w_ref, b_ref)


def _mod_specs(layer, n_rows, tile_n):
    return ([pl.BlockSpec((n_rows, D_MODEL), lambda j: (0, 0)),
             pl.BlockSpec((None, D_MODEL, tile_n), lambda j: (layer, 0, j)),
             pl.BlockSpec((None, 1, tile_n), lambda j: (layer, 0, j))],
            pl.BlockSpec((n_rows, tile_n), lambda j: (0, j)))


def _modulation(cond8, w_mod, b_mod, layer):
    n_rows = cond8.shape[0]
    in_specs, out_spec = _mod_specs(layer, n_rows, MOD_TILE_N)
    return pl.pallas_call(
        _mod_kernel,
        grid=(MOD_W // MOD_TILE_N,),
        in_specs=in_specs,
        out_specs=out_spec,
        out_shape=jax.ShapeDtypeStruct((n_rows, MOD_W), F32),
        compiler_params=_params(1),
        name="adaln_modulation",
    )(cond8, w_mod, b_mod)


def _decay_kernel(dec_ref, intra_ref, cross_ref, kdec_ref, gblk_ref):
    backward = pl.program_id(1) == 1
    log_g = jnp.log(jax.nn.sigmoid(dec_ref[...]))
    n = RET_BLOCK
    row = lax.broadcasted_iota(jnp.int32, (n, n), 0).astype(F32)
    col = lax.broadcasted_iota(jnp.int32, (n, n), 1).astype(F32)
    diff = jnp.where(backward, col - row, row - col)
    row1 = row[:, :LANES]
    cross_pow = jnp.where(backward, n - row1, row1 + 1.0)
    kdec_pow = jnp.where(backward, row1, n - 1.0 - row1)
    for h in range(RET_HEADS):
        lg = log_g[h:h + 1, :]
        lg_wide = jnp.concatenate([lg] * (n // LANES), axis=1)
        intra_ref[h] = jnp.where(diff >= 0, jnp.exp(jnp.maximum(diff, 0.0) * lg_wide), 0.0)
        cross_ref[h] = jnp.exp(cross_pow * lg)
        kdec_ref[h] = jnp.exp(kdec_pow * lg)
        gblk_ref[h] = jnp.exp(n * lg)


def _decay_tables(dec_rows):
    lead = lambda *tail: pl.BlockSpec((None, None, RET_HEADS) + tail,
                                      lambda l, d: (l, d, 0) + (0,) * len(tail))
    shape = lambda *tail: jax.ShapeDtypeStruct((DEPTH, 2, RET_HEADS) + tail, F32)
    return pl.pallas_call(
        _decay_kernel,
        grid=(DEPTH, 2),
        in_specs=[lead(LANES)],
        out_specs=[lead(RET_BLOCK, RET_BLOCK), lead(RET_BLOCK, LANES), lead(RET_BLOCK, LANES),
                   lead(1, LANES)],
        out_shape=[shape(RET_BLOCK, RET_BLOCK), shape(RET_BLOCK, LANES), shape(RET_BLOCK, LANES),
                   shape(1, LANES)],
        compiler_params=_params(2),
        name="retention_decay_tables",
    )(dec_rows)


def _decay_specs(layer, n_heads, index_map):
    mode = {"pipeline_mode": pl.Buffered(1)} if n_heads == RET_HEADS else {}
    tab = lambda *tail: pl.BlockSpec((None, 2, n_heads) + tail,
                                     lambda *idx: (layer, 0, index_map(*idx)) + (0,) * len(tail), **mode)
    return [tab(RET_BLOCK, RET_BLOCK), tab(RET_BLOCK, LANES), tab(RET_BLOCK, LANES), tab(1, LANES)]


def _modulated_norm(x, gain, shift, scale):
    y = x * lax.rsqrt(jnp.mean(x * x, axis=-1, keepdims=True) + NORM_EPS) * gain
    return y * (1.0 + scale) + shift


def _normed_input(x_ref, mod_ref, n1_ref):
    mod = mod_ref[...]
    h = _modulated_norm(x_ref[...], n1_ref[...], mod[:, 0:D_MODEL], mod[:, D_MODEL:2 * D_MODEL])
    return h.astype(BF16)


def _rope_chunk(xc, cos, sin_signed, first_half, nf):
    partner = jnp.where(first_half, pltpu.roll(xc, LANES - nf, 1), pltpu.roll(xc, nf, 1))
    return xc * cos + partner * sin_signed


def _rope(p, cos, sin_signed, nf):
    lane = lax.broadcasted_iota(jnp.int32, (p.shape[0], LANES), 1)
    first_half = (lane % (2 * nf)) < nf
    parts = [_rope_chunk(p[:, c:c + LANES], cos, sin_signed, first_half, nf)
             for c in range(0, p.shape[1], LANES)]
    return parts[0] if len(parts) == 1 else jnp.concatenate(parts, axis=1)


def _softmax_t(score_blocks, sink):
    m = sink
    for s in score_blocks:
        m = jnp.maximum(jnp.max(s, axis=0, keepdims=True), m)
    es = [jnp.exp(s - m) for s in score_blocks]
    denom = jnp.exp(sink - m)
    for e in es:
        denom = denom + jnp.sum(e, axis=0, keepdims=True)
    return [e.astype(BF16) for e in es], 1.0 / denom


def _store_heads(o_ref, rows, heads, o_t, inv):
    for pair in range(0, len(heads), 2):
        halves = [o_t[:, g * Q_BLOCK:(g + 1) * Q_BLOCK] * inv[g] for g in (pair, pair + 1)]
        both = jnp.transpose(jnp.concatenate(halves, axis=0))
        h0 = heads[pair]
        o_ref[rows, h0 * HEAD_DIM:(h0 + 2) * HEAD_DIM] = both.astype(BF16)


def _ctx_attention_scores(q_ref, k_ref, seq_len):
    scale = HEAD_DIM ** -0.5
    out = []
    for b in range(0, q_ref.shape[0], seq_len):
        for kh in range(KV_HEADS):
            for c in range(b, b + seq_len, Q_BLOCK):
                heads = [kh * GROUP + g for g in range(GROUP)]
                q4 = jnp.concatenate(
                    [q_ref[c:c + Q_BLOCK, h * HEAD_DIM:(h + 1) * HEAD_DIM] for h in heads], axis=0)
                keys = k_ref[b:b + seq_len, kh * HEAD_DIM:(kh + 1) * HEAD_DIM]
                out.append(((b, kh, c), _dot_nt(keys, q4) * scale))
    return out


def _ctx_attention_outputs(scores, b, v_ref, sink_ref, o_ref, layer, seq_len):
    v_t = jnp.transpose(v_ref[b:b + seq_len, :].astype(F32)).astype(BF16)
    for (sb, kh, c), s4 in scores:
        if sb != b:
            continue
        heads = [kh * GROUP + g for g in range(GROUP)]
        es, inv = [], []
        for g, h in enumerate(heads):
            (e,), r = _softmax_t([s4[:, g * Q_BLOCK:(g + 1) * Q_BLOCK]], sink_ref[layer, h])
            es.append(e)
            inv.append(r)
        o_t = _dot(v_t[kh * HEAD_DIM:(kh + 1) * HEAD_DIM, :], jnp.concatenate(es, axis=1))
        _store_heads(o_ref, slice(c, c + Q_BLOCK), heads, o_t, inv)


def _lat_attention_scores(j, q_ref, k_ref, ckb_ref):
    scale = HEAD_DIM ** -0.5
    n_blocks = k_ref.shape[0] // Q_BLOCK
    reach = WINDOW // Q_BLOCK
    first = max(j - reach, 0)
    band = (min(j + reach, n_blocks - 1) - first + 1) * Q_BLOCK
    start = first * Q_BLOCK
    kpos = start + lax.broadcasted_iota(jnp.int32, (band, Q_BLOCK), 0)
    qpos = j * Q_BLOCK + lax.broadcasted_iota(jnp.int32, (band, Q_BLOCK), 1)
    valid = jnp.abs(qpos - kpos) <= WINDOW
    kb = k_ref[start:start + band, :]
    rows = slice(j * Q_BLOCK, (j + 1) * Q_BLOCK)
    scores = []
    for kh in range(KV_HEADS):
        hd = slice(kh * HEAD_DIM, (kh + 1) * HEAD_DIM)
        heads = [kh * GROUP + g for g in range(GROUP)]
        q4 = jnp.concatenate([q_ref[rows, h * HEAD_DIM:(h + 1) * HEAD_DIM] for h in heads], axis=0)
        scores.append((_dot_nt(kb[:, hd], q4) * scale, _dot_nt(ckb_ref[kh], q4) * scale))
    return first, valid, scores


def _lat_attention_outputs(j, first, valid, scores, vt_ref, cvt_ref, sink_ref, o_ref, layer):
    rows = slice(j * Q_BLOCK, (j + 1) * Q_BLOCK)
    for kh, (s_loc4, s_ctx4) in enumerate(scores):
        hd = slice(kh * HEAD_DIM, (kh + 1) * HEAD_DIM)
        heads = [kh * GROUP + g for g in range(GROUP)]
        e_loc, e_ctx, inv = [], [], []
        for g, h in enumerate(heads):
            cols = slice(g * Q_BLOCK, (g + 1) * Q_BLOCK)
            (el, ec), r = _softmax_t([jnp.where(valid, s_loc4[:, cols], NEG_INF), s_ctx4[:, cols]],
                                     sink_ref[layer, h])
            e_loc.append(el)
            e_ctx.append(ec)
            inv.append(r)
        e_loc4 = jnp.concatenate(e_loc, axis=1)
        o_t = _dot(cvt_ref[hd, :], jnp.concatenate(e_ctx, axis=1))
        for t in range(e_loc4.shape[0] // Q_BLOCK):
            o_t = o_t + _dot(vt_ref[first + t, hd, :], e_loc4[t * Q_BLOCK:(t + 1) * Q_BLOCK, :])
        _store_heads(o_ref, rows, heads, o_t, inv)


def _retention_products(rq_ref, rk_ref, rv_ref, intra_ref, kdec_ref, heads, final_state):
    n_blocks = rq_ref.shape[0] // RET_BLOCK
    blocks = [slice(i * RET_BLOCK, (i + 1) * RET_BLOCK) for i in range(n_blocks)]
    head_cols = {h: slice(h * RET_DK, (h + 1) * RET_DK) for h in heads}
    qk = {(h, i): _dot_nt(rq_ref[blocks[i], head_cols[h]], rk_ref[blocks[i], head_cols[h]].astype(BF16))
          for h in heads for i in range(n_blocks)}
    o_intra, kv = {}, {}
    for h in heads:
        for d in range(2):
            for i in range(n_blocks):
                v_i = rv_ref[blocks[i], head_cols[h]]
                o_intra[h, d, i] = _dot((qk[h, i] * intra_ref[d, h]).astype(BF16), v_i)
                last = i == (0 if d else n_blocks - 1)
                if final_state or not last:
                    k_i = rk_ref[blocks[i], head_cols[h]]
                    kv[h, d, i] = _dot_tn((k_i * kdec_ref[d, h]).astype(BF16), v_i)
    return o_intra, kv


def _retention_outputs(products, rq_ref, gf_ref, gb_ref, cross_ref, gblk_ref, gn_ref, ret_ref, acc_ref,
                       heads, initial_state, store_state):
    o_intra, kv = products
    n_blocks = rq_ref.shape[0] // RET_BLOCK
    blocks = [slice(i * RET_BLOCK, (i + 1) * RET_BLOCK) for i in range(n_blocks)]
    for h in heads:
        hs = slice(h * RET_DK, (h + 1) * RET_DK)
        for d, backward in enumerate((False, True)):
            g_ref = gb_ref if backward else gf_ref
            cross = cross_ref[d, h]
            g_block = gblk_ref[d, h]
            state = initial_state(backward, h)
            order = list(range(n_blocks))
            if backward:
                order.reverse()
            for i in order:
                o = o_intra[h, d, i]
                if state is not None:
                    o = o + _dot(rq_ref[blocks[i], hs], state.astype(BF16)) * cross
                if (h, d, i) in kv:
                    state = kv[h, d, i] if state is None else state * g_block + kv[h, d, i]
                mu = jnp.mean(o, axis=-1, keepdims=True)
                dev = o - mu
                var = jnp.mean(dev * dev, axis=-1, keepdims=True)
                y = dev * lax.rsqrt(var + NORM_EPS) * gn_ref[:, hs]
                gate = g_ref[blocks[i], hs]
                contrib = gate * jax.nn.sigmoid(gate) * y
                if backward:
                    ret_ref[blocks[i], hs] = (acc_ref[blocks[i], hs] + contrib).astype(BF16)
                else:
                    acc_ref[blocks[i], hs] = contrib
            if store_state is not None:
                store_state(backward, h, state)


def _lat_mix_kernel(sink_ref, x_ref, mod_ref, n1_ref, w_ref, cos_a, sin_a, cos_r, sin_r, ck_ref, cv_ref,
                    intra_ref, cross_ref, kdec_ref, gblk_ref, gn_ref, s0f_ref, s0b_ref,
                    attn_ref, ret_ref,
                    hb_s, q_s, k_s, vt_s, ckb_s, cvt_s, rq_s, rk_s, rv_s, gf_s, gb_s, acc_s, *, layer):
    seq_len = x_ref.shape[0]
    nf_a = HEAD_DIM // 4
    nf_r = RET_DK // 4

    hb_s[...] = _normed_input(x_ref, mod_ref, n1_ref)
    for r0 in range(0, seq_len, ROW_TILE):
        rows = slice(r0, r0 + ROW_TILE)
        hb = hb_s[rows, :]
        q_s[rows, :] = _rope(_dot(hb, w_ref[:, OFF_Q:OFF_Q + ATTN_W]), cos_a[rows, :], sin_a[rows, :],
                             nf_a).astype(BF16)
        k_s[rows, :] = _rope(_dot(hb, w_ref[:, OFF_K:OFF_K + KV_W]), cos_a[rows, :], sin_a[rows, :],
                             nf_a).astype(BF16)
        v = _dot(hb, w_ref[:, OFF_V:OFF_V + KV_W])
        for t in range(ROW_TILE // Q_BLOCK):
            vt_s[r0 // Q_BLOCK + t] = jnp.transpose(v[t * Q_BLOCK:(t + 1) * Q_BLOCK, :]).astype(BF16)
    cvt_s[...] = jnp.concatenate([cv_ref[0, 0, kh] for kh in range(KV_HEADS)], axis=0).astype(BF16)
    ck = jnp.transpose(jnp.concatenate([ck_ref[0, 0, kh] for kh in range(KV_HEADS)], axis=0))
    for kh in range(KV_HEADS):
        ckb_s[kh] = ck[:, kh * HEAD_DIM:(kh + 1) * HEAD_DIM].astype(BF16)

    def query_blocks(i, carry):
        js = [i * LAT_BLOCKS_PER_STEP + n for n in range(LAT_BLOCKS_PER_STEP)]
        attention = [_lat_attention_scores(j, q_s, k_s, ckb_s) for j in js]
        n_rows = LAT_BLOCKS_PER_STEP * Q_BLOCK
        rows = slice(i * n_rows, (i + 1) * n_rows)
        hb = hb_s[rows, :]
        cos = cos_r[rows, :]
        sin = sin_r[rows, :]
        rq_s[rows, :] = _rope(_dot(hb, w_ref[:, OFF_RQ:OFF_RQ + RET_W]), cos, sin, nf_r).astype(BF16)
        rk_s[rows, :] = _rope(_dot(hb, w_ref[:, OFF_RK:OFF_RK + RET_W]), cos, sin, nf_r) * (RET_DK ** -0.5)
        rv_s[rows, :] = _dot(hb, w_ref[:, OFF_RV:OFF_RV + RET_VW]).astype(BF16)
        gf_s[rows, :] = _dot(hb, w_ref[:, OFF_RGF:OFF_RGF + RET_VW])
        gb_s[rows, :] = _dot(hb, w_ref[:, OFF_RGB:OFF_RGB + RET_VW])
        for j, (first, valid, scores) in zip(js, attention):
            _lat_attention_outputs(j, first, valid, scores, vt_s, cvt_s, sink_ref, attn_ref, layer)
        return carry

    for i in range(seq_len // (LAT_BLOCKS_PER_STEP * Q_BLOCK)):
        query_blocks(i, 0)

    products = _retention_products(rq_s, rk_s, rv_s, intra_ref, kdec_ref, [0], False)
    for h in range(RET_HEADS):
        following = (_retention_products(rq_s, rk_s, rv_s, intra_ref, kdec_ref, [h + 1], False)
                     if h + 1 < RET_HEADS else None)
        _retention_outputs(products, rq_s, gf_s, gb_s, cross_ref, gblk_ref, gn_ref, ret_ref, acc_s, [h],
                           lambda backward, h: (s0b_ref if backward else s0f_ref)[0, 0, h], None)
        products = following


def _latent_mixer(x, mod, n1, w_in, tables, cache_k, cache_v, sink, decay, gn, states0, layer, seq_len):
    n_tok = x.shape[0]
    past_len = cache_k.shape[4]
    seq = lambda w, **kw: pl.BlockSpec((seq_len, w), lambda b: (b, 0), **kw)
    per_batch = lambda dims: pl.BlockSpec((1, 1) + dims, lambda b: (b, layer) + (0,) * len(dims))
    cache = per_batch((KV_HEADS, HEAD_DIM, past_len))
    state = per_batch((RET_HEADS, RET_DK, RET_DV))
    n_blocks = seq_len // Q_BLOCK
    scratch = [pltpu.VMEM((seq_len, D_MODEL), BF16),
               pltpu.VMEM((seq_len, ATTN_W), BF16), pltpu.VMEM((seq_len, KV_W), BF16),
               pltpu.VMEM((n_blocks, KV_W, Q_BLOCK), BF16),
               pltpu.VMEM((KV_HEADS, past_len, HEAD_DIM), BF16),
               pltpu.VMEM((KV_W, past_len), BF16),
               pltpu.VMEM((seq_len, RET_W), BF16), pltpu.VMEM((seq_len, RET_W), F32),
               pltpu.VMEM((seq_len, RET_VW), BF16),
               pltpu.VMEM((seq_len, RET_VW), F32), pltpu.VMEM((seq_len, RET_VW), F32),
               pltpu.VMEM((seq_len, RET_VW), F32)]
    return pl.pallas_call(
        functools.partial(_lat_mix_kernel, layer=layer),
        grid=(n_tok // seq_len,),
        in_specs=[pl.BlockSpec(memory_space=pltpu.SMEM), seq(D_MODEL),
                  _mod_spec(1),
                  _layer_spec((1, D_MODEL), layer), _const_spec((D_MODEL, IN_WIDTH)),
                  *[_const_spec((seq_len, LANES))] * 4, cache, cache,
                  *_decay_specs(layer, RET_HEADS, lambda b: 0), _layer_spec((1, RET_VW), layer),
                  state, state],
        out_specs=[seq(ATTN_W), seq(RET_VW)],
        out_shape=[jax.ShapeDtypeStruct((n_tok, ATTN_W), BF16), jax.ShapeDtypeStruct((n_tok, RET_VW), BF16)],
        scratch_shapes=scratch,
        compiler_params=_params(1),
        name="latent_mixer",
    )(sink, x, mod, n1, w_in, *tables, cache_k, cache_v, *decay, gn, *states0)


N_CTX_MIX_OUT = 7
N_CTX_MIX_SCRATCH = 9


def _ctx_mix_kernel(*refs, layer, seq_len):
    n_in = len(refs) - N_CTX_MIX_OUT - N_CTX_MIX_SCRATCH
    (sink_ref, x_ref, mod_ref, n1_ref, w_ref, intra_ref, cross_ref, kdec_ref, gblk_ref,
     gn_ref) = refs[:10]
    attn_ref, ret_ref, g_ref, kc_ref, vc_ref, sf_ref, sb_ref = refs[n_in:n_in + N_CTX_MIX_OUT]
    q_s, k_s, v_s, rq_s, rk_s, rv_s, gf_s, gb_s, acc_s = refs[n_in + N_CTX_MIX_OUT:]
    hb = _normed_input(x_ref, mod_ref, n1_ref)

    def proj(off, width):
        return _dot(hb, w_ref[:, off:off + width])

    batch_rows = [slice(b, b + seq_len) for b in range(0, x_ref.shape[0], seq_len)]
    q_s[...] = proj(OFF_Q, ATTN_W).astype(BF16)
    k = proj(OFF_K, KV_W)
    v = proj(OFF_V, KV_W)
    k_s[...] = k.astype(BF16)
    v_s[...] = v.astype(BF16)
    slot = _zero_other_layers((kc_ref, vc_ref), layer)
    for b, rows in enumerate(batch_rows):
        for kh in range(KV_HEADS):
            kc_ref[b, slot, kh] = k[rows, kh * HEAD_DIM:(kh + 1) * HEAD_DIM]
            vc_ref[b, slot, kh] = v[rows, kh * HEAD_DIM:(kh + 1) * HEAD_DIM]
    scores = _ctx_attention_scores(q_s, k_s, seq_len)
    rq_s[...] = proj(OFF_RQ, RET_W).astype(BF16)
    rk_s[...] = proj(OFF_RK, RET_W) * (RET_DK ** -0.5)
    rv_s[...] = proj(OFF_RV, RET_VW).astype(BF16)
    products = {}
    gf_s[...] = proj(OFF_RGF, RET_VW)
    gb_s[...] = proj(OFF_RGB, RET_VW)
    state_slot = _zero_other_layers((sf_ref, sb_ref), layer)

    def attention_outputs(b):
        rows = batch_rows[b]
        products[b] = _retention_products(rq_s.at[rows], rk_s.at[rows], rv_s.at[rows], intra_ref,
                                          kdec_ref, range(RET_HEADS), True)
        _ctx_attention_outputs(scores, rows.start, v_s, sink_ref, attn_ref, layer, seq_len)

    def retention_outputs(b):
        rows = batch_rows[b]

        def store_state(backward, h, state):
            (sb_ref if backward else sf_ref)[b, state_slot, h] = state

        _retention_outputs(products[b], rq_s.at[rows], gf_s.at[rows], gb_s.at[rows], cross_ref, gblk_ref,
                           gn_ref, ret_ref.at[rows], acc_s.at[rows], range(RET_HEADS),
                           lambda backward, h: None, store_state)

    stages = ([functools.partial(attention_outputs, b) for b in range(len(batch_rows))]
              + [functools.partial(retention_outputs, b) for b in range(len(batch_rows))])
    chunk = 2 * D_MODEL // len(stages)
    for n, stage in enumerate(stages):
        g_ref[:, n * chunk:(n + 1) * chunk] = proj(OFF_GA + n * chunk, chunk)
        stage()


def _context_mixer(x, mod, n1, w_in, sink, decay, gn, layer, seq_len, prev_outputs):
    n_tok = x.shape[0]
    n_batch = n_tok // seq_len
    per_tile = ROW_TILE // seq_len
    row = lambda w: pl.BlockSpec((ROW_TILE, w), lambda i: (i, 0))
    in_specs = [pl.BlockSpec(memory_space=pltpu.SMEM), row(D_MODEL), _mod_spec(None),
                _layer_spec((1, D_MODEL), layer), _const_spec((D_MODEL, IN_WIDTH)),
                *_decay_specs(layer, RET_HEADS, lambda i: 0), _layer_spec((1, RET_VW), layer)]
    args = [sink, x, mod, n1, w_in, *decay, gn]
    out_specs = [row(ATTN_W), row(RET_VW), row(2 * D_MODEL)]
    out_shape = [jax.ShapeDtypeStruct((n_tok, ATTN_W), BF16), jax.ShapeDtypeStruct((n_tok, RET_VW), BF16),
                 jax.ShapeDtypeStruct((n_tok, 2 * D_MODEL), F32)]
    cache_dims = (KV_HEADS, seq_len, HEAD_DIM)
    state_dims = (RET_HEADS, RET_DK, RET_DV)
    aliases = {}
    for idx, dims in enumerate((cache_dims, cache_dims, state_dims, state_dims)):
        if prev_outputs is None:
            assert layer == 0
            out_specs.append(pl.BlockSpec((per_tile, DEPTH) + dims, lambda i: (i, 0, 0, 0, 0)))
        else:
            aliases[len(args)] = len(out_shape)
            in_specs.append(pl.BlockSpec(memory_space=pl.ANY))
            args.append(prev_outputs[idx])
            out_specs.append(pl.BlockSpec((per_tile, 1) + dims, lambda i: (i, layer, 0, 0, 0)))
        out_shape.append(jax.ShapeDtypeStruct((n_batch, DEPTH) + dims, F32))
    assert len(out_shape) == N_CTX_MIX_OUT
    scratch = [pltpu.VMEM((ROW_TILE, ATTN_W), BF16), pltpu.VMEM((ROW_TILE, KV_W), BF16),
               pltpu.VMEM((ROW_TILE, KV_W), BF16), pltpu.VMEM((ROW_TILE, RET_W), BF16),
               pltpu.VMEM((ROW_TILE, RET_W), F32), pltpu.VMEM((ROW_TILE, RET_VW), BF16),
               pltpu.VMEM((ROW_TILE, RET_VW), F32), pltpu.VMEM((ROW_TILE, RET_VW), F32),
               pltpu.VMEM((ROW_TILE, RET_VW), F32)]
    assert len(scratch) == N_CTX_MIX_SCRATCH
    return pl.pallas_call(
        functools.partial(_ctx_mix_kernel, layer=layer, seq_len=seq_len),
        grid=(n_tok // ROW_TILE,),
        in_specs=in_specs,
        out_specs=out_specs,
        out_shape=out_shape,
        scratch_shapes=scratch,
        input_output_aliases=aliases,
        compiler_params=_params(1),
        name="context_mixer",
    )(*args)


def _post_kernel(*refs, final, n_cast, next_mod, own_gates):
    (attn_ref, ret_ref, g0_ref, g1_ref, x_ref, mod_ref, n2_ref,
     wba_ref, wbr_ref, wo_ref, w1_ref, w2_ref) = refs[:12]
    n_side_in = n_cast + (3 if next_mod else 0)
    n_in = 12 + (1 if final else 0) + n_side_in
    fn_ref = refs[12] if final else None
    out_ref = refs[n_in]
    side_in = refs[n_in - n_side_in:n_in]
    side_out = refs[n_in + 1:]
    for src, dst in zip(side_in[:n_cast], side_out[:n_cast]):
        dst[...] = src[...].astype(BF16)
    if next_mod:
        side_out[n_cast][...] = _modulation_columns(*side_in[n_cast:])
    mod = mod_ref[...]
    part = lambda k: mod[:, k * D_MODEL:(k + 1) * D_MODEL]
    if own_gates:
        hb = _normed_input(x_ref, mod_ref, g0_ref)
        gate_a = _dot(hb, g1_ref[:, OFF_GA:OFF_GA + D_MODEL])
        gate_r = _dot(hb, g1_ref[:, OFF_GA + D_MODEL:OFF_GA + 2 * D_MODEL])
    else:
        gate_a = g0_ref[...]
        gate_r = g1_ref[...]
    merged = (jax.nn.sigmoid(gate_a) * _dot(attn_ref[...], wba_ref[...])
              + jax.nn.sigmoid(gate_r) * _dot(ret_ref[...], wbr_ref[...]))
    x = x_ref[...] + part(2) * _dot(merged.astype(BF16), wo_ref[...])
    h2 = _modulated_norm(x, n2_ref[...], part(3), part(4)).astype(BF16)
    ff = None
    for c in range(0, D_FF, FF_CHUNK):
        a = jnp.maximum(_dot(h2, w1_ref[:, c:c + FF_CHUNK]), 0.0)
        t = _dot((a * a).astype(BF16), w2_ref[c:c + FF_CHUNK, :])
        ff = t if ff is None else ff + t
    x = x + part(5) * ff
    if final:
        x = x * lax.rsqrt(jnp.mean(x * x, axis=-1, keepdims=True) + NORM_EPS) * fn_ref[...]
    out_ref[...] = x


def _post(attn, ret, gates, x, mod, n2, weights, layer, seq_len, latent, final_gain, cast_next=(),
          mod_next=None):
    n_tok = x.shape[0]
    n_steps = n_tok // ROW_TILE
    tiles_per_batch = seq_len // ROW_TILE
    final = final_gain is not None
    own_gates = isinstance(gates, tuple)
    row = lambda w, c=0: pl.BlockSpec((ROW_TILE, w), lambda i, c=c: (i, c))
    if own_gates:
        gate_specs = [_layer_spec((1, D_MODEL), layer), _const_spec((D_MODEL, IN_WIDTH))]
        gate_args = list(gates)
    else:
        gate_specs = [row(D_MODEL, 0), row(D_MODEL, 1)]
        gate_args = [gates, gates]
    in_specs = [
        row(ATTN_W), row(RET_VW), *gate_specs, row(D_MODEL),
        _mod_spec(tiles_per_batch if latent else None),
        _layer_spec((1, D_MODEL), layer),
        *[_const_spec(w.shape) for w in weights],
    ]
    args = [attn, ret, *gate_args, x, mod, n2, *weights]
    if final:
        in_specs.append(_const_spec((1, D_MODEL)))
        args.append(final_gain)
    out_specs = [row(D_MODEL)]
    out_shape = [jax.ShapeDtypeStruct((n_tok, D_MODEL), F32)]
    for w in cast_next:
        _, k, n = w.shape
        rows = k // n_steps
        assert rows * n_steps == k and rows % 16 == 0
        in_specs.append(pl.BlockSpec((None, rows, n), lambda i: (layer + 1, i, 0)))
        args.append(w)
        out_specs.append(pl.BlockSpec((rows, n), lambda i: (i, 0)))
        out_shape.append(jax.ShapeDtypeStruct((k, n), BF16))
    if mod_next is not None:
        n_rows = mod_next[0].shape[0]
        mod_in_specs, mod_out_spec = _mod_specs(layer + 1, n_rows, MOD_W // n_steps)
        in_specs += mod_in_specs
        args += list(mod_next)
        out_specs.append(mod_out_spec)
        out_shape.append(jax.ShapeDtypeStruct((n_rows, MOD_W), F32))
    return pl.pallas_call(
        functools.partial(_post_kernel, final=final, n_cast=len(cast_next),
                          next_mod=mod_next is not None, own_gates=own_gates),
        grid=(n_steps,),
        in_specs=in_specs,
        out_specs=out_specs,
        out_shape=out_shape,
        compiler_params=_params(1),
        name="post_final" if final else "post",
    )(*args)


def _rope_tables(seq_len):
    t = np.arange(seq_len)
    rows = (t // GRID_W).astype(np.float64)
    cols = (t % GRID_W).astype(np.float64)

    def table(head_dim):
        half = head_dim // 2
        nf = half // 2
        inv = ROPE_BASE ** (-np.arange(nf, dtype=np.float64) / nf)
        lane = np.arange(LANES) % head_dim
        pos = np.where((lane < half)[None, :], rows[:, None], cols[:, None])
        ang = pos * inv[lane % nf][None, :]
        sign = np.where((lane % half) < nf, -1.0, 1.0)[None, :]
        return jnp.asarray(np.cos(ang), F32), jnp.asarray(np.sin(ang) * sign, F32)

    cos_a, sin_a = table(HEAD_DIM)
    cos_r, sin_r = table(RET_DK)
    return cos_a, sin_a, cos_r, sin_r


def kernel(x_prompt, x_sample, cache_k, cache_v, state_ret_fwd, state_ret_bwd, c, c_ctx, norm1_g, norm2_g, final_norm_g, w_mod, b_mod, w_in, attn_sink, ret_decay_fwd, ret_decay_bwd, ret_gn_g, w_branch_attn, w_branch_ret, w_out, w_ff1, w_ff2):
    n_ctx, ctx_len, _ = x_prompt.shape
    n_lat, lat_len, _ = x_sample.shape
    xp = x_prompt.reshape(n_ctx * ctx_len, D_MODEL)
    xs = x_sample.reshape(n_lat * lat_len, D_MODEL)

    cond = jnp.concatenate([c_ctx[None, :], c], axis=0)
    cond8 = jnp.pad(cond, ((0, -cond.shape[0] % 8), (0, 0)))
    mod_inputs = (cond8, w_mod, b_mod.reshape(DEPTH, 1, MOD_W))
    mod = _modulation(*mod_inputs, 0).reshape(cond8.shape[0], 1, MOD_W)
    cache_kt = jnp.swapaxes(cache_k, 3, 4)
    cache_vt = jnp.swapaxes(cache_v, 3, 4)
    dec_rows = jnp.broadcast_to(jnp.stack([ret_decay_fwd, ret_decay_bwd], axis=1)[..., None],
                                (DEPTH, 2, RET_HEADS, LANES))
    decay = _decay_tables(dec_rows)
    tables = _rope_tables(lat_len)
    final_gain = final_norm_g.reshape(1, D_MODEL)

    n1 = norm1_g.reshape(DEPTH, 1, D_MODEL)
    n2 = norm2_g.reshape(DEPTH, 1, D_MODEL)
    gn = ret_gn_g.reshape(DEPTH, 1, RET_VW)
    f32_weights = (w_in, w_branch_attn, w_branch_ret, w_out, w_ff1, w_ff2)
    w_in_b, *post_w = [w[0].astype(BF16) for w in f32_weights]

    ctx_outputs = None
    for l in range(DEPTH):
        last = l == DEPTH - 1
        attn, ret, gates, *ctx_outputs = _context_mixer(xp, mod, n1, w_in_b, attn_sink, decay, gn, l,
                                                        ctx_len, ctx_outputs)
        xp, *next_layer = _post(attn, ret, gates, xp, mod, n2, post_w, l, ctx_len, False,
                                final_gain if last else None, () if last else f32_weights,
                                None if last else mod_inputs)
        attn, ret = _latent_mixer(xs, mod, n1, w_in_b, tables, cache_kt, cache_vt, attn_sink, decay, gn,
                                  (state_ret_fwd, state_ret_bwd), l, lat_len)
        (xs,) = _post(attn, ret, (n1, w_in_b), xs, mod, n2, post_w, l, lat_len, True,
                      final_gain if last else None)
        if not last:
            w_in_b, *post_w, mod = next_layer
            mod = mod.reshape(cond8.shape[0], 1, MOD_W)

    return (xp.reshape(n_ctx, ctx_len, D_MODEL), xs.reshape(n_lat, lat_len, D_MODEL), *ctx_outputs)
```

```python
import functools

import jax
import jax.numpy as jnp
import numpy as np
from jax import lax
from jax.experimental import pallas as pl
from jax.experimental.pallas import tpu as pltpu

D_MODEL = 1024
DEPTH = 4
GRID_W = 64
N_HEADS = 8
HEAD_DIM = 64
KV_HEADS = 2
GROUP = N_HEADS // KV_HEADS
WINDOW = 128
Q_BLOCK = 128
RET_HEADS = 4
RET_DK = 128
RET_DV = 128
ATTN_W = N_HEADS * HEAD_DIM
KV_W = KV_HEADS * HEAD_DIM
RET_W = RET_HEADS * RET_DK
RET_VW = RET_HEADS * RET_DV
GATE_W = 2 * RET_VW + 2 * D_MODEL
IN_WIDTH = ATTN_W + 2 * KV_W + 2 * RET_W + RET_VW + GATE_W
D_FF = 4 * D_MODEL
MOD_W = 6 * D_MODEL
ROPE_BASE = 10000.0
NORM_EPS = 1e-6
NEG_INF = -1e30
LOG2_E = 1.4426950408889634

OFF_Q = 0
OFF_K = OFF_Q + ATTN_W
OFF_V = OFF_K + KV_W
OFF_RQ = OFF_V + KV_W
OFF_RK = OFF_RQ + RET_W
OFF_RV = OFF_RK + RET_W
OFF_RGF = OFF_RV + RET_VW
OFF_RGB = OFF_RGF + RET_VW
OFF_GA = OFF_RGB + RET_VW

LANES = 128
ROW_TILE = 512
RET_BLOCK = 256
LAT_BLOCKS_PER_STEP = 4
MOD_TILE_N = 1536
FF_CHUNK = 1024
VMEM_LIMIT = 58 * 1024 * 1024

BF16 = jnp.bfloat16
F32 = jnp.float32


def _params(n_axes):
    return pltpu.CompilerParams(dimension_semantics=("arbitrary",) * n_axes,
                                vmem_limit_bytes=VMEM_LIMIT)


def _const_spec(shape):
    nd = len(shape)
    return pl.BlockSpec(shape, lambda *_: (0,) * nd, pipeline_mode=pl.Buffered(1))


def _layer_spec(shape, layer):
    nd = len(shape)
    return pl.BlockSpec((None,) + tuple(shape), lambda *_: (layer,) + (0,) * nd,
                        pipeline_mode=pl.Buffered(1))


def _mod_spec(tiles_per_batch):
    if tiles_per_batch is None:
        return pl.BlockSpec((None, 1, MOD_W), lambda *_: (0, 0, 0))
    return pl.BlockSpec((None, 1, MOD_W), lambda i: (1 + i // tiles_per_batch, 0, 0))


def _zero_other_layers(out_refs, layer):
    n_slots = out_refs[0].shape[1]
    if n_slots == 1:
        return 0
    for ref in out_refs:
        for other in range(n_slots):
            if other != layer:
                ref[:, other] = jnp.zeros((ref.shape[0],) + ref.shape[2:], ref.dtype)
    return layer


def _dot(a, b):
    return jnp.dot(a, b, preferred_element_type=F32)


def _dot_nt(a, b):
    return lax.dot_general(a, b, (((1,), (1,)), ((), ())), preferred_element_type=F32)


def _dot_tn(a, b):
    return lax.dot_general(a, b, (((0,), (0,)), ((), ())), preferred_element_type=F32)


def _modulation_columns(cond_ref, w_ref, b_ref):
    cond = cond_ref[...]
    s = (cond * jax.nn.sigmoid(cond)).astype(BF16)
    return _dot(s, w_ref[...].astype(BF16)) + b_ref[...]


def _mod_kernel(cond_ref, w_ref, b_ref, out_ref):
    out_ref[...] = _modulation_columns(cond_ref, w_ref, b_ref)


def _mod_specs(layer, n_rows, tile_n):
    return ([pl.BlockSpec((n_rows, D_MODEL), lambda j: (0, 0)),
             pl.BlockSpec((None, D_MODEL, tile_n), lambda j: (layer, 0, j)),
             pl.BlockSpec((None, 1, tile_n), lambda j: (layer, 0, j))],
            pl.BlockSpec((n_rows, tile_n), lambda j: (0, j)))


def _modulation(cond8, w_mod, b_mod, layer):
    n_rows = cond8.shape[0]
    in_specs, out_spec = _mod_specs(layer, n_rows, MOD_TILE_N)
    return pl.pallas_call(
        _mod_kernel,
        grid=(MOD_W // MOD_TILE_N,),
        in_specs=in_specs,
        out_specs=out_spec,
        out_shape=jax.ShapeDtypeStruct((n_rows, MOD_W), F32),
        compiler_params=_params(1),
        name="adaln_modulation",
    )(cond8, w_mod, b_mod)


def _decay_kernel(dec_ref, intra_ref, cross_ref, kdec_ref, gblk_ref):
    backward = pl.program_id(1) == 1
    log_g = jnp.log(jax.nn.sigmoid(dec_ref[...]))
    n = RET_BLOCK
    row = lax.broadcasted_iota(jnp.int32, (n, n), 0).astype(F32)
    col = lax.broadcasted_iota(jnp.int32, (n, n), 1).astype(F32)
    diff = jnp.where(backward, col - row, row - col)
    row1 = row[:, :LANES]
    cross_pow = jnp.where(backward, n - row1, row1 + 1.0)
    kdec_pow = jnp.where(backward, row1, n - 1.0 - row1)
    for h in range(RET_HEADS):
        lg = log_g[h:h + 1, :]
        lg_wide = jnp.concatenate([lg] * (n // LANES), axis=1)
        intra_ref[h] = jnp.where(diff >= 0, jnp.exp(jnp.maximum(diff, 0.0) * lg_wide), 0.0)
        cross_ref[h] = jnp.exp(cross_pow * lg)
        kdec_ref[h] = jnp.exp(kdec_pow * lg)
        gblk_ref[h] = jnp.exp(n * lg)


def _decay_tables(dec_rows):
    lead = lambda *tail: pl.BlockSpec((None, None, RET_HEADS) + tail,
                                      lambda l, d: (l, d, 0) + (0,) * len(tail))
    shape = lambda *tail: jax.ShapeDtypeStruct((DEPTH, 2, RET_HEADS) + tail, F32)
    return pl.pallas_call(
        _decay_kernel,
        grid=(DEPTH, 2),
        in_specs=[lead(LANES)],
        out_specs=[lead(RET_BLOCK, RET_BLOCK), lead(RET_BLOCK, LANES), lead(RET_BLOCK, LANES),
                   lead(1, LANES)],
        out_shape=[shape(RET_BLOCK, RET_BLOCK), shape(RET_BLOCK, LANES), shape(RET_BLOCK, LANES),
                   shape(1, LANES)],
        compiler_params=_params(2),
        name="retention_decay_tables",
    )(dec_rows)


def _decay_specs(layer, n_heads, index_map):
    mode = {"pipeline_mode": pl.Buffered(1)} if n_heads == RET_HEADS else {}
    tab = lambda *tail: pl.BlockSpec((None, 2, n_heads) + tail,
                                     lambda *idx: (layer, 0, index_map(*idx)) + (0,) * len(tail), **mode)
    return [tab(RET_BLOCK, RET_BLOCK), tab(RET_BLOCK, LANES), tab(RET_BLOCK, LANES), tab(1, LANES)]


def _modulated_norm(x, gain, shift, scale):
    y = x * lax.rsqrt(jnp.mean(x * x, axis=-1, keepdims=True) + NORM_EPS) * gain
    return y * (1.0 + scale) + shift


def _normed_input(x_ref, mod_ref, n1_ref):
    mod = mod_ref[...]
    h = _modulated_norm(x_ref[...], n1_ref[...], mod[:, 0:D_MODEL], mod[:, D_MODEL:2 * D_MODEL])
    return h.astype(BF16)


def _rope_chunk(xc, cos, sin_signed, first_half, nf):
    partner = jnp.where(first_half, pltpu.roll(xc, LANES - nf, 1), pltpu.roll(xc, nf, 1))
    return xc * cos + partner * sin_signed


def _rope(p, cos, sin_signed, nf):
    lane = lax.broadcasted_iota(jnp.int32, (p.shape[0], LANES), 1)
    first_half = (lane % (2 * nf)) < nf
    parts = [_rope_chunk(p[:, c:c + LANES], cos, sin_signed, first_half, nf)
             for c in range(0, p.shape[1], LANES)]
    return parts[0] if len(parts) == 1 else jnp.concatenate(parts, axis=1)


def _softmax_t(dot_blocks, sink):
    blocks = [d * (HEAD_DIM ** -0.5 * LOG2_E) for d in dot_blocks]
    sink = sink * LOG2_E
    m = sink
    for s in blocks:
        m = jnp.maximum(jnp.max(s, axis=0, keepdims=True), m)
    es = [jnp.exp2(s - m) for s in blocks]
    denom = jnp.exp2(sink - m)
    for e in es:
        denom = denom + jnp.sum(e, axis=0, keepdims=True)
    return [e.astype(BF16) for e in es], 1.0 / denom


def _store_heads(o_ref, rows, heads, o_t, inv):
    for pair in range(0, len(heads), 2):
        halves = [o_t[:, g * Q_BLOCK:(g + 1) * Q_BLOCK] * inv[g] for g in (pair, pair + 1)]
        both = jnp.transpose(jnp.concatenate(halves, axis=0))
        h0 = heads[pair]
        o_ref[rows, h0 * HEAD_DIM:(h0 + 2) * HEAD_DIM] = both.astype(BF16)


def _ctx_attention_scores(q_ref, k_ref, seq_len):
    out = []
    for b in range(0, q_ref.shape[0], seq_len):
        for kh in range(KV_HEADS):
            for c in range(b, b + seq_len, Q_BLOCK):
                heads = [kh * GROUP + g for g in range(GROUP)]
                q4 = jnp.concatenate(
                    [q_ref[c:c + Q_BLOCK, h * HEAD_DIM:(h + 1) * HEAD_DIM] for h in heads], axis=0)
                keys = k_ref[b:b + seq_len, kh * HEAD_DIM:(kh + 1) * HEAD_DIM]
                out.append(((b, kh, c), _dot_nt(keys, q4)))
    return out


def _ctx_attention_outputs(scores, b, v_ref, sink_ref, o_ref, layer, seq_len):
    v_t = jnp.transpose(v_ref[b:b + seq_len, :].astype(F32)).astype(BF16)
    for (sb, kh, c), s4 in scores:
        if sb != b:
            continue
        heads = [kh * GROUP + g for g in range(GROUP)]
        es, inv = [], []
        for g, h in enumerate(heads):
            (e,), r = _softmax_t([s4[:, g * Q_BLOCK:(g + 1) * Q_BLOCK]], sink_ref[layer, h])
            es.append(e)
            inv.append(r)
        o_t = _dot(v_t[kh * HEAD_DIM:(kh + 1) * HEAD_DIM, :], jnp.concatenate(es, axis=1))
        _store_heads(o_ref, slice(c, c + Q_BLOCK), heads, o_t, inv)


def _lat_attention_scores(j, q_ref, k_ref, ckb_ref):
    n_blocks = k_ref.shape[0] // Q_BLOCK
    reach = WINDOW // Q_BLOCK
    first = max(j - reach, 0)
    band = (min(j + reach, n_blocks - 1) - first + 1) * Q_BLOCK
    start = first * Q_BLOCK
    kpos = start + lax.broadcasted_iota(jnp.int32, (band, Q_BLOCK), 0)
    qpos = j * Q_BLOCK + lax.broadcasted_iota(jnp.int32, (band, Q_BLOCK), 1)
    valid = jnp.abs(qpos - kpos) <= WINDOW
    kb = k_ref[start:start + band, :]
    rows = slice(j * Q_BLOCK, (j + 1) * Q_BLOCK)
    scores = []
    for kh in range(KV_HEADS):
        hd = slice(kh * HEAD_DIM, (kh + 1) * HEAD_DIM)
        heads = [kh * GROUP + g for g in range(GROUP)]
        q4 = jnp.concatenate([q_ref[rows, h * HEAD_DIM:(h + 1) * HEAD_DIM] for h in heads], axis=0)
        scores.append((_dot_nt(kb[:, hd], q4), _dot_nt(ckb_ref[kh], q4)))
    return first, valid, scores


def _lat_attention_outputs(j, first, valid, scores, vt_ref, cvt_ref, sink_ref, o_ref, layer):
    rows = slice(j * Q_BLOCK, (j + 1) * Q_BLOCK)
    for kh, (s_loc4, s_ctx4) in enumerate(scores):
        hd = slice(kh * HEAD_DIM, (kh + 1) * HEAD_DIM)
        heads = [kh * GROUP + g for g in range(GROUP)]
        e_loc, e_ctx, inv = [], [], []
        for g, h in enumerate(heads):
            cols = slice(g * Q_BLOCK, (g + 1) * Q_BLOCK)
            (el, ec), r = _softmax_t([jnp.where(valid, s_loc4[:, cols], NEG_INF), s_ctx4[:, cols]],
                                     sink_ref[layer, h])
            e_loc.append(el)
            e_ctx.append(ec)
            inv.append(r)
        e_loc4 = jnp.concatenate(e_loc, axis=1)
        o_t = _dot(cvt_ref[hd, :], jnp.concatenate(e_ctx, axis=1))
        for t in range(e_loc4.shape[0] // Q_BLOCK):
            o_t = o_t + _dot(vt_ref[first + t, hd, :], e_loc4[t * Q_BLOCK:(t + 1) * Q_BLOCK, :])
        _store_heads(o_ref, rows, heads, o_t, inv)


def _retention_products(rq_ref, rk_ref, rv_ref, intra_ref, kdec_ref, heads, final_state):
    n_blocks = rq_ref.shape[0] // RET_BLOCK
    blocks = [slice(i * RET_BLOCK, (i + 1) * RET_BLOCK) for i in range(n_blocks)]
    head_cols = {h: slice(h * RET_DK, (h + 1) * RET_DK) for h in heads}
    qk = {(h, i): _dot_nt(rq_ref[blocks[i], head_cols[h]], rk_ref[blocks[i], head_cols[h]].astype(BF16))
          for h in heads for i in range(n_blocks)}
    o_intra, kv = {}, {}
    for h in heads:
        for d in range(2):
            for i in range(n_blocks):
                v_i = rv_ref[blocks[i], head_cols[h]]
                o_intra[h, d, i] = _dot((qk[h, i] * intra_ref[d, h]).astype(BF16), v_i)
                last = i == (0 if d else n_blocks - 1)
                if final_state or not last:
                    k_i = rk_ref[blocks[i], head_cols[h]]
                    kv[h, d, i] = _dot_tn((k_i * kdec_ref[d, h]).astype(BF16), v_i)
    return o_intra, kv


def _retention_outputs(products, rq_ref, gf_ref, gb_ref, cross_ref, gblk_ref, gn_ref, ret_ref, acc_ref,
                       heads, initial_state, store_state):
    o_intra, kv = products
    n_blocks = rq_ref.shape[0] // RET_BLOCK
    blocks = [slice(i * RET_BLOCK, (i + 1) * RET_BLOCK) for i in range(n_blocks)]
    for h in heads:
        hs = slice(h * RET_DK, (h + 1) * RET_DK)
        for d, backward in enumerate((False, True)):
            g_ref = gb_ref if backward else gf_ref
            cross = cross_ref[d, h]
            g_block = gblk_ref[d, h]
            state = initial_state(backward, h)
            order = list(range(n_blocks))
            if backward:
                order.reverse()
            for i in order:
                o = o_intra[h, d, i]
                if state is not None:
                    o = o + _dot(rq_ref[blocks[i], hs], state.astype(BF16)) * cross
                if (h, d, i) in kv:
                    state = kv[h, d, i] if state is None else state * g_block + kv[h, d, i]
                mu = jnp.mean(o, axis=-1, keepdims=True)
                dev = o - mu
                var = jnp.mean(dev * dev, axis=-1, keepdims=True)
                y = dev * lax.rsqrt(var + NORM_EPS) * gn_ref[:, hs]
                gate = g_ref[blocks[i], hs]
                contrib = gate * jax.nn.sigmoid(gate) * y
                if backward:
                    ret_ref[blocks[i], hs] = (acc_ref[blocks[i], hs] + contrib).astype(BF16)
                else:
                    acc_ref[blocks[i], hs] = contrib
            if store_state is not None:
                store_state(backward, h, state)


def _lat_mix_kernel(sink_ref, x_ref, mod_ref, n1_ref, w_ref, cos_a, sin_a, cos_r, sin_r, ck_ref, cv_ref,
                    intra_ref, cross_ref, kdec_ref, gblk_ref, gn_ref, s0f_ref, s0b_ref,
                    attn_ref, ret_ref,
                    hb_s, q_s, k_s, vt_s, ckb_s, cvt_s, rq_s, rk_s, rv_s, gf_s, gb_s, acc_s, *, layer):
    seq_len = x_ref.shape[0]
    nf_a = HEAD_DIM // 4
    nf_r = RET_DK // 4

    hb_s[...] = _normed_input(x_ref, mod_ref, n1_ref)
    for r0 in range(0, seq_len, ROW_TILE):
        rows = slice(r0, r0 + ROW_TILE)
        hb = hb_s[rows, :]
        q_s[rows, :] = _rope(_dot(hb, w_ref[:, OFF_Q:OFF_Q + ATTN_W]), cos_a[rows, :], sin_a[rows, :],
                             nf_a).astype(BF16)
        k_s[rows, :] = _rope(_dot(hb, w_ref[:, OFF_K:OFF_K + KV_W]), cos_a[rows, :], sin_a[rows, :],
                             nf_a).astype(BF16)
        v = _dot(hb, w_ref[:, OFF_V:OFF_V + KV_W])
        for t in range(ROW_TILE // Q_BLOCK):
            vt_s[r0 // Q_BLOCK + t] = jnp.transpose(v[t * Q_BLOCK:(t + 1) * Q_BLOCK, :]).astype(BF16)
    cvt_s[...] = jnp.concatenate([cv_ref[0, 0, kh] for kh in range(KV_HEADS)], axis=0).astype(BF16)
    ck = jnp.transpose(jnp.concatenate([ck_ref[0, 0, kh] for kh in range(KV_HEADS)], axis=0))
    for kh in range(KV_HEADS):
        ckb_s[kh] = ck[:, kh * HEAD_DIM:(kh + 1) * HEAD_DIM].astype(BF16)

    def query_blocks(i, carry):
        js = [i * LAT_BLOCKS_PER_STEP + n for n in range(LAT_BLOCKS_PER_STEP)]
        attention = [_lat_attention_scores(j, q_s, k_s, ckb_s) for j in js]
        n_rows = LAT_BLOCKS_PER_STEP * Q_BLOCK
        rows = slice(i * n_rows, (i + 1) * n_rows)
        hb = hb_s[rows, :]
        cos = cos_r[rows, :]
        sin = sin_r[rows, :]
        rq_s[rows, :] = _rope(_dot(hb, w_ref[:, OFF_RQ:OFF_RQ + RET_W]), cos, sin, nf_r).astype(BF16)
        rk_s[rows, :] = _rope(_dot(hb, w_ref[:, OFF_RK:OFF_RK + RET_W]), cos, sin, nf_r) * (RET_DK ** -0.5)
        rv_s[rows, :] = _dot(hb, w_ref[:, OFF_RV:OFF_RV + RET_VW]).astype(BF16)
        gf_s[rows, :] = _dot(hb, w_ref[:, OFF_RGF:OFF_RGF + RET_VW])
        gb_s[rows, :] = _dot(hb, w_ref[:, OFF_RGB:OFF_RGB + RET_VW])
        for j, (first, valid, scores) in zip(js, attention):
            _lat_attention_outputs(j, first, valid, scores, vt_s, cvt_s, sink_ref, attn_ref, layer)
        return carry

    for i in range(seq_len // (LAT_BLOCKS_PER_STEP * Q_BLOCK)):
        query_blocks(i, 0)

    products = _retention_products(rq_s, rk_s, rv_s, intra_ref, kdec_ref, [0], False)
    for h in range(RET_HEADS):
        following = (_retention_products(rq_s, rk_s, rv_s, intra_ref, kdec_ref, [h + 1], False)
                     if h + 1 < RET_HEADS else None)
        _retention_outputs(products, rq_s, gf_s, gb_s, cross_ref, gblk_ref, gn_ref, ret_ref, acc_s, [h],
                           lambda backward, h: (s0b_ref if backward else s0f_ref)[0, 0, h], None)
        products = following


def _latent_mixer(x, mod, n1, w_in, tables, cache_k, cache_v, sink, decay, gn, states0, layer, seq_len):
    n_tok = x.shape[0]
    past_len = cache_k.shape[4]
    seq = lambda w, **kw: pl.BlockSpec((seq_len, w), lambda b: (b, 0), **kw)
    per_batch = lambda dims: pl.BlockSpec((1, 1) + dims, lambda b: (b, layer) + (0,) * len(dims))
    cache = per_batch((KV_HEADS, HEAD_DIM, past_len))
    state = per_batch((RET_HEADS, RET_DK, RET_DV))
    n_blocks = seq_len // Q_BLOCK
    scratch = [pltpu.VMEM((seq_len, D_MODEL), BF16),
               pltpu.VMEM((seq_len, ATTN_W), BF16), pltpu.VMEM((seq_len, KV_W), BF16),
               pltpu.VMEM((n_blocks, KV_W, Q_BLOCK), BF16),
               pltpu.VMEM((KV_HEADS, past_len, HEAD_DIM), BF16),
               pltpu.VMEM((KV_W, past_len), BF16),
               pltpu.VMEM((seq_len, RET_W), BF16), pltpu.VMEM((seq_len, RET_W), F32),
               pltpu.VMEM((seq_len, RET_VW), BF16),
               pltpu.VMEM((seq_len, RET_VW), F32), pltpu.VMEM((seq_len, RET_VW), F32),
               pltpu.VMEM((seq_len, RET_VW), F32)]
    return pl.pallas_call(
        functools.partial(_lat_mix_kernel, layer=layer),
        grid=(n_tok // seq_len,),
        in_specs=[pl.BlockSpec(memory_space=pltpu.SMEM), seq(D_MODEL),
                  _mod_spec(1),
                  _layer_spec((1, D_MODEL), layer), _const_spec((D_MODEL, IN_WIDTH)),
                  *[_const_spec((seq_len, LANES))] * 4, cache, cache,
                  *_decay_specs(layer, RET_HEADS, lambda b: 0), _layer_spec((1, RET_VW), layer),
                  state, state],
        out_specs=[seq(ATTN_W), seq(RET_VW)],
        out_shape=[jax.ShapeDtypeStruct((n_tok, ATTN_W), BF16), jax.ShapeDtypeStruct((n_tok, RET_VW), BF16)],
        scratch_shapes=scratch,
        compiler_params=_params(1),
        name="latent_mixer",
    )(sink, x, mod, n1, w_in, *tables, cache_k, cache_v, *decay, gn, *states0)


N_CTX_MIX_OUT = 7
N_CTX_MIX_SCRATCH = 9


def _ctx_mix_kernel(*refs, layer, seq_len):
    n_in = len(refs) - N_CTX_MIX_OUT - N_CTX_MIX_SCRATCH
    (sink_ref, x_ref, mod_ref, n1_ref, w_ref, intra_ref, cross_ref, kdec_ref, gblk_ref,
     gn_ref) = refs[:10]
    attn_ref, ret_ref, g_ref, kc_ref, vc_ref, sf_ref, sb_ref = refs[n_in:n_in + N_CTX_MIX_OUT]
    q_s, k_s, v_s, rq_s, rk_s, rv_s, gf_s, gb_s, acc_s = refs[n_in + N_CTX_MIX_OUT:]
    hb = _normed_input(x_ref, mod_ref, n1_ref)

    def proj(off, width):
        return _dot(hb, w_ref[:, off:off + width])

    batch_rows = [slice(b, b + seq_len) for b in range(0, x_ref.shape[0], seq_len)]
    q_s[...] = proj(OFF_Q, ATTN_W).astype(BF16)
    k = proj(OFF_K, KV_W)
    v = proj(OFF_V, KV_W)
    k_s[...] = k.astype(BF16)
    v_s[...] = v.astype(BF16)
    slot = _zero_other_layers((kc_ref, vc_ref), layer)
    for b, rows in enumerate(batch_rows):
        for kh in range(KV_HEADS):
            kc_ref[b, slot, kh] = k[rows, kh * HEAD_DIM:(kh + 1) * HEAD_DIM]
            vc_ref[b, slot, kh] = v[rows, kh * HEAD_DIM:(kh + 1) * HEAD_DIM]
    scores = _ctx_attention_scores(q_s, k_s, seq_len)
    rq_s[...] = proj(OFF_RQ, RET_W).astype(BF16)
    rk_s[...] = proj(OFF_RK, RET_W) * (RET_DK ** -0.5)
    rv_s[...] = proj(OFF_RV, RET_VW).astype(BF16)
    products = {}
    gf_s[...] = proj(OFF_RGF, RET_VW)
    gb_s[...] = proj(OFF_RGB, RET_VW)
    state_slot = _zero_other_layers((sf_ref, sb_ref), layer)

    def attention_outputs(b):
        rows = batch_rows[b]
        products[b] = _retention_products(rq_s.at[rows], rk_s.at[rows], rv_s.at[rows], intra_ref,
                                          kdec_ref, range(RET_HEADS), True)
        _ctx_attention_outputs(scores, rows.start, v_s, sink_ref, attn_ref, layer, seq_len)

    def retention_outputs(b):
        rows = batch_rows[b]

        def store_state(backward, h, state):
            (sb_ref if backward else sf_ref)[b, state_slot, h] = state

        _retention_outputs(products[b], rq_s.at[rows], gf_s.at[rows], gb_s.at[rows], cross_ref, gblk_ref,
                           gn_ref, ret_ref.at[rows], acc_s.at[rows], range(RET_HEADS),
                           lambda backward, h: None, store_state)

    stages = ([functools.partial(attention_outputs, b) for b in range(len(batch_rows))]
              + [functools.partial(retention_outputs, b) for b in range(len(batch_rows))])
    chunk = 2 * D_MODEL // len(stages)
    for n, stage in enumerate(stages):
        g_ref[:, n * chunk:(n + 1) * chunk] = proj(OFF_GA + n * chunk, chunk)
        stage()


def _context_mixer(x, mod, n1, w_in, sink, decay, gn, layer, seq_len, prev_outputs):
    n_tok = x.shape[0]
    n_batch = n_tok // seq_len
    per_tile = ROW_TILE // seq_len
    row = lambda w: pl.BlockSpec((ROW_TILE, w), lambda i: (i, 0))
    in_specs = [pl.BlockSpec(memory_space=pltpu.SMEM), row(D_MODEL), _mod_spec(None),
                _layer_spec((1, D_MODEL), layer), _const_spec((D_MODEL, IN_WIDTH)),
                *_decay_specs(layer, RET_HEADS, lambda i: 0), _layer_spec((1, RET_VW), layer)]
    args = [sink, x, mod, n1, w_in, *decay, gn]
    out_specs = [row(ATTN_W), row(RET_VW), row(2 * D_MODEL)]
    out_shape = [jax.ShapeDtypeStruct((n_tok, ATTN_W), BF16), jax.ShapeDtypeStruct((n_tok, RET_VW), BF16),
                 jax.ShapeDtypeStruct((n_tok, 2 * D_MODEL), F32)]
    cache_dims = (KV_HEADS, seq_len, HEAD_DIM)
    state_dims = (RET_HEADS, RET_DK, RET_DV)
    aliases = {}
    for idx, dims in enumerate((cache_dims, cache_dims, state_dims, state_dims)):
        if prev_outputs is None:
            assert layer == 0
            out_specs.append(pl.BlockSpec((per_tile, DEPTH) + dims, lambda i: (i, 0, 0, 0, 0)))
        else:
            aliases[len(args)] = len(out_shape)
            in_specs.append(pl.BlockSpec(memory_space=pl.ANY))
            args.append(prev_outputs[idx])
            out_specs.append(pl.BlockSpec((per_tile, 1) + dims, lambda i: (i, layer, 0, 0, 0)))
        out_shape.append(jax.ShapeDtypeStruct((n_batch, DEPTH) + dims, F32))
    assert len(out_shape) == N_CTX_MIX_OUT
    scratch = [pltpu.VMEM((ROW_TILE, ATTN_W), BF16), pltpu.VMEM((ROW_TILE, KV_W), BF16),
               pltpu.VMEM((ROW_TILE, KV_W), BF16), pltpu.VMEM((ROW_TILE, RET_W), BF16),
               pltpu.VMEM((ROW_TILE, RET_W), F32), pltpu.VMEM((ROW_TILE, RET_VW), BF16),
               pltpu.VMEM((ROW_TILE, RET_VW), F32), pltpu.VMEM((ROW_TILE, RET_VW), F32),
               pltpu.VMEM((ROW_TILE, RET_VW), F32)]
    assert len(scratch) == N_CTX_MIX_SCRATCH
    return pl.pallas_call(
        functools.partial(_ctx_mix_kernel, layer=layer, seq_len=seq_len),
        grid=(n_tok // ROW_TILE,),
        in_specs=in_specs,
        out_specs=out_specs,
        out_shape=out_shape,
        scratch_shapes=scratch,
        input_output_aliases=aliases,
        compiler_params=_params(1),
        name="context_mixer",
    )(*args)


def _post_kernel(*refs, final, n_cast, next_mod, own_gates):
    (attn_ref, ret_ref, g0_ref, g1_ref, x_ref, mod_ref, n2_ref,
     wba_ref, wbr_ref, wo_ref, w1_ref, w2_ref) = refs[:12]
    n_side_in = n_cast + (3 if next_mod else 0)
    n_in = 12 + (1 if final else 0) + n_side_in
    fn_ref = refs[12] if final else None
    out_ref = refs[n_in]
    side_in = refs[n_in - n_side_in:n_in]
    side_out = refs[n_in + 1:]
    for src, dst in zip(side_in[:n_cast], side_out[:n_cast]):
        dst[...] = src[...].astype(BF16)
    if next_mod:
        side_out[n_cast][...] = _modulation_columns(*side_in[n_cast:])
    mod = mod_ref[...]
    part = lambda k: mod[:, k * D_MODEL:(k + 1) * D_MODEL]
    if own_gates:
        hb = _normed_input(x_ref, mod_ref, g0_ref)
        gate_a = _dot(hb, g1_ref[:, OFF_GA:OFF_GA + D_MODEL])
        gate_r = _dot(hb, g1_ref[:, OFF_GA + D_MODEL:OFF_GA + 2 * D_MODEL])
    else:
        gate_a = g0_ref[...]
        gate_r = g1_ref[...]
    merged = (jax.nn.sigmoid(gate_a) * _dot(attn_ref[...], wba_ref[...])
              + jax.nn.sigmoid(gate_r) * _dot(ret_ref[...], wbr_ref[...]))
    x = x_ref[...] + part(2) * _dot(merged.astype(BF16), wo_ref[...])
    h2 = _modulated_norm(x, n2_ref[...], part(3), part(4)).astype(BF16)
    ff = None
    for c in range(0, D_FF, FF_CHUNK):
        a = jnp.maximum(_dot(h2, w1_ref[:, c:c + FF_CHUNK]), 0.0)
        t = _dot((a * a).astype(BF16), w2_ref[c:c + FF_CHUNK, :])
        ff = t if ff is None else ff + t
    x = x + part(5) * ff
    if final:
        x = x * lax.rsqrt(jnp.mean(x * x, axis=-1, keepdims=True) + NORM_EPS) * fn_ref[...]
    out_ref[...] = x


def _post(attn, ret, gates, x, mod, n2, weights, layer, seq_len, latent, final_gain, cast_next=(),
          mod_next=None):
    n_tok = x.shape[0]
    n_steps = n_tok // ROW_TILE
    tiles_per_batch = seq_len // ROW_TILE
    final = final_gain is not None
    own_gates = isinstance(gates, tuple)
    row = lambda w, c=0: pl.BlockSpec((ROW_TILE, w), lambda i, c=c: (i, c))
    if own_gates:
        gate_specs = [_layer_spec((1, D_MODEL), layer), _const_spec((D_MODEL, IN_WIDTH))]
        gate_args = list(gates)
    else:
        gate_specs = [row(D_MODEL, 0), row(D_MODEL, 1)]
        gate_args = [gates, gates]
    in_specs = [
        row(ATTN_W), row(RET_VW), *gate_specs, row(D_MODEL),
        _mod_spec(tiles_per_batch if latent else None),
        _layer_spec((1, D_MODEL), layer),
        *[_const_spec(w.shape) for w in weights],
    ]
    args = [attn, ret, *gate_args, x, mod, n2, *weights]
    if final:
        in_specs.append(_const_spec((1, D_MODEL)))
        args.append(final_gain)
    out_specs = [row(D_MODEL)]
    out_shape = [jax.ShapeDtypeStruct((n_tok, D_MODEL), F32)]
    for w in cast_next:
        _, k, n = w.shape
        rows = k // n_steps
        assert rows * n_steps == k and rows % 16 == 0
        in_specs.append(pl.BlockSpec((None, rows, n), lambda i: (layer + 1, i, 0)))
        args.append(w)
        out_specs.append(pl.BlockSpec((rows, n), lambda i: (i, 0)))
        out_shape.append(jax.ShapeDtypeStruct((k, n), BF16))
    if mod_next is not None:
        n_rows = mod_next[0].shape[0]
        mod_in_specs, mod_out_spec = _mod_specs(layer + 1, n_rows, MOD_W // n_steps)
        in_specs += mod_in_specs
        args += list(mod_next)
        out_specs.append(mod_out_spec)
        out_shape.append(jax.ShapeDtypeStruct((n_rows, MOD_W), F32))
    return pl.pallas_call(
        functools.partial(_post_kernel, final=final, n_cast=len(cast_next),
                          next_mod=mod_next is not None, own_gates=own_gates),
        grid=(n_steps,),
        in_specs=in_specs,
        out_specs=out_specs,
        out_shape=out_shape,
        compiler_params=_params(1),
        name="post_final" if final else "post",
    )(*args)


def _rope_tables(seq_len):
    t = np.arange(seq_len)
    rows = (t // GRID_W).astype(np.float64)
    cols = (t % GRID_W).astype(np.float64)

    def table(head_dim):
        half = head_dim // 2
        nf = half // 2
        inv = ROPE_BASE ** (-np.arange(nf, dtype=np.float64) / nf)
        lane = np.arange(LANES) % head_dim
        pos = np.where((lane < half)[None, :], rows[:, None], cols[:, None])
        ang = pos * inv[lane % nf][None, :]
        sign = np.where((lane % half) < nf, -1.0, 1.0)[None, :]
        return jnp.asarray(np.cos(ang), F32), jnp.asarray(np.sin(ang) * sign, F32)

    cos_a, sin_a = table(HEAD_DIM)
    cos_r, sin_r = table(RET_DK)
    return cos_a, sin_a, cos_r, sin_r


def kernel(x_prompt, x_sample, cache_k, cache_v, state_ret_fwd, state_ret_bwd, c, c_ctx, norm1_g, norm2_g, final_norm_g, w_mod, b_mod, w_in, attn_sink, ret_decay_fwd, ret_decay_bwd, ret_gn_g, w_branch_attn, w_branch_ret, w_out, w_ff1, w_ff2):
    n_ctx, ctx_len, _ = x_prompt.shape
    n_lat, lat_len, _ = x_sample.shape
    xp = x_prompt.reshape(n_ctx * ctx_len, D_MODEL)
    xs = x_sample.reshape(n_lat * lat_len, D_MODEL)

    cond = jnp.concatenate([c_ctx[None, :], c], axis=0)
    cond8 = jnp.pad(cond, ((0, -cond.shape[0] % 8), (0, 0)))
    mod_inputs = (cond8, w_mod, b_mod.reshape(DEPTH, 1, MOD_W))
    mod = _modulation(*mod_inputs, 0).reshape(cond8.shape[0], 1, MOD_W)
    cache_kt = jnp.swapaxes(cache_k, 3, 4)
    cache_vt = jnp.swapaxes(cache_v, 3, 4)
    dec_rows = jnp.broadcast_to(jnp.stack([ret_decay_fwd, ret_decay_bwd], axis=1)[..., None],
                                (DEPTH, 2, RET_HEADS, LANES))
    decay = _decay_tables(dec_rows)
    tables = _rope_tables(lat_len)
    final_gain = final_norm_g.reshape(1, D_MODEL)

    n1 = norm1_g.reshape(DEPTH, 1, D_MODEL)
    n2 = norm2_g.reshape(DEPTH, 1, D_MODEL)
    gn = ret_gn_g.reshape(DEPTH, 1, RET_VW)
    f32_weights = (w_in, w_branch_attn, w_branch_ret, w_out, w_ff1, w_ff2)
    w_in_b, *post_w = [w[0].astype(BF16) for w in f32_weights]

    ctx_outputs = None
    for l in range(DEPTH):
        last = l == DEPTH - 1
        attn, ret, gates, *ctx_outputs = _context_mixer(xp, mod, n1, w_in_b, attn_sink, decay, gn, l,
                                                        ctx_len, ctx_outputs)
        xp, *next_layer = _post(attn, ret, gates, xp, mod, n2, post_w, l, ctx_len, False,
                                final_gain if last else None, () if last else f32_weights,
                                None if last else mod_inputs)
        attn, ret = _latent_mixer(xs, mod, n1, w_in_b, tables, cache_kt, cache_vt, attn_sink, decay, gn,
                                  (state_ret_fwd, state_ret_bwd), l, lat_len)
        (xs,) = _post(attn, ret, (n1, w_in_b), xs, mod, n2, post_w, l, lat_len, True,
                      final_gain if last else None)
        if not last:
            w_in_b, *post_w, mod = next_layer
            mod = mod.reshape(cond8.shape[0], 1, MOD_W)

    return (xp.reshape(n_ctx, ctx_len, D_MODEL), xs.reshape(n_lat, lat_len, D_MODEL), *ctx_outputs)
```
